```python
import jax, jax.numpy as jnp
from jax import lax
import numpy as np

D_MODEL = 1024
BATCH = 4
SEQ = 4096
DEPTH = 4
DEC_BATCH = 128
DEC_SEQ = 1
PAST_LEN = 8192
PAGE_SIZE = 128

N_MIXERS = 2
N_A_LAYERS = (DEPTH + 1) // 2
N_B_LAYERS = DEPTH // 2
N_DENSE_LAYERS = (DEPTH + 1) // 2
N_MOE_LAYERS = DEPTH // 2
CHUNK = 128
SGU_WIDTH = 2 * D_MODEL
SGU_GROUPS = 8
SGU_GROUP_DIM = SGU_WIDTH // SGU_GROUPS
WINDOW = 128
BLOCK = 128
HEAD_DIM = 64
N_HEADS = D_MODEL // HEAD_DIM
N_KV_HEADS = max(1, N_HEADS // 8)
GQA_GROUP = N_HEADS // N_KV_HEADS
Q_DIM = N_HEADS * HEAD_DIM
KV_DIM = N_KV_HEADS * HEAD_DIM
D_FF = 2816
N_EXPERTS = 8
TOP_K = 2
D_FF_EXPERT = 3584
EPS = 1e-6
LN_EPS = 1e-5

kernel_name = 'hybrid_sgu_swa_sink_alibi_moe_decode_step'


def rmsnorm(x, g):
    xf = x.astype(jnp.float32)
    y = xf * lax.rsqrt(jnp.mean(xf * xf, axis=-1, keepdims=True) + EPS)
    return (y * g.astype(jnp.float32)).astype(x.dtype)


def layernorm(x, g, b):
    xf = x.astype(jnp.float32)
    xc = xf - jnp.mean(xf, axis=-1, keepdims=True)
    var = jnp.mean(xc * xc, axis=-1, keepdims=True)
    return (xc * lax.rsqrt(var + LN_EPS) * g.astype(jnp.float32) + b.astype(jnp.float32)).astype(x.dtype)


def alibi_slopes():
    h = jnp.arange(1, N_HEADS + 1, dtype=jnp.float32)
    return jnp.exp2(-8.0 * h / N_HEADS).reshape(N_KV_HEADS, GQA_GROUP)


def chunk_sgu(h, w_in, b_in, ln_g, ln_b, w_s, b_s, w_out):
    n, L, _ = h.shape
    z = jax.nn.gelu(h @ w_in + b_in)
    u, v = jnp.split(z, 2, axis=-1)
    v = layernorm(v, ln_g, ln_b)
    cl = CHUNK if L > CHUNK else L
    n_chunks = -(-L // cl)
    pad = n_chunks * cl - L
    vc = jnp.pad(v, ((0, 0), (0, pad), (0, 0))).reshape(n, n_chunks, cl, SGU_GROUPS, SGU_GROUP_DIM)
    mask = jnp.tril(jnp.ones((cl, cl), dtype=bool))
    ws = jnp.where(mask, w_s[:, :cl, :cl], 0.0)
    gate = jnp.einsum('gts,ncsgd->nctgd', ws, vc) + b_s[:, :cl].T[:, :, None]
    gate = gate.reshape(n, n_chunks * cl, SGU_WIDTH)[:, :L]
    y = (u * gate) @ w_out
    start = ((L - 1) // CHUNK) * CHUNK
    return y, v[:, start:]


def swa_qkv(h, w_qkv):
    n, L, _ = h.shape
    q, k, v = jnp.split(h @ w_qkv, [Q_DIM, Q_DIM + KV_DIM], axis=-1)
    return (q.reshape(n, L, N_KV_HEADS, GQA_GROUP, HEAD_DIM),
            k.reshape(n, L, N_KV_HEADS, HEAD_DIM),
            v.reshape(n, L, N_KV_HEADS, HEAD_DIM))


def sink_attention(q, k, v, dist, valid, slopes, sinks):
    logits = jnp.einsum('...thgd,...shd->...hgts', q, k).astype(jnp.float32) * (HEAD_DIM ** -0.5)
    logits = logits - slopes[:, :, None, None] * dist
    logits = jnp.where(valid, logits, -jnp.inf)
    sink = jnp.broadcast_to(sinks.astype(jnp.float32).reshape(N_KV_HEADS, GQA_GROUP)[:, :, None, None],
                            logits.shape[:-1] + (1,))
    probs = jax.nn.softmax(jnp.concatenate([logits, sink], axis=-1), axis=-1)[..., :-1]
    return jnp.einsum('...hgts,...shd->...thgd', probs.astype(v.dtype), v)


def _pad_time(a, pad):
    return jnp.pad(a, ((0, 0), (0, pad)) + ((0, 0),) * (a.ndim - 2))


def _with_prev_block(a):
    prev = jnp.concatenate([jnp.zeros_like(a[:, :1]), a[:, :-1]], axis=1)
    return jnp.concatenate([prev, a], axis=2)


def swa_prompt(h, w_qkv, w_o, sinks, slopes):
    n, L, _ = h.shape
    q, k, v = swa_qkv(h, w_qkv)
    nb = -(-L // BLOCK)
    pad = nb * BLOCK - L
    qb = _pad_time(q, pad).reshape(n, nb, BLOCK, N_KV_HEADS, GQA_GROUP, HEAD_DIM)
    kb = _pad_time(k, pad).reshape(n, nb, BLOCK, N_KV_HEADS, HEAD_DIM)
    vb = _pad_time(v, pad).reshape(n, nb, BLOCK, N_KV_HEADS, HEAD_DIM)
    t = jnp.arange(BLOCK)
    s = jnp.arange(2 * BLOCK)
    dist = BLOCK + t[:, None] - s[None, :]
    kpos = (jnp.arange(nb)[:, None] - 1) * BLOCK + s[None, :]
    valid = ((dist >= 0) & (dist <= WINDOW))[None] & (kpos >= 0)[:, None, :]
    o = sink_attention(qb, _with_prev_block(kb), _with_prev_block(vb),
                       dist.astype(jnp.float32), valid[:, None, None], slopes, sinks)
    o = o.reshape(n, nb * BLOCK, Q_DIM)[:, :L]
    rows = min(WINDOW, L)
    return o @ w_o, k[:, L - rows:], v[:, L - rows:]


def swa_sample(h, k_cache, v_cache, w_qkv, w_o, sinks, slopes):
    n, L, _ = h.shape
    q, k, v = swa_qkv(h, w_qkv)
    R = k_cache.shape[1]
    kk = jnp.concatenate([k_cache.astype(k.dtype), k], axis=1)
    vv = jnp.concatenate([v_cache.astype(v.dtype), v], axis=1)
    qpos = PAST_LEN + jnp.arange(L)
    kpos = PAST_LEN - R + jnp.arange(R + L)
    dist = qpos[:, None] - kpos[None, :]
    valid = (dist >= 0) & (dist <= WINDOW)
    o = sink_attention(q, kk, vv, dist.astype(jnp.float32), valid, slopes, sinks).reshape(n, L, Q_DIM)
    rows = min(WINDOW, PAST_LEN + L)
    return o @ w_o, kk[:, R + L - rows:], vv[:, R + L - rows:]


def swiglu(h, w_gate, w_up, w_down):
    return (jax.nn.silu(h @ w_gate) * (h @ w_up)) @ w_down


def moe_swiglu(h, w_router, w_gate, w_up, w_down):
    logits = (h @ w_router).astype(jnp.float32)
    top_val, top_idx = lax.top_k(logits, TOP_K)
    top_w = jax.nn.softmax(top_val, axis=-1)
    gates = jnp.einsum('...k,...ke->...e', top_w,
                       jax.nn.one_hot(top_idx, N_EXPERTS, dtype=jnp.float32)).astype(h.dtype)
    out = jnp.zeros_like(h)
    for e in range(N_EXPERTS):
        out = out + gates[..., e:e + 1] * swiglu(h, w_gate[e], w_up[e], w_down[e])
    return out


def setup_inputs(seed: int = 0) -> dict:
    key = jax.random.key(seed)
    ks = jax.random.split(key, 28)
    f32 = jnp.float32

    def nrm(k, shape, scale):
        return jax.random.normal(k, shape, f32) * scale

    rows = min(WINDOW, PAST_LEN)
    return {
        'x_prompt': nrm(ks[0], (BATCH, SEQ, D_MODEL), 1.0),
        'x_sample': nrm(ks[1], (DEC_BATCH, DEC_SEQ, D_MODEL), 1.0),
        'cache_swa_k': nrm(ks[2], (N_B_LAYERS, DEC_BATCH, rows, N_KV_HEADS, HEAD_DIM), 1.0),
        'cache_swa_v': nrm(ks[3], (N_B_LAYERS, DEC_BATCH, rows, N_KV_HEADS, HEAD_DIM), 1.0),
        'norm_mix': 1.0 + nrm(ks[4], (DEPTH, D_MODEL), 0.02),
        'norm_ffn': 1.0 + nrm(ks[5], (DEPTH, D_MODEL), 0.02),
        'norm_final': 1.0 + nrm(ks[6], (D_MODEL,), 0.02),
        'sgu_w_in': nrm(ks[7], (N_A_LAYERS, D_MODEL, 2 * SGU_WIDTH), D_MODEL ** -0.5),
        'sgu_b_in': nrm(ks[8], (N_A_LAYERS, 2 * SGU_WIDTH), 0.02),
        'sgu_ln_g': 1.0 + nrm(ks[9], (N_A_LAYERS, SGU_WIDTH), 0.02),
        'sgu_ln_b': nrm(ks[10], (N_A_LAYERS, SGU_WIDTH), 0.02),
        'sgu_w_s': nrm(ks[11], (N_A_LAYERS, SGU_GROUPS, CHUNK, CHUNK), CHUNK ** -0.5),
        'sgu_b_s': 1.0 + nrm(ks[12], (N_A_LAYERS, SGU_GROUPS, CHUNK), 0.1),
        'sgu_w_out': nrm(ks[13], (N_A_LAYERS, SGU_WIDTH, D_MODEL), SGU_WIDTH ** -0.5),
        'attn_w_qkv': nrm(ks[14], (N_B_LAYERS, D_MODEL, Q_DIM + 2 * KV_DIM), D_MODEL ** -0.5),
        'attn_sinks': nrm(ks[15], (N_B_LAYERS, N_HEADS), 0.5),
        'attn_w_o': nrm(ks[16], (N_B_LAYERS, Q_DIM, D_MODEL), Q_DIM ** -0.5),
        'ffn_w_gate': nrm(ks[17], (N_DENSE_LAYERS, D_MODEL, D_FF), D_MODEL ** -0.5),
        'ffn_w_up': nrm(ks[18], (N_DENSE_LAYERS, D_MODEL, D_FF), D_MODEL ** -0.5),
        'ffn_w_down': nrm(ks[19], (N_DENSE_LAYERS, D_FF, D_MODEL), D_FF ** -0.5),
        'moe_w_router': nrm(ks[20], (N_MOE_LAYERS, D_MODEL, N_EXPERTS), D_MODEL ** -0.5),
        'moe_w_gate': nrm(ks[21], (N_MOE_LAYERS, N_EXPERTS, D_MODEL, D_FF_EXPERT), D_MODEL ** -0.5),
        'moe_w_up': nrm(ks[22], (N_MOE_LAYERS, N_EXPERTS, D_MODEL, D_FF_EXPERT), D_MODEL ** -0.5),
        'moe_w_down': nrm(ks[23], (N_MOE_LAYERS, N_EXPERTS, D_FF_EXPERT, D_MODEL), D_FF_EXPERT ** -0.5),
    }


def reference(x_prompt, x_sample, cache_swa_k, cache_swa_v, norm_mix, norm_ffn, norm_final,
              sgu_w_in, sgu_b_in, sgu_ln_g, sgu_ln_b, sgu_w_s, sgu_b_s, sgu_w_out,
              attn_w_qkv, attn_sinks, attn_w_o,
              ffn_w_gate, ffn_w_up, ffn_w_down,
              moe_w_router, moe_w_gate, moe_w_up, moe_w_down):
    slopes = alibi_slopes()
    xp, xs = x_prompt, x_sample
    sgu_v_p, sgu_v_s = [], []
    k_p, v_p, k_s, v_s = [], [], [], []
    for i in range(DEPTH):
        j = i // N_MIXERS
        hp = rmsnorm(xp, norm_mix[i])
        hs = rmsnorm(xs, norm_mix[i])
        if i % N_MIXERS == 0:
            sgu_args = (sgu_w_in[j], sgu_b_in[j], sgu_ln_g[j], sgu_ln_b[j], sgu_w_s[j], sgu_b_s[j], sgu_w_out[j])
            yp, vrow_p = chunk_sgu(hp, *sgu_args)
            ys, vrow_s = chunk_sgu(hs, *sgu_args)
            sgu_v_p.append(vrow_p)
            sgu_v_s.append(vrow_s)
        else:
            yp, kp_new, vp_new = swa_prompt(hp, attn_w_qkv[j], attn_w_o[j], attn_sinks[j], slopes)
            ys, ks_new, vs_new = swa_sample(hs, cache_swa_k[j], cache_swa_v[j],
                                            attn_w_qkv[j], attn_w_o[j], attn_sinks[j], slopes)
            k_p.append(kp_new)
            v_p.append(vp_new)
            k_s.append(ks_new)
            v_s.append(vs_new)
        xp = xp + yp
        xs = xs + ys
        hp = rmsnorm(xp, norm_ffn[i])
        hs = rmsnorm(xs, norm_ffn[i])
        if i % 2 == 0:
            xp = xp + swiglu(hp, ffn_w_gate[j], ffn_w_up[j], ffn_w_down[j])
            xs = xs + swiglu(hs, ffn_w_gate[j], ffn_w_up[j], ffn_w_down[j])
        else:
            xp = xp + moe_swiglu(hp, moe_w_router[j], moe_w_gate[j], moe_w_up[j], moe_w_down[j])
            xs = xs + moe_swiglu(hs, moe_w_router[j], moe_w_gate[j], moe_w_up[j], moe_w_down[j])
    y_prompt = rmsnorm(xp, norm_final)
    y_sample = rmsnorm(xs, norm_final)
    return (y_prompt, y_sample, jnp.stack(sgu_v_p), jnp.stack(sgu_v_s),
            jnp.stack(k_p), jnp.stack(v_p), jnp.stack(k_s), jnp.stack(v_s))
```

```python
import functools

import numpy as np
import jax
import jax.numpy as jnp
from jax import lax
from jax.experimental import pallas as pl
from jax.experimental.pallas import tpu as pltpu

D_MODEL = 1024
BATCH = 4
SEQ = 4096
DEPTH = 4
DEC_BATCH = 128
PAST_LEN = 8192
CHUNK = 128
SGU_WIDTH = 2 * D_MODEL
SGU_GROUPS = 8
SGU_GROUP_DIM = SGU_WIDTH // SGU_GROUPS
WINDOW = 128
BLOCK = 128
HEAD_DIM = 64
N_HEADS = D_MODEL // HEAD_DIM
N_KV_HEADS = 2
GQA_GROUP = N_HEADS // N_KV_HEADS
Q_DIM = N_HEADS * HEAD_DIM
KV_DIM = N_KV_HEADS * HEAD_DIM
D_FF = 2816
N_EXPERTS = 8
D_FF_EXPERT = 3584
EPS = 1e-6
LN_EPS = 1e-5

F32 = jnp.float32
BF16 = jnp.bfloat16

LANES = 128
TM = 512
N_PROMPT = BATCH * SEQ
T_REAL = N_PROMPT + DEC_BATCH
N_TILES = -(-T_REAL // TM)
T_PAD = N_TILES * TM
PROMPT_TILES = N_PROMPT // TM
TILES_PER_SEQ = SEQ // TM
CHUNKS_PER_TILE = TM // CHUNK
N_PAIRS = N_HEADS // 2

TMM = 512
TF = 512
NF = D_FF_EXPERT // TF
N_SLOTS = 2 * T_REAL
NT_MOE = (N_SLOTS + N_EXPERTS * (TMM - 1)) // TMM + 1
P_TOTAL = NT_MOE * TMM

VMEM_LIMIT = 56 * 1024 * 1024

_SLOPES = [2.0 ** (-8.0 * (h + 1) / N_HEADS) for h in range(N_HEADS)]


def _rms(x, g):
    return x * lax.rsqrt(jnp.mean(x * x, axis=-1, keepdims=True) + EPS) * g


def _gelu(x):
    c = np.sqrt(2.0 / np.pi).astype(np.float32)
    return x * (0.5 * (1.0 + jnp.tanh(c * (x + 0.044715 * (x * x * x)))))


def _silu(x):
    return x * (1.0 / (1.0 + jnp.exp(-x)))


def _dot(a, b):
    return jnp.dot(a, b, preferred_element_type=F32)


def _dot_t(a, b):
    return lax.dot_general(a, b, (((1,), (1,)), ((), ())), preferred_element_type=F32)


def _const_spec(shape):
    nd = len(shape)
    return pl.BlockSpec(shape, lambda *_: (0,) * nd, pipeline_mode=pl.Buffered(1))


def _params(n_axes=1):
    return pltpu.CompilerParams(dimension_semantics=("arbitrary",) * n_axes,
                                vmem_limit_bytes=VMEM_LIMIT)


def _sgu_kernel(x_ref, g_ref, win_ref, bin_ref, lng_ref, lnb_ref, ws_ref, bs_ref, wout_ref,
                xo_ref, vlast_ref):
    i = pl.program_id(0)
    is_sample = i == N_TILES - 1
    x = x_ref[...]
    h = _rms(x, g_ref[...]).astype(BF16)

    v = _gelu(_dot(h, win_ref[:, SGU_WIDTH:]) + bin_ref[:, SGU_WIDTH:])
    mu = jnp.mean(v, axis=-1, keepdims=True)
    vc = v - mu
    var = jnp.mean(vc * vc, axis=-1, keepdims=True)
    vn = vc * lax.rsqrt(var + LN_EPS) * lng_ref[...] + lnb_ref[...]

    @pl.when(is_sample)
    def _():
        vlast_ref[...] = vn[:CHUNK]

    @pl.when(jnp.logical_not(is_sample))
    def _():
        vlast_ref[...] = vn[TM - CHUNK:]

    vb = vn.astype(BF16)
    row = lax.broadcasted_iota(jnp.int32, (CHUNK, CHUNK), 0)
    col = lax.broadcasted_iota(jnp.int32, (CHUNK, CHUNK), 1)
    sample_flag = jnp.where(is_sample, 1, 0)
    sample_mat = (jnp.zeros((CHUNK, CHUNK), jnp.int32) + sample_flag) == 1
    sample_col = (jnp.zeros((CHUNK, 1), jnp.int32) + sample_flag) == 1
    y = jnp.zeros((TM, D_MODEL), F32)
    for g in range(SGU_GROUPS):
        lo, hi = g * SGU_GROUP_DIM, (g + 1) * SGU_GROUP_DIM
        w = ws_ref[g]
        w_tril = jnp.where(row >= col, w, 0.0)
        w_diag = jnp.where(row == col, w[0:1, 0:1], 0.0)
        w_eff = jnp.where(sample_mat, w_diag, w_tril).astype(BF16)
        b = bs_ref[g]
        b_eff = jnp.where(sample_col, b[0:1, 0:1], b)
        u = _gelu(_dot(h, win_ref[:, lo:hi]) + bin_ref[:, lo:hi])
        gate = jnp.concatenate(
            [_dot(w_eff, vb[c * CHUNK:(c + 1) * CHUNK, lo:hi]) + b_eff
             for c in range(CHUNKS_PER_TILE)], axis=0)
        y = y + _dot((u * gate).astype(BF16), wout_ref[lo:hi, :])
    xo_ref[...] = x + y


def _sgu_layer(x, g, w_in, b_in, ln_g, ln_b, w_s, b_s, w_out):
    tile = pl.BlockSpec((TM, D_MODEL), lambda i: (i, 0))
    vlast_spec = pl.BlockSpec(
        (CHUNK, SGU_WIDTH),
        lambda i: (jnp.where(i == N_TILES - 1, BATCH, i // TILES_PER_SEQ), 0))
    return pl.pallas_call(
        _sgu_kernel,
        grid=(N_TILES,),
        in_specs=[tile,
                  _const_spec((1, D_MODEL)),
                  _const_spec((D_MODEL, 2 * SGU_WIDTH)),
                  _const_spec((1, 2 * SGU_WIDTH)),
                  _const_spec((1, SGU_WIDTH)),
                  _const_spec((1, SGU_WIDTH)),
                  _const_spec((SGU_GROUPS, CHUNK, CHUNK)),
                  _const_spec((SGU_GROUPS, CHUNK, 1)),
                  _const_spec((SGU_WIDTH, D_MODEL))],
        out_specs=[tile, vlast_spec],
        out_shape=[jax.ShapeDtypeStruct((T_PAD, D_MODEL), F32),
                   jax.ShapeDtypeStruct(((BATCH + 1) * CHUNK, SGU_WIDTH), F32)],
        compiler_params=_params(),
        name="sgu_mixer",
    )(x, g.reshape(1, -1), w_in.astype(BF16), b_in.reshape(1, -1), ln_g.reshape(1, -1),
      ln_b.reshape(1, -1), w_s, b_s.reshape(SGU_GROUPS, CHUNK, 1), w_out.astype(BF16))


def _ffn_kernel(x_ref, g_ref, wg_ref, wu_ref, wd_ref, xo_ref):
    x = x_ref[...]
    h = _rms(x, g_ref[...]).astype(BF16)
    a = (_silu(_dot(h, wg_ref[...])) * _dot(h, wu_ref[...])).astype(BF16)
    xo_ref[...] = x + _dot(a, wd_ref[...])


def _ffn_layer(x, g, w_gate, w_up, w_down):
    tile = pl.BlockSpec((TM, D_MODEL), lambda i: (i, 0))
    return pl.pallas_call(
        _ffn_kernel,
        grid=(N_TILES,),
        in_specs=[tile,
                  _const_spec((1, D_MODEL)),
                  _const_spec((D_MODEL, D_FF)),
                  _const_spec((D_MODEL, D_FF)),
                  _const_spec((D_FF, D_MODEL))],
        out_specs=tile,
        out_shape=jax.ShapeDtypeStruct((T_PAD, D_MODEL), F32),
        compiler_params=_params(),
        name="dense_swiglu",
    )(x, g.reshape(1, -1), w_gate.astype(BF16), w_up.astype(BF16), w_down.astype(BF16))


def _qkv_kernel(x_ref, g_ref, w_ref, q_ref, kv_ref):
    h = _rms(x_ref[...], g_ref[...]).astype(BF16)
    qkv = _dot(h, w_ref[...])
    q_ref[...] = (qkv[:, :Q_DIM] * (HEAD_DIM ** -0.5)).astype(BF16)
    kv_ref[...] = qkv[:, Q_DIM:]


def _qkv_layer(x, g, w_qkv):
    tile = pl.BlockSpec((TM, D_MODEL), lambda i: (i, 0))
    return pl.pallas_call(
        _qkv_kernel,
        grid=(N_TILES,),
        in_specs=[tile, _const_spec((1, D_MODEL)), _const_spec((D_MODEL, Q_DIM + 2 * KV_DIM))],
        out_specs=[tile, pl.BlockSpec((TM, 2 * KV_DIM), lambda i: (i, 0))],
        out_shape=[jax.ShapeDtypeStruct((T_PAD, Q_DIM), BF16),
                   jax.ShapeDtypeStruct((T_PAD, 2 * KV_DIM), F32)],
        compiler_params=_params(),
        name="swa_qkv",
    )(x, g.reshape(1, -1), w_qkv.astype(BF16))


def _half_masks(x):
    lane = lax.broadcasted_iota(jnp.int32, x.shape, 1)
    low = lane < HEAD_DIM
    xr = pltpu.roll(x, HEAD_DIM, 1)
    e0 = jnp.where(low, x, 0.0).astype(BF16)
    o0 = jnp.where(low, 0.0, xr).astype(BF16)
    e1 = jnp.where(low, xr, 0.0).astype(BF16)
    o1 = jnp.where(low, 0.0, x).astype(BF16)
    return ((e0, o0), (e1, o1))


def _swa_prompt_kernel(sink_ref, q_ref, kv_ref, kvp_ref, o_ref):
    i = pl.program_id(0)

    @pl.when(i >= PROMPT_TILES)
    def _():
        o_ref[...] = jnp.zeros_like(o_ref)

    @pl.when(i < PROMPT_TILES)
    def _():
        first_tile = (i % TILES_PER_SEQ) == 0
        kv_all = jnp.concatenate([kvp_ref[...], kv_ref[...]], axis=0)
        t = lax.broadcasted_iota(jnp.int32, (BLOCK, 2 * BLOCK), 0)
        s = lax.broadcasted_iota(jnp.int32, (BLOCK, 2 * BLOCK), 1)
        dist_i = BLOCK + t - s
        in_window = (dist_i >= 0) & (dist_i <= WINDOW)
        dist = dist_i.astype(F32)
        for c in range(CHUNKS_PER_TILE):
            rows = slice(c * BLOCK, (c + 1) * BLOCK)
            keys = kv_all[c * BLOCK:(c + 2) * BLOCK]
            k_sel = _half_masks(keys[:, :KV_DIM])
            v_sel = _half_masks(keys[:, KV_DIM:])
            valid = in_window
            if c == 0:
                valid = valid & (s >= jnp.where(first_tile, BLOCK, 0))
            for p in range(N_PAIRS):
                kvh = (2 * p) // GQA_GROUP
                qp = q_ref[rows, p * LANES:(p + 1) * LANES]
                acc = None
                for par in range(2):
                    hd = 2 * p + par
                    logits = _dot_t(qp, k_sel[kvh][par]) - _SLOPES[hd] * dist
                    logits = jnp.where(valid, logits, -jnp.inf)
                    sink = sink_ref[hd]
                    m = jnp.maximum(jnp.max(logits, axis=-1, keepdims=True), sink)
                    e = jnp.exp(logits - m)
                    denom = jnp.sum(e, axis=-1, keepdims=True) + jnp.exp(sink - m)
                    probs = (e * (1.0 / denom)).astype(BF16)
                    part = _dot(probs, v_sel[kvh][par])
                    acc = part if acc is None else acc + part
                o_ref[rows, p * LANES:(p + 1) * LANES] = acc.astype(BF16)


def _swa_prompt(q, kv, sinks):
    last = PROMPT_TILES - 1

    def cur(i):
        return (jnp.minimum(i, last), 0)

    def prev(i):
        ii = jnp.minimum(i, last)
        return (jnp.maximum(ii * CHUNKS_PER_TILE - 1, 0), 0)

    return pl.pallas_call(
        _swa_prompt_kernel,
        grid=(N_TILES,),
        in_specs=[pl.BlockSpec(memory_space=pltpu.SMEM),
                  pl.BlockSpec((TM, Q_DIM), cur),
                  pl.BlockSpec((TM, 2 * KV_DIM), cur),
                  pl.BlockSpec((BLOCK, 2 * KV_DIM), prev)],
        out_specs=pl.BlockSpec((TM, Q_DIM), lambda i: (i, 0)),
        out_shape=jax.ShapeDtypeStruct((T_PAD, Q_DIM), BF16),
        compiler_params=_params(),
        name="swa_prompt",
    )(sinks, q, kv, kv)


SAMPLE_TILE = 32


def _swa_sample_kernel(q_ref, kvn_ref, ck_ref, cv_ref, slope_ref, sink_ref, o_ref):
    shape = (SAMPLE_TILE, N_PAIRS, LANES)
    lane = lax.broadcasted_iota(jnp.int32, shape, 2)
    pair = lax.broadcasted_iota(jnp.int32, shape, 1)
    low = lane < HEAD_DIM
    kv0 = pair < (N_PAIRS // 2)

    def swap(x):
        return pltpu.roll(x, HEAD_DIM, 2)

    q = q_ref[...].astype(F32)
    q_even = jnp.where(low, q, 0.0)
    q_odd = jnp.where(low, 0.0, q)
    q_al = (jnp.where(kv0, q_even, swap(q_even)), jnp.where(kv0, swap(q_odd), q_odd))

    ck = ck_ref[...].astype(BF16)
    cv = cv_ref[...].astype(BF16)
    kvn = kvn_ref[...]
    k_new = kvn[:, :, :KV_DIM]
    v_new = kvn[:, :, KV_DIM:]
    r = lax.broadcasted_iota(jnp.int32, (SAMPLE_TILE, N_PAIRS, WINDOW), 2)
    dist = (WINDOW - r).astype(F32)

    outs = []
    for par in range(2):
        qa = q_al[par]
        slope = slope_ref[par]
        sink = sink_ref[par]
        logits = lax.dot_general(qa.astype(BF16), ck, (((2,), (2,)), ((0,), (0,))),
                                 preferred_element_type=F32) - slope * dist
        l_self = jnp.sum(qa * k_new, axis=-1, keepdims=True)
        m = jnp.maximum(jnp.maximum(jnp.max(logits, axis=-1, keepdims=True), l_self), sink)
        e = jnp.exp(logits - m)
        e_self = jnp.exp(l_self - m)
        inv = 1.0 / (jnp.sum(e, axis=-1, keepdims=True) + e_self + jnp.exp(sink - m))
        o = lax.dot_general((e * inv).astype(BF16), cv, (((2,), (1,)), ((0,), (0,))),
                            preferred_element_type=F32) + (e_self * inv) * v_new
        outs.append(o)
    o_even = jnp.where(kv0, outs[0], swap(outs[0]))
    o_odd = jnp.where(kv0, swap(outs[1]), outs[1])
    o_ref[...] = jnp.where(low, o_even, o_odd).astype(BF16)


def _swa_sample(q_s, kv_s, cache_k, cache_v, sinks):
    rows = cache_k.shape[1]
    slopes = np.asarray(_SLOPES, np.float32).reshape(N_PAIRS, 2).T.reshape(2, N_PAIRS, 1)
    sink_arr = sinks.astype(F32).reshape(N_PAIRS, 2).T.reshape(2, N_PAIRS, 1)
    blk = lambda *shape: pl.BlockSpec((SAMPLE_TILE,) + shape, lambda i: (i,) + (0,) * len(shape))
    o3 = pl.pallas_call(
        _swa_sample_kernel,
        grid=(DEC_BATCH // SAMPLE_TILE,),
        in_specs=[blk(N_PAIRS, LANES), blk(1, 2 * KV_DIM), blk(rows, KV_DIM), blk(rows, KV_DIM),
                  _const_spec((2, N_PAIRS, 1)), _const_spec((2, N_PAIRS, 1))],
        out_specs=blk(N_PAIRS, LANES),
        out_shape=jax.ShapeDtypeStruct((DEC_BATCH, N_PAIRS, LANES), BF16),
        compiler_params=_params(),
        name="swa_sample",
    )(q_s.reshape(DEC_BATCH, N_PAIRS, LANES), kv_s.reshape(DEC_BATCH, 1, 2 * KV_DIM),
      cache_k.reshape(DEC_BATCH, rows, KV_DIM), cache_v.reshape(DEC_BATCH, rows, KV_DIM),
      jnp.asarray(slopes), sink_arr)
    return o3.reshape(DEC_BATCH, Q_DIM)


def _proj_route_kernel(x_ref, o_ref, wo_ref, g_ref, wr_ref, xo_ref, idx_ref, gate_ref, rank_ref,
                       cnt_ref, carry_ref):
    i = pl.program_id(0)

    @pl.when(i == 0)
    def _():
        carry_ref[...] = jnp.zeros_like(carry_ref)

    x = x_ref[...] + _dot(o_ref[...], wo_ref[...])
    xo_ref[...] = x
    h = _rms(x, g_ref[...])
    logits = jnp.dot(h, wr_ref[...], preferred_element_type=F32, precision=lax.Precision.HIGHEST)
    lane = lax.broadcasted_iota(jnp.int32, (TM, LANES), 1)
    logits = jnp.where(lane < N_EXPERTS, logits, -jnp.inf)
    m0 = jnp.max(logits, axis=-1, keepdims=True)
    i0 = jnp.min(jnp.where(logits == m0, lane, LANES), axis=-1, keepdims=True)
    rest = jnp.where(lane == i0, -jnp.inf, logits)
    m1 = jnp.max(rest, axis=-1, keepdims=True)
    i1 = jnp.min(jnp.where(rest == m1, lane, LANES), axis=-1, keepdims=True)
    e1 = jnp.exp(m1 - m0)
    g0 = 1.0 / (1.0 + e1)
    g1 = e1 * g0
    idx_ref[...] = jnp.concatenate([i0, i1], axis=1)
    gate_ref[...] = jnp.concatenate([g0, g1], axis=1)

    row_id = i * TM + lax.broadcasted_iota(jnp.int32, (TM, 1), 0)
    real = row_id < T_REAL
    onehot = jnp.where(((lane == i0) | (lane == i1)) & real, 1.0, 0.0)
    r = lax.broadcasted_iota(jnp.int32, (TM, TM), 0)
    c = lax.broadcasted_iota(jnp.int32, (TM, TM), 1)
    before = jnp.where(c < r, 1.0, 0.0).astype(BF16)
    ranks = _dot(before, onehot.astype(BF16)) + carry_ref[...]
    r0 = jnp.sum(jnp.where(lane == i0, ranks, 0.0), axis=-1, keepdims=True)
    r1 = jnp.sum(jnp.where(lane == i1, ranks, 0.0), axis=-1, keepdims=True)
    rank_ref[...] = jnp.concatenate([r0, r1], axis=1).astype(jnp.int32)
    carry_ref[...] = carry_ref[...] + jnp.sum(onehot, axis=0, keepdims=True)
    cnt_ref[...] = carry_ref[...].astype(jnp.int32)


def _proj_route(x, o, w_o, g, w_router):
    tile = pl.BlockSpec((TM, D_MODEL), lambda i: (i, 0))
    pair = pl.BlockSpec((TM, 2), lambda i: (i, 0))
    wr = jnp.zeros((D_MODEL, LANES), F32).at[:, :N_EXPERTS].set(w_router)
    return pl.pallas_call(
        _proj_route_kernel,
        grid=(N_TILES,),
        in_specs=[tile, tile, _const_spec((Q_DIM, D_MODEL)), _const_spec((1, D_MODEL)),
                  _const_spec((D_MODEL, LANES))],
        out_specs=[tile, pair, pair, pair, pl.BlockSpec((1, LANES), lambda i: (0, 0))],
        out_shape=[jax.ShapeDtypeStruct((T_PAD, D_MODEL), F32),
                   jax.ShapeDtypeStruct((T_PAD, 2), jnp.int32),
                   jax.ShapeDtypeStruct((T_PAD, 2), F32),
                   jax.ShapeDtypeStruct((T_PAD, 2), jnp.int32),
                   jax.ShapeDtypeStruct((1, LANES), jnp.int32)],
        scratch_shapes=[pltpu.VMEM((1, LANES), F32)],
        compiler_params=_params(),
        name="proj_route",
    )(x, o, w_o.astype(BF16), g.reshape(1, -1), wr)


def _row_copy(x_hbm, xbuf, sem, tok, r):
    return pltpu.make_async_copy(x_hbm.at[pl.ds(tok, 1), :], xbuf.at[pl.ds(r, 1), :], sem)


def _moe_kernel(te_ref, nu_ref, tok_ref, x_hbm, g_ref, wg_ref, wu_ref, wd_ref, o_ref,
                xbuf, hbuf, acc, sem):
    i = pl.program_id(0)
    f = pl.program_id(1)

    @pl.when((i >= nu_ref[0]) & (f == NF - 1))
    def _():
        o_ref[...] = jnp.zeros_like(o_ref)

    @pl.when(i < nu_ref[0])
    def _():
        @pl.when(f == 0)
        def _():
            base = i * TMM

            def start(r, carry):
                _row_copy(x_hbm, xbuf, sem, tok_ref[base + r], r).start()
                return carry

            def wait(r, carry):
                _row_copy(x_hbm, xbuf, sem, 0, r).wait()
                return carry

            lax.fori_loop(0, TMM, start, 0)
            lax.fori_loop(0, TMM, wait, 0)
            hbuf[...] = _rms(xbuf[...], g_ref[...]).astype(BF16)
            acc[...] = jnp.zeros_like(acc)

        h = hbuf[...]
        a = (_silu(_dot(h, wg_ref[...])) * _dot(h, wu_ref[...])).astype(BF16)
        acc[...] += _dot(a, wd_ref[...])

        @pl.when(f == NF - 1)
        def _():
            o_ref[...] = acc[...]


def _moe_layer(x, g, tile_expert, n_used, token_of, w_gate, w_up, w_down):
    def f_eff(i, f, nu):
        return jnp.where(i < nu[0], f, NF - 1)

    grid_spec = pltpu.PrefetchScalarGridSpec(
        num_scalar_prefetch=3,
        grid=(NT_MOE, NF),
        in_specs=[pl.BlockSpec(memory_space=pl.ANY),
                  pl.BlockSpec((1, D_MODEL), lambda i, f, te, nu, tok: (0, 0)),
                  pl.BlockSpec((None, D_MODEL, TF), lambda i, f, te, nu, tok: (te[i], 0, f_eff(i, f, nu))),
                  pl.BlockSpec((None, D_MODEL, TF), lambda i, f, te, nu, tok: (te[i], 0, f_eff(i, f, nu))),
                  pl.BlockSpec((None, TF, D_MODEL), lambda i, f, te, nu, tok: (te[i], f_eff(i, f, nu), 0))],
        out_specs=pl.BlockSpec((TMM, D_MODEL), lambda i, f, te, nu, tok: (i, 0)),
        scratch_shapes=[pltpu.VMEM((TMM, D_MODEL), F32),
                        pltpu.VMEM((TMM, D_MODEL), BF16),
                        pltpu.VMEM((TMM, D_MODEL), F32),
                        pltpu.SemaphoreType.DMA(())],
    )
    return pl.pallas_call(
        _moe_kernel,
        grid_spec=grid_spec,
        out_shape=jax.ShapeDtypeStruct((P_TOTAL, D_MODEL), F32),
        compiler_params=_params(2),
        name="moe_experts",
    )(tile_expert, n_used, token_of, x, g.reshape(1, -1),
      w_gate.astype(BF16), w_up.astype(BF16), w_down.astype(BF16))


def _combine_kernel(p0_ref, p1_ref, x_ref, gate_ref, ys_hbm, gf_ref, xo_ref, buf, sem, *, final_norm):
    i = pl.program_id(0)
    base = i * TM

    def start(r, carry):
        _row_copy(ys_hbm, buf.at[0], sem, p0_ref[base + r], r).start()
        _row_copy(ys_hbm, buf.at[1], sem, p1_ref[base + r], r).start()
        return carry

    def wait(r, carry):
        _row_copy(ys_hbm, buf.at[0], sem, 0, r).wait()
        _row_copy(ys_hbm, buf.at[1], sem, 0, r).wait()
        return carry

    lax.fori_loop(0, TM, start, 0)
    lax.fori_loop(0, TM, wait, 0)
    gate = gate_ref[...]
    y = x_ref[...] + (gate[:, 0:1] * buf[0] + gate[:, 1:2] * buf[1])
    if final_norm:
        y = _rms(y, gf_ref[...])
    xo_ref[...] = y


def _combine(x, gates, pos0, pos1, y_sorted, g_final, final_norm):
    tile = pl.BlockSpec((TM, D_MODEL), lambda i, p0, p1: (i, 0))
    grid_spec = pltpu.PrefetchScalarGridSpec(
        num_scalar_prefetch=2,
        grid=(N_TILES,),
        in_specs=[tile,
                  pl.BlockSpec((TM, 2), lambda i, p0, p1: (i, 0)),
                  pl.BlockSpec(memory_space=pl.ANY),
                  pl.BlockSpec((1, D_MODEL), lambda i, p0, p1: (0, 0))],
        out_specs=tile,
        scratch_shapes=[pltpu.VMEM((2, TM, D_MODEL), F32), pltpu.SemaphoreType.DMA(())],
    )
    return pl.pallas_call(
        functools.partial(_combine_kernel, final_norm=final_norm),
        grid_spec=grid_spec,
        out_shape=jax.ShapeDtypeStruct((T_PAD, D_MODEL), F32),
        compiler_params=_params(),
        name="moe_combine",
    )(pos0, pos1, x, gates, y_sorted, g_final.reshape(1, -1))


def _slot_plan(idx, rank, counts):
    cnt = counts[0, :N_EXPERTS]
    tiles = (cnt + TMM - 1) // TMM
    tile_end = jnp.cumsum(tiles)
    start = (tile_end - tiles) * TMM
    n_used = tile_end[-1:].astype(jnp.int32)
    real = (jnp.arange(T_PAD) < T_REAL)[:, None]
    pos = jnp.where(real, start[idx] + rank, 0).astype(jnp.int32)
    scatter_pos = jnp.where(real, pos, P_TOTAL)
    rows = jnp.broadcast_to(jnp.arange(T_PAD, dtype=jnp.int32)[:, None], (T_PAD, 2))
    token_of = jnp.zeros((P_TOTAL,), jnp.int32).at[scatter_pos.reshape(-1)].set(
        rows.reshape(-1), mode="drop")
    tile_ids = jnp.arange(NT_MOE, dtype=jnp.int32)
    tile_expert = jnp.minimum(jnp.searchsorted(tile_end, tile_ids, side="right"),
                              N_EXPERTS - 1).astype(jnp.int32)
    last_expert = tile_expert[jnp.maximum(n_used[0] - 1, 0)]
    tile_expert = jnp.where(tile_ids < n_used[0], tile_expert, last_expert)
    return pos[:, 0], pos[:, 1], token_of, tile_expert, n_used


def kernel(x_prompt, x_sample, cache_swa_k, cache_swa_v, norm_mix, norm_ffn, norm_final,
           sgu_w_in, sgu_b_in, sgu_ln_g, sgu_ln_b, sgu_w_s, sgu_b_s, sgu_w_out,
           attn_w_qkv, attn_sinks, attn_w_o,
           ffn_w_gate, ffn_w_up, ffn_w_down,
           moe_w_router, moe_w_gate, moe_w_up, moe_w_down):
    x = jnp.concatenate([x_prompt.reshape(N_PROMPT, D_MODEL),
                         x_sample.reshape(DEC_BATCH, D_MODEL),
                         jnp.zeros((T_PAD - T_REAL, D_MODEL), F32)], axis=0)
    sgu_v_p, sgu_v_s, k_p, v_p, k_s, v_s = [], [], [], [], [], []
    rows_p = min(WINDOW, SEQ)
    for i in range(DEPTH):
        j = i // 2
        if i % 2 == 0:
            x, vlast = _sgu_layer(x, norm_mix[i], sgu_w_in[j], sgu_b_in[j], sgu_ln_g[j],
                                  sgu_ln_b[j], sgu_w_s[j], sgu_b_s[j], sgu_w_out[j])
            sgu_v_p.append(vlast[:BATCH * CHUNK].reshape(BATCH, CHUNK, SGU_WIDTH))
            sgu_v_s.append(vlast[BATCH * CHUNK:].reshape(DEC_BATCH, 1, SGU_WIDTH))
            x = _ffn_layer(x, norm_ffn[i], ffn_w_gate[j], ffn_w_up[j], ffn_w_down[j])
        else:
            q, kv = _qkv_layer(x, norm_mix[i], attn_w_qkv[j])
            kv_p = kv[:N_PROMPT].reshape(BATCH, SEQ, 2 * KV_DIM)[:, SEQ - rows_p:]
            k_p.append(kv_p[..., :KV_DIM].reshape(BATCH, rows_p, N_KV_HEADS, HEAD_DIM))
            v_p.append(kv_p[..., KV_DIM:].reshape(BATCH, rows_p, N_KV_HEADS, HEAD_DIM))
            kv_s = kv[N_PROMPT:T_REAL]
            k_new = kv_s[:, None, :KV_DIM].reshape(DEC_BATCH, 1, N_KV_HEADS, HEAD_DIM)
            v_new = kv_s[:, None, KV_DIM:].reshape(DEC_BATCH, 1, N_KV_HEADS, HEAD_DIM)
            k_s.append(jnp.concatenate([cache_swa_k[j][:, 1:], k_new], axis=1))
            v_s.append(jnp.concatenate([cache_swa_v[j][:, 1:], v_new], axis=1))

            o = _swa_prompt(q, kv, attn_sinks[j].astype(F32))
            o_s = _swa_sample(q[N_PROMPT:T_REAL], kv_s, cache_swa_k[j], cache_swa_v[j], attn_sinks[j])
            o = lax.dynamic_update_slice(o, o_s, (N_PROMPT, 0))

            x, idx, gates, rank, counts = _proj_route(x, o, attn_w_o[j], norm_ffn[i], moe_w_router[j])
            pos0, pos1, token_of, tile_expert, n_used = _slot_plan(idx, rank, counts)
            y_sorted = _moe_layer(x, norm_ffn[i], tile_expert, n_used, token_of,
                                  moe_w_gate[j], moe_w_up[j], moe_w_down[j])
            x = _combine(x, gates, pos0, pos1, y_sorted, norm_final, final_norm=(i == DEPTH - 1))
    y_prompt = x[:N_PROMPT].reshape(BATCH, SEQ, D_MODEL)
    y_sample = x[N_PROMPT:T_REAL].reshape(DEC_BATCH, 1, D_MODEL)
    return (y_prompt, y_sample, jnp.stack(sgu_v_p), jnp.stack(sgu_v_s),
            jnp.stack(k_p), jnp.stack(v_p), jnp.stack(k_s), jnp.stack(v_s))
```

```python
import functools

import numpy as np
import jax
import jax.numpy as jnp
from jax import lax
from jax.experimental import pallas as pl
from jax.experimental.pallas import tpu as pltpu

D_MODEL = 1024
BATCH = 4
SEQ = 4096
DEPTH = 4
DEC_BATCH = 128
PAST_LEN = 8192
CHUNK = 128
SGU_WIDTH = 2 * D_MODEL
SGU_GROUPS = 8
SGU_GROUP_DIM = SGU_WIDTH // SGU_GROUPS
WINDOW = 128
BLOCK = 128
HEAD_DIM = 64
N_HEADS = D_MODEL // HEAD_DIM
N_KV_HEADS = 2
GQA_GROUP = N_HEADS // N_KV_HEADS
Q_DIM = N_HEADS * HEAD_DIM
KV_DIM = N_KV_HEADS * HEAD_DIM
D_FF = 2816
N_EXPERTS = 8
D_FF_EXPERT = 3584
EPS = 1e-6
LN_EPS = 1e-5

F32 = jnp.float32
BF16 = jnp.bfloat16

LANES = 128
TM = 512
N_PROMPT = BATCH * SEQ
T_REAL = N_PROMPT + DEC_BATCH
N_TILES = -(-T_REAL // TM)
T_PAD = N_TILES * TM
PROMPT_TILES = N_PROMPT // TM
TILES_PER_SEQ = SEQ // TM
CHUNKS_PER_TILE = TM // CHUNK
N_PAIRS = N_HEADS // 2

TMM = 512
TF = 896
NF = D_FF_EXPERT // TF
ROWS_PER_STEP = TMM // NF
N_SLOTS = 2 * T_PAD
NT_MOE = (N_SLOTS + N_EXPERTS * (TMM - 1)) // TMM + 1
P_TOTAL = NT_MOE * TMM
Y_ROWS = 2 * T_PAD + 2 * TMM
SLOT_SHIFT = 20
SLOT_MASK = (1 << SLOT_SHIFT) - 1

VMEM_LIMIT = 56 * 1024 * 1024

_SLOPES = [2.0 ** (-8.0 * (h + 1) / N_HEADS) for h in range(N_HEADS)]


def _rms(x, g):
    return x * lax.rsqrt(jnp.mean(x * x, axis=-1, keepdims=True) + EPS) * g


def _gelu(x):
    c = np.sqrt(2.0 / np.pi).astype(np.float32)
    return x * (0.5 * (1.0 + jnp.tanh(c * (x + 0.044715 * (x * x * x)))))


def _silu(x):
    return x * (1.0 / (1.0 + jnp.exp(-x)))


def _dot(a, b):
    return jnp.dot(a, b, preferred_element_type=F32)


def _dot_t(a, b):
    return lax.dot_general(a, b, (((1,), (1,)), ((), ())), preferred_element_type=F32)


def _const_spec(shape):
    nd = len(shape)
    return pl.BlockSpec(shape, lambda *_: (0,) * nd, pipeline_mode=pl.Buffered(1))


def _params(n_axes=1):
    return pltpu.CompilerParams(dimension_semantics=("arbitrary",) * n_axes,
                                vmem_limit_bytes=VMEM_LIMIT)


def _sgu_kernel(x_ref, g_ref, win_ref, bin_ref, lng_ref, lnb_ref, ws_ref, bs_ref, wout_ref,
                xo_ref, vlast_ref):
    i = pl.program_id(0)
    is_sample = i == N_TILES - 1
    x = x_ref[...]
    h = _rms(x, g_ref[...]).astype(BF16)

    v = _gelu(_dot(h, win_ref[:, SGU_WIDTH:]) + bin_ref[:, SGU_WIDTH:])
    mu = jnp.mean(v, axis=-1, keepdims=True)
    vc = v - mu
    var = jnp.mean(vc * vc, axis=-1, keepdims=True)
    vn = vc * lax.rsqrt(var + LN_EPS) * lng_ref[...] + lnb_ref[...]

    @pl.when(is_sample)
    def _():
        vlast_ref[...] = vn[:CHUNK]

    @pl.when(jnp.logical_not(is_sample))
    def _():
        vlast_ref[...] = vn[TM - CHUNK:]

    vb = vn.astype(BF16)
    row = lax.broadcasted_iota(jnp.int32, (CHUNK, CHUNK), 0)
    col = lax.broadcasted_iota(jnp.int32, (CHUNK, CHUNK), 1)
    sample_flag = jnp.where(is_sample, 1, 0)
    sample_mat = (jnp.zeros((CHUNK, CHUNK), jnp.int32) + sample_flag) == 1
    sample_col = (jnp.zeros((CHUNK, 1), jnp.int32) + sample_flag) == 1
    y = jnp.zeros((TM, D_MODEL), F32)
    for g in range(SGU_GROUPS):
        lo, hi = g * SGU_GROUP_DIM, (g + 1) * SGU_GROUP_DIM
        w = ws_ref[g]
        w_tril = jnp.where(row >= col, w, 0.0)
        w_diag = jnp.where(row == col, w[0:1, 0:1], 0.0)
        w_eff = jnp.where(sample_mat, w_diag, w_tril).astype(BF16)
        b = bs_ref[g]
        b_eff = jnp.where(sample_col, b[0:1, 0:1], b)
        u = _gelu(_dot(h, win_ref[:, lo:hi]) + bin_ref[:, lo:hi])
        gate = jnp.concatenate(
            [_dot(w_eff, vb[c * CHUNK:(c + 1) * CHUNK, lo:hi]) + b_eff
             for c in range(CHUNKS_PER_TILE)], axis=0)
        y = y + _dot((u * gate).astype(BF16), wout_ref[lo:hi, :])
    xo_ref[...] = x + y


def _sgu_layer(x, g, w_in, b_in, ln_g, ln_b, w_s, b_s, w_out):
    tile = pl.BlockSpec((TM, D_MODEL), lambda i: (i, 0))
    vlast_spec = pl.BlockSpec(
        (CHUNK, SGU_WIDTH),
        lambda i: (jnp.where(i == N_TILES - 1, BATCH, i // TILES_PER_SEQ), 0))
    return pl.pallas_call(
        _sgu_kernel,
        grid=(N_TILES,),
        in_specs=[tile,
                  _const_spec((1, D_MODEL)),
                  _const_spec((D_MODEL, 2 * SGU_WIDTH)),
                  _const_spec((1, 2 * SGU_WIDTH)),
                  _const_spec((1, SGU_WIDTH)),
                  _const_spec((1, SGU_WIDTH)),
                  _const_spec((SGU_GROUPS, CHUNK, CHUNK)),
                  _const_spec((SGU_GROUPS, CHUNK, 1)),
                  _const_spec((SGU_WIDTH, D_MODEL))],
        out_specs=[tile, vlast_spec],
        out_shape=[jax.ShapeDtypeStruct((T_PAD, D_MODEL), F32),
                   jax.ShapeDtypeStruct(((BATCH + 1) * CHUNK, SGU_WIDTH), F32)],
        compiler_params=_params(),
        name="sgu_mixer",
    )(x, g.reshape(1, -1), w_in.astype(BF16), b_in.reshape(1, -1), ln_g.reshape(1, -1),
      ln_b.reshape(1, -1), w_s, b_s.reshape(SGU_GROUPS, CHUNK, 1), w_out.astype(BF16))


def _ffn_kernel(x_ref, g_ref, wg_ref, wu_ref, wd_ref, xo_ref):
    x = x_ref[...]
    h = _rms(x, g_ref[...]).astype(BF16)
    a = (_silu(_dot(h, wg_ref[...])) * _dot(h, wu_ref[...])).astype(BF16)
    xo_ref[...] = x + _dot(a, wd_ref[...])


def _ffn_layer(x, g, w_gate, w_up, w_down):
    tile = pl.BlockSpec((TM, D_MODEL), lambda i: (i, 0))
    return pl.pallas_call(
        _ffn_kernel,
        grid=(N_TILES,),
        in_specs=[tile,
                  _const_spec((1, D_MODEL)),
                  _const_spec((D_MODEL, D_FF)),
                  _const_spec((D_MODEL, D_FF)),
                  _const_spec((D_FF, D_MODEL))],
        out_specs=tile,
        out_shape=jax.ShapeDtypeStruct((T_PAD, D_MODEL), F32),
        compiler_params=_params(),
        name="dense_swiglu",
    )(x, g.reshape(1, -1), w_gate.astype(BF16), w_up.astype(BF16), w_down.astype(BF16))


def _qkv_kernel(x_ref, g_ref, w_ref, q_ref, kv_ref):
    h = _rms(x_ref[...], g_ref[...]).astype(BF16)
    qkv = _dot(h, w_ref[...])
    q_ref[...] = (qkv[:, :Q_DIM] * (HEAD_DIM ** -0.5)).astype(BF16)
    kv_ref[...] = qkv[:, Q_DIM:]


def _qkv_layer(x, g, w_qkv):
    tile = pl.BlockSpec((TM, D_MODEL), lambda i: (i, 0))
    return pl.pallas_call(
        _qkv_kernel,
        grid=(N_TILES,),
        in_specs=[tile, _const_spec((1, D_MODEL)), _const_spec((D_MODEL, Q_DIM + 2 * KV_DIM))],
        out_specs=[tile, pl.BlockSpec((TM, 2 * KV_DIM), lambda i: (i, 0))],
        out_shape=[jax.ShapeDtypeStruct((T_PAD, Q_DIM), BF16),
                   jax.ShapeDtypeStruct((T_PAD, 2 * KV_DIM), F32)],
        compiler_params=_params(),
        name="swa_qkv",
    )(x, g.reshape(1, -1), w_qkv.astype(BF16))


def _half_masks(x):
    lane = lax.broadcasted_iota(jnp.int32, x.shape, 1)
    low = lane < HEAD_DIM
    xr = pltpu.roll(x, HEAD_DIM, 1)
    e0 = jnp.where(low, x, 0.0).astype(BF16)
    o0 = jnp.where(low, 0.0, xr).astype(BF16)
    e1 = jnp.where(low, xr, 0.0).astype(BF16)
    o1 = jnp.where(low, 0.0, x).astype(BF16)
    return ((e0, o0), (e1, o1))


def _swa_prompt_kernel(sink_ref, q_ref, kv_ref, kvp_ref, o_ref):
    i = pl.program_id(0)

    @pl.when(i >= PROMPT_TILES)
    def _():
        o_ref[...] = jnp.zeros_like(o_ref)

    @pl.when(i < PROMPT_TILES)
    def _():
        first_tile = (i % TILES_PER_SEQ) == 0
        kv_all = jnp.concatenate([kvp_ref[...], kv_ref[...]], axis=0)
        t = lax.broadcasted_iota(jnp.int32, (BLOCK, 2 * BLOCK), 0)
        s = lax.broadcasted_iota(jnp.int32, (BLOCK, 2 * BLOCK), 1)
        dist_i = BLOCK + t - s
        in_window = (dist_i >= 0) & (dist_i <= WINDOW)
        dist = dist_i.astype(F32)
        for c in range(CHUNKS_PER_TILE):
            rows = slice(c * BLOCK, (c + 1) * BLOCK)
            keys = kv_all[c * BLOCK:(c + 2) * BLOCK]
            k_sel = _half_masks(keys[:, :KV_DIM])
            v_sel = _half_masks(keys[:, KV_DIM:])
            valid = in_window
            if c == 0:
                valid = valid & (s >= jnp.where(first_tile, BLOCK, 0))
            for p in range(N_PAIRS):
                kvh = (2 * p) // GQA_GROUP
                qp = q_ref[rows, p * LANES:(p + 1) * LANES]
                acc = None
                for par in range(2):
                    hd = 2 * p + par
                    logits = _dot_t(qp, k_sel[kvh][par]) - _SLOPES[hd] * dist
                    logits = jnp.where(valid, logits, -jnp.inf)
                    sink = sink_ref[hd]
                    m = jnp.maximum(jnp.max(logits, axis=-1, keepdims=True), sink)
                    e = jnp.exp(logits - m)
                    denom = jnp.sum(e, axis=-1, keepdims=True) + jnp.exp(sink - m)
                    probs = (e * (1.0 / denom)).astype(BF16)
                    part = _dot(probs, v_sel[kvh][par])
                    acc = part if acc is None else acc + part
                o_ref[rows, p * LANES:(p + 1) * LANES] = acc.astype(BF16)


def _swa_prompt(q, kv, sinks):
    last = PROMPT_TILES - 1

    def cur(i):
        return (jnp.minimum(i, last), 0)

    def prev(i):
        ii = jnp.minimum(i, last)
        return (jnp.maximum(ii * CHUNKS_PER_TILE - 1, 0), 0)

    return pl.pallas_call(
        _swa_prompt_kernel,
        grid=(N_TILES,),
        in_specs=[pl.BlockSpec(memory_space=pltpu.SMEM),
                  pl.BlockSpec((TM, Q_DIM), cur),
                  pl.BlockSpec((TM, 2 * KV_DIM), cur),
                  pl.BlockSpec((BLOCK, 2 * KV_DIM), prev)],
        out_specs=pl.BlockSpec((TM, Q_DIM), lambda i: (i, 0)),
        out_shape=jax.ShapeDtypeStruct((T_PAD, Q_DIM), BF16),
        compiler_params=_params(),
        name="swa_prompt",
    )(sinks, q, kv, kv)


SAMPLE_TILE = 32


def _swa_sample_kernel(q_ref, kvn_ref, ck_ref, cv_ref, slope_ref, sink_ref, o_ref):
    shape = (SAMPLE_TILE, N_PAIRS, LANES)
    lane = lax.broadcasted_iota(jnp.int32, shape, 2)
    pair = lax.broadcasted_iota(jnp.int32, shape, 1)
    low = lane < HEAD_DIM
    kv0 = pair < (N_PAIRS // 2)

    def swap(x):
        return pltpu.roll(x, HEAD_DIM, 2)

    q = q_ref[...].astype(F32)
    q_even = jnp.where(low, q, 0.0)
    q_odd = jnp.where(low, 0.0, q)
    q_al = (jnp.where(kv0, q_even, swap(q_even)), jnp.where(kv0, swap(q_odd), q_odd))

    ck = ck_ref[...].astype(BF16)
    cv = cv_ref[...].astype(BF16)
    kvn = kvn_ref[...]
    k_new = kvn[:, :, :KV_DIM]
    v_new = kvn[:, :, KV_DIM:]
    r = lax.broadcasted_iota(jnp.int32, (SAMPLE_TILE, N_PAIRS, WINDOW), 2)
    dist = (WINDOW - r).astype(F32)

    outs = []
    for par in range(2):
        qa = q_al[par]
        slope = slope_ref[par]
        sink = sink_ref[par]
        logits = lax.dot_general(qa.astype(BF16), ck, (((2,), (2,)), ((0,), (0,))),
                                 preferred_element_type=F32) - slope * dist
        l_self = jnp.sum(qa * k_new, axis=-1, keepdims=True)
        m = jnp.maximum(jnp.maximum(jnp.max(logits, axis=-1, keepdims=True), l_self), sink)
        e = jnp.exp(logits - m)
        e_self = jnp.exp(l_self - m)
        inv = 1.0 / (jnp.sum(e, axis=-1, keepdims=True) + e_self + jnp.exp(sink - m))
        o = lax.dot_general((e * inv).astype(BF16), cv, (((2,), (1,)), ((0,), (0,))),
                            preferred_element_type=F32) + (e_self * inv) * v_new
        outs.append(o)
    o_even = jnp.where(kv0, outs[0], swap(outs[0]))
    o_odd = jnp.where(kv0, swap(outs[1]), outs[1])
    o_ref[...] = jnp.where(low, o_even, o_odd).astype(BF16)


def _swa_sample(q_s, kv_s, cache_k, cache_v, sinks):
    rows = cache_k.shape[1]
    slopes = np.asarray(_SLOPES, np.float32).reshape(N_PAIRS, 2).T.reshape(2, N_PAIRS, 1)
    sink_arr = sinks.astype(F32).reshape(N_PAIRS, 2).T.reshape(2, N_PAIRS, 1)
    blk = lambda *shape: pl.BlockSpec((SAMPLE_TILE,) + shape, lambda i: (i,) + (0,) * len(shape))
    o3 = pl.pallas_call(
        _swa_sample_kernel,
        grid=(DEC_BATCH // SAMPLE_TILE,),
        in_specs=[blk(N_PAIRS, LANES), blk(1, 2 * KV_DIM), blk(rows, KV_DIM), blk(rows, KV_DIM),
                  _const_spec((2, N_PAIRS, 1)), _const_spec((2, N_PAIRS, 1))],
        out_specs=blk(N_PAIRS, LANES),
        out_shape=jax.ShapeDtypeStruct((DEC_BATCH, N_PAIRS, LANES), BF16),
        compiler_params=_params(),
        name="swa_sample",
    )(q_s.reshape(DEC_BATCH, N_PAIRS, LANES), kv_s.reshape(DEC_BATCH, 1, 2 * KV_DIM),
      cache_k.reshape(DEC_BATCH, rows, KV_DIM), cache_v.reshape(DEC_BATCH, rows, KV_DIM),
      jnp.asarray(slopes), sink_arr)
    return o3.reshape(DEC_BATCH, Q_DIM)


def _proj_route_kernel(x_ref, o_ref, wo_ref, g_ref, wr_ref, xo_ref, idx_ref, gate_ref, rank_ref,
                       cnt_ref, carry_ref):
    i = pl.program_id(0)

    @pl.when(i == 0)
    def _():
        carry_ref[...] = jnp.zeros_like(carry_ref)

    x = x_ref[...] + _dot(o_ref[...], wo_ref[...])
    xo_ref[...] = x
    h = _rms(x, g_ref[...])
    logits = jnp.dot(h, wr_ref[...], preferred_element_type=F32, precision=lax.Precision.HIGHEST)
    lane = lax.broadcasted_iota(jnp.int32, (TM, LANES), 1)
    logits = jnp.where(lane < N_EXPERTS, logits, -jnp.inf)
    m0 = jnp.max(logits, axis=-1, keepdims=True)
    i0 = jnp.min(jnp.where(logits == m0, lane, LANES), axis=-1, keepdims=True)
    rest = jnp.where(lane == i0, -jnp.inf, logits)
    m1 = jnp.max(rest, axis=-1, keepdims=True)
    i1 = jnp.min(jnp.where(rest == m1, lane, LANES), axis=-1, keepdims=True)
    e1 = jnp.exp(m1 - m0)
    g0 = 1.0 / (1.0 + e1)
    g1 = e1 * g0
    idx_ref[...] = jnp.concatenate([i0, i1], axis=1)
    gate_ref[...] = jnp.concatenate([g0, g1], axis=1)

    onehot = jnp.where((lane == i0) | (lane == i1), 1.0, 0.0)
    r = lax.broadcasted_iota(jnp.int32, (TM, TM), 0)
    c = lax.broadcasted_iota(jnp.int32, (TM, TM), 1)
    before = jnp.where(c < r, 1.0, 0.0).astype(BF16)
    ranks = _dot(before, onehot.astype(BF16)) + carry_ref[...]
    r0 = jnp.sum(jnp.where(lane == i0, ranks, 0.0), axis=-1, keepdims=True)
    r1 = jnp.sum(jnp.where(lane == i1, ranks, 0.0), axis=-1, keepdims=True)
    rank_ref[...] = jnp.concatenate([r0, r1], axis=1).astype(jnp.int32)
    carry_ref[...] = carry_ref[...] + jnp.sum(onehot, axis=0, keepdims=True)
    cnt_ref[...] = carry_ref[...].astype(jnp.int32)


def _proj_route(x, o, w_o, g, w_router):
    tile = pl.BlockSpec((TM, D_MODEL), lambda i: (i, 0))
    pair = pl.BlockSpec((TM, 2), lambda i: (i, 0))
    wr = jnp.zeros((D_MODEL, LANES), F32).at[:, :N_EXPERTS].set(w_router)
    return pl.pallas_call(
        _proj_route_kernel,
        grid=(N_TILES,),
        in_specs=[tile, tile, _const_spec((Q_DIM, D_MODEL)), _const_spec((1, D_MODEL)),
                  _const_spec((D_MODEL, LANES))],
        out_specs=[tile, pair, pair, pair, pl.BlockSpec((1, LANES), lambda i: (0, 0))],
        out_shape=[jax.ShapeDtypeStruct((T_PAD, D_MODEL), F32),
                   jax.ShapeDtypeStruct((T_PAD, 2), jnp.int32),
                   jax.ShapeDtypeStruct((T_PAD, 2), F32),
                   jax.ShapeDtypeStruct((T_PAD, 2), jnp.int32),
                   jax.ShapeDtypeStruct((1, LANES), jnp.int32)],
        scratch_shapes=[pltpu.VMEM((1, LANES), F32)],
        compiler_params=_params(),
        name="proj_route",
    )(x, o, w_o.astype(BF16), g.reshape(1, -1), wr)


def _row_copy(x_hbm, xbuf, sem, tok, r):
    return pltpu.make_async_copy(x_hbm.at[pl.ds(tok, 1), :], xbuf.at[pl.ds(r, 1), :], sem)


def _gather_wait(x_hbm, xbuf, sem, slot):
    pltpu.make_async_copy(x_hbm.at[pl.ds(0, TMM), :], xbuf.at[slot], sem.at[slot]).wait()


def _scatter_wait(obuf, y_hbm, sem, slot):
    pltpu.make_async_copy(obuf.at[slot], y_hbm.at[pl.ds(0, TMM), :], sem.at[slot]).wait()


def _slot_token(entry):
    return entry & SLOT_MASK


def _slot_dest(entry):
    return (entry >> SLOT_SHIFT) * T_PAD + (entry & SLOT_MASK)


def _moe_kernel(te_ref, nu_ref, slot_ref, x_hbm, g_ref, wg_ref, wu_ref, wd_ref, y_hbm,
                xbuf, hbuf, acc, obuf, gsem, ssem):
    i = pl.program_id(0)
    f = pl.program_id(1)
    n_used = nu_ref[0]
    slot = i % 2
    other = 1 - slot

    @pl.when(i < n_used)
    def _():
        @pl.when((i == 0) & (f == 0))
        def _():
            def start(r, carry):
                _row_copy(x_hbm, xbuf.at[0], gsem.at[0], _slot_token(slot_ref[TMM + r]), r).start()
                return carry
            lax.fori_loop(0, TMM, start, 0)
            obuf[...] = jnp.zeros_like(obuf)
            for s in range(2):
                spare = pltpu.make_async_copy(
                    obuf.at[s], y_hbm.at[pl.ds(2 * T_PAD + s * TMM, TMM), :], ssem.at[s])
                spare.start()
                spare.wait()

        @pl.when(f == 0)
        def _():
            _gather_wait(x_hbm, xbuf, gsem, slot)
            hbuf[...] = _rms(xbuf[slot], g_ref[...]).astype(BF16)
            acc[...] = jnp.zeros_like(acc)

        row0 = f * ROWS_PER_STEP
        nxt = (i + 2) * TMM + row0
        prv = i * TMM + row0
        for r in range(ROWS_PER_STEP):
            _row_copy(x_hbm, xbuf.at[other], gsem.at[other],
                      _slot_token(slot_ref[nxt + r]), row0 + r).start()
            pltpu.make_async_copy(obuf.at[other, pl.ds(row0 + r, 1), :],
                                  y_hbm.at[pl.ds(_slot_dest(slot_ref[prv + r]), 1), :],
                                  ssem.at[other]).start()

        h = hbuf[...]
        a = (_silu(_dot(h, wg_ref[...])) * _dot(h, wu_ref[...])).astype(BF16)
        acc[...] += _dot(a, wd_ref[...])

        @pl.when(f == NF - 1)
        def _():
            @pl.when(i >= 1)
            def _():
                _scatter_wait(obuf, y_hbm, ssem, slot)
            obuf[slot] = acc[...]

        @pl.when((f == NF - 1) & (i == n_used - 1))
        def _():
            _gather_wait(x_hbm, xbuf, gsem, other)
            _scatter_wait(obuf, y_hbm, ssem, other)
            base = (i + 1) * TMM

            def start(r, carry):
                pltpu.make_async_copy(obuf.at[slot, pl.ds(r, 1), :],
                                      y_hbm.at[pl.ds(_slot_dest(slot_ref[base + r]), 1), :],
                                      ssem.at[slot]).start()
                return carry
            lax.fori_loop(0, TMM, start, 0)
            _scatter_wait(obuf, y_hbm, ssem, slot)


def _moe_layer(x, g, tile_expert, n_used, slots, layer, w_gate, w_up, w_down):
    def f_eff(i, f, nu):
        return jnp.where(i < nu[0], f, NF - 1)

    grid_spec = pltpu.PrefetchScalarGridSpec(
        num_scalar_prefetch=3,
        grid=(NT_MOE, NF),
        in_specs=[pl.BlockSpec(memory_space=pl.ANY),
                  pl.BlockSpec((1, D_MODEL), lambda i, f, te, nu, tok: (0, 0)),
                  pl.BlockSpec((None, None, D_MODEL, TF),
                               lambda i, f, te, nu, tok: (layer, te[i], 0, f_eff(i, f, nu))),
                  pl.BlockSpec((None, None, D_MODEL, TF),
                               lambda i, f, te, nu, tok: (layer, te[i], 0, f_eff(i, f, nu))),
                  pl.BlockSpec((None, None, TF, D_MODEL),
                               lambda i, f, te, nu, tok: (layer, te[i], f_eff(i, f, nu), 0))],
        out_specs=pl.BlockSpec(memory_space=pl.ANY),
        scratch_shapes=[pltpu.VMEM((2, TMM, D_MODEL), F32),
                        pltpu.VMEM((TMM, D_MODEL), BF16),
                        pltpu.VMEM((TMM, D_MODEL), F32),
                        pltpu.VMEM((2, TMM, D_MODEL), F32),
                        pltpu.SemaphoreType.DMA((2,)),
                        pltpu.SemaphoreType.DMA((2,))],
    )
    return pl.pallas_call(
        _moe_kernel,
        grid_spec=grid_spec,
        out_shape=jax.ShapeDtypeStruct((Y_ROWS, D_MODEL), F32),
        compiler_params=_params(2),
        name="moe_experts",
    )(tile_expert, n_used, slots, x, g.reshape(1, -1), w_gate, w_up, w_down)


def _combine_kernel(x_ref, gate_ref, y0_ref, y1_ref, gf_ref, xo_ref, *, final_norm):
    gate = gate_ref[...]
    y = x_ref[...] + (gate[:, 0:1] * y0_ref[...] + gate[:, 1:2] * y1_ref[...])
    if final_norm:
        y = _rms(y, gf_ref[...])
    xo_ref[...] = y


def _combine(x, gates, y_slots, g_final, final_norm):
    tile = pl.BlockSpec((TM, D_MODEL), lambda i: (i, 0))
    return pl.pallas_call(
        functools.partial(_combine_kernel, final_norm=final_norm),
        grid=(N_TILES,),
        in_specs=[tile,
                  pl.BlockSpec((TM, 2), lambda i: (i, 0)),
                  tile,
                  pl.BlockSpec((TM, D_MODEL), lambda i: (N_TILES + i, 0)),
                  _const_spec((1, D_MODEL))],
        out_specs=tile,
        out_shape=jax.ShapeDtypeStruct((T_PAD, D_MODEL), F32),
        compiler_params=_params(),
        name="moe_combine",
    )(x, gates, y_slots, y_slots, g_final.reshape(1, -1))


def _padding_slot_entries():
    p = np.arange(-TMM, P_TOTAL + TMM)
    spare_row = ((p // TMM) % 2) * TMM + p % TMM
    return ((2 << SLOT_SHIFT) | spare_row).astype(np.int32)


def _slot_plan(idx, rank, counts):
    cnt = counts[0, :N_EXPERTS]
    tiles = (cnt + TMM - 1) // TMM
    tile_end = jnp.cumsum(tiles)
    start = (tile_end - tiles) * TMM
    n_used = tile_end[-1:].astype(jnp.int32)
    pos = (start[idx] + rank).astype(jnp.int32)
    rows = jnp.arange(T_PAD, dtype=jnp.int32)[:, None]
    entry = rows | (jnp.arange(2, dtype=jnp.int32)[None, :] << SLOT_SHIFT)
    slots = jnp.asarray(_padding_slot_entries()).at[(TMM + pos).reshape(-1)].set(entry.reshape(-1))
    tile_ids = jnp.arange(NT_MOE, dtype=jnp.int32)
    tile_expert = jnp.minimum(jnp.sum(tile_end[None, :] <= tile_ids[:, None], axis=1),
                              N_EXPERTS - 1).astype(jnp.int32)
    last_expert = tile_expert[jnp.maximum(n_used[0] - 1, 0)]
    tile_expert = jnp.where(tile_ids < n_used[0], tile_expert, last_expert)
    return slots, tile_expert, n_used


def kernel(x_prompt, x_sample, cache_swa_k, cache_swa_v, norm_mix, norm_ffn, norm_final,
           sgu_w_in, sgu_b_in, sgu_ln_g, sgu_ln_b, sgu_w_s, sgu_b_s, sgu_w_out,
           attn_w_qkv, attn_sinks, attn_w_o,
           ffn_w_gate, ffn_w_up, ffn_w_down,
           moe_w_router, moe_w_gate, moe_w_up, moe_w_down):
    x = jnp.concatenate([x_prompt.reshape(N_PROMPT, D_MODEL),
                         x_sample.reshape(DEC_BATCH, D_MODEL),
                         jnp.zeros((T_PAD - T_REAL, D_MODEL), F32)], axis=0)
    moe_wg, moe_wu, moe_wd = (w.astype(BF16) for w in (moe_w_gate, moe_w_up, moe_w_down))
    sgu_v_p, sgu_v_s, k_p, v_p, k_s, v_s = [], [], [], [], [], []
    rows_p = min(WINDOW, SEQ)
    for i in range(DEPTH):
        j = i // 2
        if i % 2 == 0:
            x, vlast = _sgu_layer(x, norm_mix[i], sgu_w_in[j], sgu_b_in[j], sgu_ln_g[j],
                                  sgu_ln_b[j], sgu_w_s[j], sgu_b_s[j], sgu_w_out[j])
            sgu_v_p.append(vlast[:BATCH * CHUNK].reshape(BATCH, CHUNK, SGU_WIDTH))
            sgu_v_s.append(vlast[BATCH * CHUNK:].reshape(DEC_BATCH, 1, SGU_WIDTH))
            x = _ffn_layer(x, norm_ffn[i], ffn_w_gate[j], ffn_w_up[j], ffn_w_down[j])
        else:
            q, kv = _qkv_layer(x, norm_mix[i], attn_w_qkv[j])
            kv_p = kv[:N_PROMPT].reshape(BATCH, SEQ, 2 * KV_DIM)[:, SEQ - rows_p:]
            k_p.append(kv_p[..., :KV_DIM].reshape(BATCH, rows_p, N_KV_HEADS, HEAD_DIM))
            v_p.append(kv_p[..., KV_DIM:].reshape(BATCH, rows_p, N_KV_HEADS, HEAD_DIM))
            kv_s = kv[N_PROMPT:T_REAL]
            k_new = kv_s[:, None, :KV_DIM].reshape(DEC_BATCH, 1, N_KV_HEADS, HEAD_DIM)
            v_new = kv_s[:, None, KV_DIM:].reshape(DEC_BATCH, 1, N_KV_HEADS, HEAD_DIM)
            k_s.append(jnp.concatenate([cache_swa_k[j][:, 1:], k_new], axis=1))
            v_s.append(jnp.concatenate([cache_swa_v[j][:, 1:], v_new], axis=1))

            o = _swa_prompt(q, kv, attn_sinks[j].astype(F32))
            o_s = _swa_sample(q[N_PROMPT:T_REAL], kv_s, cache_swa_k[j], cache_swa_v[j], attn_sinks[j])
            o = lax.dynamic_update_slice(o, o_s, (N_PROMPT, 0))

            x, idx, gates, rank, counts = _proj_route(x, o, attn_w_o[j], norm_ffn[i], moe_w_router[j])
            slots, tile_expert, n_used = _slot_plan(idx, rank, counts)
            y_slots = _moe_layer(x, norm_ffn[i], tile_expert, n_used, slots, j,
                                 moe_wg, moe_wu, moe_wd)
            x = _combine(x, gates, y_slots, norm_final, final_norm=(i == DEPTH - 1))
    y_prompt = x[:N_PROMPT].reshape(BATCH, SEQ, D_MODEL)
    y_sample = x[N_PROMPT:T_REAL].reshape(DEC_BATCH, 1, D_MODEL)
    return (y_prompt, y_sample, jnp.stack(sgu_v_p), jnp.stack(sgu_v_s),
            jnp.stack(k_p), jnp.stack(v_p), jnp.stack(k_s), jnp.stack(v_s))
```

```python
import functools

import numpy as np
import jax
import jax.numpy as jnp
from jax import lax
from jax.experimental import pallas as pl
from jax.experimental.pallas import tpu as pltpu

D_MODEL = 1024
BATCH = 4
SEQ = 4096
DEPTH = 4
DEC_BATCH = 128
PAST_LEN = 8192
CHUNK = 128
SGU_WIDTH = 2 * D_MODEL
SGU_GROUPS = 8
SGU_GROUP_DIM = SGU_WIDTH // SGU_GROUPS
WINDOW = 128
BLOCK = 128
HEAD_DIM = 64
N_HEADS = D_MODEL // HEAD_DIM
N_KV_HEADS = 2
GQA_GROUP = N_HEADS // N_KV_HEADS
Q_DIM = N_HEADS * HEAD_DIM
KV_DIM = N_KV_HEADS * HEAD_DIM
D_FF = 2816
N_EXPERTS = 8
D_FF_EXPERT = 3584
EPS = 1e-6
LN_EPS = 1e-5

F32 = jnp.float32
BF16 = jnp.bfloat16

LANES = 128
TM = 512
N_PROMPT = BATCH * SEQ
T_REAL = N_PROMPT + DEC_BATCH
N_TILES = -(-T_REAL // TM)
T_PAD = N_TILES * TM
PROMPT_TILES = N_PROMPT // TM
TILES_PER_SEQ = SEQ // TM
CHUNKS_PER_TILE = TM // CHUNK
N_PAIRS = N_HEADS // 2

TMM = 512
TF = 896
NF = D_FF_EXPERT // TF
ROWS_PER_STEP = TMM // NF
N_SLOTS = 2 * T_PAD
NT_MOE = (N_SLOTS + N_EXPERTS * (TMM - 1)) // TMM + 1
P_TOTAL = NT_MOE * TMM
Y_ROWS = 2 * T_PAD + 2 * TMM
SLOT_SHIFT = 20
SLOT_MASK = (1 << SLOT_SHIFT) - 1

VMEM_LIMIT = 56 * 1024 * 1024

_SLOPES = [2.0 ** (-8.0 * (h + 1) / N_HEADS) for h in range(N_HEADS)]


def _rms(x, g):
    return x * lax.rsqrt(jnp.mean(x * x, axis=-1, keepdims=True) + EPS) * g


def _gelu(x):
    c = np.sqrt(2.0 / np.pi).astype(np.float32)
    return x * (0.5 * (1.0 + jnp.tanh(c * (x + 0.044715 * (x * x * x)))))


def _silu(x):
    return x * (1.0 / (1.0 + jnp.exp(-x)))


def _dot(a, b):
    return jnp.dot(a, b, preferred_element_type=F32)


def _dot_t(a, b):
    return lax.dot_general(a, b, (((1,), (1,)), ((), ())), preferred_element_type=F32)


SUBLANES = 8
ROW_CHUNKS = D_MODEL // LANES


def _chunk(s, rows):
    return pl.ds(s, rows, stride=SUBLANES)


def _load_row_tiles(ref, rows):
    return jnp.concatenate([ref[_chunk(s, rows), :] for s in range(ROW_CHUNKS)], axis=1)


def _store_row_tiles(ref, x):
    rows = x.shape[0]
    for s in range(ROW_CHUNKS):
        ref[_chunk(s, rows), :] = x[:, s * LANES:(s + 1) * LANES]


def _const_spec(shape):
    nd = len(shape)
    return pl.BlockSpec(shape, lambda *_: (0,) * nd, pipeline_mode=pl.Buffered(1))


def _params(n_axes=1):
    return pltpu.CompilerParams(dimension_semantics=("arbitrary",) * n_axes,
                                vmem_limit_bytes=VMEM_LIMIT)


def _combined_rows(x_ref, gate_ref, y0_ref, y1_ref):
    gate = gate_ref[...]
    g0, g1 = gate[:, 0:1], gate[:, 1:2]
    return jnp.concatenate(
        [x_ref[_chunk(s, TM), :] + (g0 * y0_ref[_chunk(s, TM), :] + g1 * y1_ref[_chunk(s, TM), :])
         for s in range(ROW_CHUNKS)], axis=1)


def _sgu_kernel(*refs, source):
    i = pl.program_id(0)
    is_sample = i == N_TILES - 1
    if source == "inputs":
        xp_ref, xs_ref = refs[:2]
        refs = refs[2:]
        xs = jnp.concatenate([xs_ref[...], jnp.zeros((TM - DEC_BATCH, D_MODEL), F32)], axis=0)
        from_sample = (jnp.zeros((TM, 1), jnp.int32) + jnp.where(is_sample, 1, 0)) == 1
        x = jnp.where(from_sample, xs, xp_ref[...])
    else:
        x = _combined_rows(*refs[:4])
        refs = refs[4:]
    g_ref, win_ref, bin_ref, lng_ref, lnb_ref, ws_ref, bs_ref, wout_ref, xo_ref, vlast_ref = refs
    h = _rms(x, g_ref[...]).astype(BF16)

    v = _gelu(_dot(h, win_ref[:, SGU_WIDTH:]) + bin_ref[:, SGU_WIDTH:])
    mu = jnp.mean(v, axis=-1, keepdims=True)
    vc = v - mu
    var = jnp.mean(vc * vc, axis=-1, keepdims=True)
    vn = vc * lax.rsqrt(var + LN_EPS) * lng_ref[...] + lnb_ref[...]

    @pl.when(is_sample)
    def _():
        vlast_ref[...] = vn[:CHUNK]

    @pl.when(jnp.logical_not(is_sample))
    def _():
        vlast_ref[...] = vn[TM - CHUNK:]

    vb = vn.astype(BF16)
    row = lax.broadcasted_iota(jnp.int32, (CHUNK, CHUNK), 0)
    col = lax.broadcasted_iota(jnp.int32, (CHUNK, CHUNK), 1)
    sample_flag = jnp.where(is_sample, 1, 0)
    sample_mat = (jnp.zeros((CHUNK, CHUNK), jnp.int32) + sample_flag) == 1
    sample_col = (jnp.zeros((CHUNK, 1), jnp.int32) + sample_flag) == 1
    y = jnp.zeros((TM, D_MODEL), F32)
    for g in range(SGU_GROUPS):
        lo, hi = g * SGU_GROUP_DIM, (g + 1) * SGU_GROUP_DIM
        w = ws_ref[g]
        w_tril = jnp.where(row >= col, w, 0.0)
        w_diag = jnp.where(row == col, w[0:1, 0:1], 0.0)
        w_eff = jnp.where(sample_mat, w_diag, w_tril).astype(BF16)
        b = bs_ref[g]
        b_eff = jnp.where(sample_col, b[0:1, 0:1], b)
        u = _gelu(_dot(h, win_ref[:, lo:hi]) + bin_ref[:, lo:hi])
        gate = jnp.concatenate(
            [_dot(w_eff, vb[c * CHUNK:(c + 1) * CHUNK, lo:hi]) + b_eff
             for c in range(CHUNKS_PER_TILE)], axis=0)
        y = y + _dot((u * gate).astype(BF16), wout_ref[lo:hi, :])
    xo_ref[...] = x + y


def _row_tile_specs():
    return [pl.BlockSpec((TM * SUBLANES, LANES), lambda i: (i, 0)),
            pl.BlockSpec((TM, 2), lambda i: (i, 0)),
            pl.BlockSpec((TM * SUBLANES, LANES), lambda i: (i, 0)),
            pl.BlockSpec((TM * SUBLANES, LANES), lambda i: (N_TILES + i, 0))]


def _sgu_layer(rows, source, g, w_in, b_in, ln_g, ln_b, w_s, b_s, w_out):
    tile = pl.BlockSpec((TM, D_MODEL), lambda i: (i, 0))
    vlast_spec = pl.BlockSpec(
        (CHUNK, SGU_WIDTH),
        lambda i: (jnp.where(i == N_TILES - 1, BATCH, i // TILES_PER_SEQ), 0))
    if source == "inputs":
        row_specs = [pl.BlockSpec((TM, D_MODEL), lambda i: (jnp.minimum(i, PROMPT_TILES - 1), 0)),
                     pl.BlockSpec((DEC_BATCH, D_MODEL), lambda i: (0, 0))]
        row_args = rows
    else:
        row_specs = _row_tile_specs()
        x, gates, y_slots = rows
        row_args = (x, gates, y_slots, y_slots)
    return pl.pallas_call(
        functools.partial(_sgu_kernel, source=source),
        grid=(N_TILES,),
        in_specs=row_specs + [
                  _const_spec((1, D_MODEL)),
                  _const_spec((D_MODEL, 2 * SGU_WIDTH)),
                  _const_spec((1, 2 * SGU_WIDTH)),
                  _const_spec((1, SGU_WIDTH)),
                  _const_spec((1, SGU_WIDTH)),
                  _const_spec((SGU_GROUPS, CHUNK, CHUNK)),
                  _const_spec((SGU_GROUPS, CHUNK, 1)),
                  _const_spec((SGU_WIDTH, D_MODEL))],
        out_specs=[tile, vlast_spec],
        out_shape=[jax.ShapeDtypeStruct((T_PAD, D_MODEL), F32),
                   jax.ShapeDtypeStruct(((BATCH + 1) * CHUNK, SGU_WIDTH), F32)],
        compiler_params=_params(),
        name="sgu_mixer",
    )(*row_args, g.reshape(1, -1), w_in.astype(BF16), b_in.reshape(1, -1), ln_g.reshape(1, -1),
      ln_b.reshape(1, -1), w_s, b_s.reshape(SGU_GROUPS, CHUNK, 1), w_out.astype(BF16))


def _ffn_kernel(x_ref, g_ref, wg_ref, wu_ref, wd_ref, xo_ref):
    x = x_ref[...]
    h = _rms(x, g_ref[...]).astype(BF16)
    a = (_silu(_dot(h, wg_ref[...])) * _dot(h, wu_ref[...])).astype(BF16)
    xo_ref[...] = x + _dot(a, wd_ref[...])


def _ffn_layer(x, g, w_gate, w_up, w_down):
    tile = pl.BlockSpec((TM, D_MODEL), lambda i: (i, 0))
    return pl.pallas_call(
        _ffn_kernel,
        grid=(N_TILES,),
        in_specs=[tile,
                  _const_spec((1, D_MODEL)),
                  _const_spec((D_MODEL, D_FF)),
                  _const_spec((D_MODEL, D_FF)),
                  _const_spec((D_FF, D_MODEL))],
        out_specs=tile,
        out_shape=jax.ShapeDtypeStruct((T_PAD, D_MODEL), F32),
        compiler_params=_params(),
        name="dense_swiglu",
    )(x, g.reshape(1, -1), w_gate.astype(BF16), w_up.astype(BF16), w_down.astype(BF16))


def _qkv_kernel(x_ref, g_ref, w_ref, q_ref, kv_ref):
    h = _rms(x_ref[...], g_ref[...]).astype(BF16)
    qkv = _dot(h, w_ref[...])
    q_ref[...] = (qkv[:, :Q_DIM] * (HEAD_DIM ** -0.5)).astype(BF16)
    kv_ref[...] = qkv[:, Q_DIM:]


def _qkv_layer(x, g, w_qkv):
    tile = pl.BlockSpec((TM, D_MODEL), lambda i: (i, 0))
    return pl.pallas_call(
        _qkv_kernel,
        grid=(N_TILES,),
        in_specs=[tile, _const_spec((1, D_MODEL)), _const_spec((D_MODEL, Q_DIM + 2 * KV_DIM))],
        out_specs=[tile, pl.BlockSpec((TM, 2 * KV_DIM), lambda i: (i, 0))],
        out_shape=[jax.ShapeDtypeStruct((T_PAD, Q_DIM), BF16),
                   jax.ShapeDtypeStruct((T_PAD, 2 * KV_DIM), F32)],
        compiler_params=_params(),
        name="swa_qkv",
    )(x, g.reshape(1, -1), w_qkv.astype(BF16))


def _half_masks_f32(x):
    lane = lax.broadcasted_iota(jnp.int32, x.shape, 1)
    low = lane < HEAD_DIM
    xr = pltpu.roll(x, HEAD_DIM, 1)
    return ((jnp.where(low, x, 0.0), jnp.where(low, 0.0, xr)),
            (jnp.where(low, xr, 0.0), jnp.where(low, 0.0, x)))


def _half_masks(x):
    return tuple(tuple(m.astype(BF16) for m in pair) for pair in _half_masks_f32(x))


def _swa_prompt_kernel(sink_ref, q_ref, kv_ref, kvp_ref, o_ref, bias_ref):
    i = pl.program_id(0)
    t = lax.broadcasted_iota(jnp.int32, (BLOCK, BLOCK), 0)
    c = lax.broadcasted_iota(jnp.int32, (BLOCK, BLOCK), 1)
    own = c <= t
    diag = c == t

    @pl.when(i == 0)
    def _():
        dist = jnp.where(own, t - c, BLOCK + t - c).astype(F32)
        for hd in range(N_HEADS):
            bias_ref[hd] = _SLOPES[hd] * dist

    first_tile = (i % TILES_PER_SEQ) == 0
    kv_all = jnp.concatenate([kvp_ref[...], kv_ref[...]], axis=0)
    k_blk, v_blk, vf_blk = [], [], []
    for b in range(CHUNKS_PER_TILE + 1):
        blk = kv_all[b * BLOCK:(b + 1) * BLOCK]
        k_blk.append(_half_masks(blk[:, :KV_DIM]))
        v_blk.append(_half_masks(blk[:, KV_DIM:]))
        vf_blk.append(_half_masks_f32(blk[:, KV_DIM:]))
    has_prev = (jnp.zeros((BLOCK, BLOCK), jnp.int32) + jnp.where(first_tile, 0, 1)) == 1

    for b in range(CHUNKS_PER_TILE):
        rows = slice(b * BLOCK, (b + 1) * BLOCK)
        for p in range(N_PAIRS):
            kvh = (2 * p) // GQA_GROUP
            qp = q_ref[rows, p * LANES:(p + 1) * LANES]
            acc = None
            for par in range(2):
                hd = 2 * p + par
                sink = sink_ref[hd]
                keys = jnp.concatenate([k_blk[b][kvh][par], k_blk[b + 1][kvh][par]], axis=0)
                both = _dot_t(qp, keys)
                l_prev, l_own = both[:, :BLOCK], both[:, BLOCK:]
                if b == 0:
                    l_prev = jnp.where(has_prev, l_prev, -jnp.inf)
                logits = jnp.where(own, l_own, l_prev) - bias_ref[hd]
                extra = (jnp.sum(jnp.where(diag, l_prev, 0.0), axis=-1, keepdims=True)
                         - _SLOPES[hd] * BLOCK)
                m = jnp.maximum(jnp.maximum(jnp.max(logits, axis=-1, keepdims=True), extra), sink)
                e = jnp.exp(logits - m)
                e_extra = jnp.exp(extra - m)
                denom = jnp.sum(e, axis=-1, keepdims=True) + e_extra + jnp.exp(sink - m)
                probs = jnp.concatenate([jnp.where(own, 0.0, e).astype(BF16),
                                         jnp.where(own, e, 0.0).astype(BF16)], axis=1)
                vals = jnp.concatenate([v_blk[b][kvh][par], v_blk[b + 1][kvh][par]], axis=0)
                part = (_dot(probs, vals) + e_extra * vf_blk[b][kvh][par]) * (1.0 / denom)
                acc = part if acc is None else acc + part
            o_ref[rows, p * LANES:(p + 1) * LANES] = acc.astype(BF16)


def _swa_prompt(q, kv, sinks):
    return pl.pallas_call(
        _swa_prompt_kernel,
        grid=(PROMPT_TILES,),
        in_specs=[pl.BlockSpec(memory_space=pltpu.SMEM),
                  pl.BlockSpec((TM, Q_DIM), lambda i: (i, 0)),
                  pl.BlockSpec((TM, 2 * KV_DIM), lambda i: (i, 0)),
                  pl.BlockSpec((BLOCK, 2 * KV_DIM),
                               lambda i: (jnp.maximum(i * CHUNKS_PER_TILE - 1, 0), 0))],
        out_specs=pl.BlockSpec((TM, Q_DIM), lambda i: (i, 0)),
        out_shape=jax.ShapeDtypeStruct((N_PROMPT, Q_DIM), BF16),
        scratch_shapes=[pltpu.VMEM((N_HEADS, BLOCK, BLOCK), F32)],
        compiler_params=_params(),
        name="swa_prompt",
    )(sinks, q, kv, kv)


SAMPLE_TILE = 32


def _swa_sample_kernel(q_ref, kvn_ref, ck_ref, cv_ref, slope_ref, sink_ref, o_ref):
    shape = (SAMPLE_TILE, N_PAIRS, LANES)
    lane = lax.broadcasted_iota(jnp.int32, shape, 2)
    pair = lax.broadcasted_iota(jnp.int32, shape, 1)
    low = lane < HEAD_DIM
    kv0 = pair < (N_PAIRS // 2)

    def swap(x):
        return pltpu.roll(x, HEAD_DIM, 2)

    q = q_ref[...].astype(F32)
    q_even = jnp.where(low, q, 0.0)
    q_odd = jnp.where(low, 0.0, q)
    q_al = (jnp.where(kv0, q_even, swap(q_even)), jnp.where(kv0, swap(q_odd), q_odd))

    ck = ck_ref[...].astype(BF16)
    cv = cv_ref[...].astype(BF16)
    kvn = kvn_ref[...]
    k_new = kvn[:, :, :KV_DIM]
    v_new = kvn[:, :, KV_DIM:]
    r = lax.broadcasted_iota(jnp.int32, (SAMPLE_TILE, N_PAIRS, WINDOW), 2)
    dist = (WINDOW - r).astype(F32)

    outs = []
    for par in range(2):
        qa = q_al[par]
        slope = slope_ref[par]
        sink = sink_ref[par]
        logits = lax.dot_general(qa.astype(BF16), ck, (((2,), (2,)), ((0,), (0,))),
                                 preferred_element_type=F32) - slope * dist
        l_self = jnp.sum(qa * k_new, axis=-1, keepdims=True)
        m = jnp.maximum(jnp.maximum(jnp.max(logits, axis=-1, keepdims=True), l_self), sink)
        e = jnp.exp(logits - m)
        e_self = jnp.exp(l_self - m)
        inv = 1.0 / (jnp.sum(e, axis=-1, keepdims=True) + e_self + jnp.exp(sink - m))
        o = lax.dot_general((e * inv).astype(BF16), cv, (((2,), (1,)), ((0,), (0,))),
                            preferred_element_type=F32) + (e_self * inv) * v_new
        outs.append(o)
    o_even = jnp.where(kv0, outs[0], swap(outs[0]))
    o_odd = jnp.where(kv0, swap(outs[1]), outs[1])
    o_ref[...] = jnp.where(low, o_even, o_odd).astype(BF16)


def _swa_sample(q_s, kv_s, cache_k, cache_v, sinks):
    rows = cache_k.shape[1]
    slopes = np.asarray(_SLOPES, np.float32).reshape(N_PAIRS, 2).T.reshape(2, N_PAIRS, 1)
    sink_arr = sinks.astype(F32).reshape(N_PAIRS, 2).T.reshape(2, N_PAIRS, 1)
    blk = lambda *shape: pl.BlockSpec((SAMPLE_TILE,) + shape, lambda i: (i,) + (0,) * len(shape))
    o3 = pl.pallas_call(
        _swa_sample_kernel,
        grid=(DEC_BATCH // SAMPLE_TILE,),
        in_specs=[blk(N_PAIRS, LANES), blk(1, 2 * KV_DIM), blk(rows, KV_DIM), blk(rows, KV_DIM),
                  _const_spec((2, N_PAIRS, 1)), _const_spec((2, N_PAIRS, 1))],
        out_specs=blk(N_PAIRS, LANES),
        out_shape=jax.ShapeDtypeStruct((DEC_BATCH, N_PAIRS, LANES), BF16),
        compiler_params=_params(),
        name="swa_sample",
    )(q_s.reshape(DEC_BATCH, N_PAIRS, LANES), kv_s.reshape(DEC_BATCH, 1, 2 * KV_DIM),
      cache_k.reshape(DEC_BATCH, rows, KV_DIM), cache_v.reshape(DEC_BATCH, rows, KV_DIM),
      jnp.asarray(slopes), sink_arr)
    return o3.reshape(DEC_BATCH, Q_DIM)


def _proj_route_kernel(x_ref, op_ref, os_ref, wo_ref, g_ref, wr_ref, xo_ref, idx_ref, gate_ref,
                       rank_ref, cnt_ref, carry_ref):
    i = pl.program_id(0)

    @pl.when(i == 0)
    def _():
        carry_ref[...] = jnp.zeros_like(carry_ref)

    o_s = jnp.concatenate([os_ref[...], jnp.zeros((TM - DEC_BATCH, Q_DIM), BF16)], axis=0)
    from_sample = (jnp.zeros((TM, 1), jnp.int32) + jnp.where(i == N_TILES - 1, 1, 0)) == 1
    o = jnp.where(from_sample, o_s, op_ref[...])
    x = x_ref[...] + _dot(o, wo_ref[...])
    _store_row_tiles(xo_ref, x)
    h = _rms(x, g_ref[...])
    logits = jnp.dot(h, wr_ref[...], preferred_element_type=F32, precision=lax.Precision.HIGHEST)
    lane = lax.broadcasted_iota(jnp.int32, (TM, LANES), 1)
    logits = jnp.where(lane < N_EXPERTS, logits, -jnp.inf)
    m0 = jnp.max(logits, axis=-1, keepdims=True)
    i0 = jnp.min(jnp.where(logits == m0, lane, LANES), axis=-1, keepdims=True)
    rest = jnp.where(lane == i0, -jnp.inf, logits)
    m1 = jnp.max(rest, axis=-1, keepdims=True)
    i1 = jnp.min(jnp.where(rest == m1, lane, LANES), axis=-1, keepdims=True)
    e1 = jnp.exp(m1 - m0)
    g0 = 1.0 / (1.0 + e1)
    g1 = e1 * g0
    idx_ref[...] = jnp.concatenate([i0, i1], axis=1)
    gate_ref[...] = jnp.concatenate([g0, g1], axis=1)

    onehot = jnp.where((lane == i0) | (lane == i1), 1.0, 0.0)
    r = lax.broadcasted_iota(jnp.int32, (TM, TM), 0)
    c = lax.broadcasted_iota(jnp.int32, (TM, TM), 1)
    before = jnp.where(c < r, 1.0, 0.0).astype(BF16)
    ranks = _dot(before, onehot.astype(BF16)) + carry_ref[...]
    r0 = jnp.sum(jnp.where(lane == i0, ranks, 0.0), axis=-1, keepdims=True)
    r1 = jnp.sum(jnp.where(lane == i1, ranks, 0.0), axis=-1, keepdims=True)
    rank_ref[...] = jnp.concatenate([r0, r1], axis=1).astype(jnp.int32)
    carry_ref[...] = carry_ref[...] + jnp.sum(onehot, axis=0, keepdims=True)
    cnt_ref[...] = carry_ref[...].astype(jnp.int32)


def _proj_route(x, o_prompt, o_sample, w_o, g, w_router):
    tile = pl.BlockSpec((TM, D_MODEL), lambda i: (i, 0))
    pair = pl.BlockSpec((TM, 2), lambda i: (i, 0))
    wr = jnp.zeros((D_MODEL, LANES), F32).at[:, :N_EXPERTS].set(w_router)
    return pl.pallas_call(
        _proj_route_kernel,
        grid=(N_TILES,),
        in_specs=[tile,
                  pl.BlockSpec((TM, Q_DIM), lambda i: (jnp.minimum(i, PROMPT_TILES - 1), 0)),
                  pl.BlockSpec((DEC_BATCH, Q_DIM), lambda i: (0, 0)),
                  _const_spec((Q_DIM, D_MODEL)), _const_spec((1, D_MODEL)),
                  _const_spec((D_MODEL, LANES))],
        out_specs=[pl.BlockSpec((TM * SUBLANES, LANES), lambda i: (i, 0)), pair, pair, pair,
                   pl.BlockSpec((1, LANES), lambda i: (0, 0))],
        out_shape=[jax.ShapeDtypeStruct((T_PAD * SUBLANES, LANES), F32),
                   jax.ShapeDtypeStruct((T_PAD, 2), jnp.int32),
                   jax.ShapeDtypeStruct((T_PAD, 2), F32),
                   jax.ShapeDtypeStruct((T_PAD, 2), jnp.int32),
                   jax.ShapeDtypeStruct((1, LANES), jnp.int32)],
        scratch_shapes=[pltpu.VMEM((1, LANES), F32)],
        compiler_params=_params(),
        name="proj_route",
    )(x, o_prompt, o_sample, w_o.astype(BF16), g.reshape(1, -1), wr)


def _tile_rows(r):
    start = r * SUBLANES
    return pl.ds(start if isinstance(start, int) else pl.multiple_of(start, SUBLANES), SUBLANES)


def _gather_row(x_hbm, xbuf, sem, slot, tok, r):
    return pltpu.make_async_copy(x_hbm.at[_tile_rows(tok), :], xbuf.at[slot, _tile_rows(r), :],
                                 sem.at[slot])


def _scatter_row(obuf, y_hbm, sem, slot, r, dest):
    return pltpu.make_async_copy(obuf.at[slot, _tile_rows(r), :], y_hbm.at[_tile_rows(dest), :],
                                 sem.at[slot])


def _gather_wait(x_hbm, xbuf, sem, slot):
    pltpu.make_async_copy(x_hbm.at[pl.ds(0, TMM * SUBLANES), :], xbuf.at[slot], sem.at[slot]).wait()


def _scatter_wait(obuf, y_hbm, sem, slot):
    pltpu.make_async_copy(obuf.at[slot], y_hbm.at[pl.ds(0, TMM * SUBLANES), :], sem.at[slot]).wait()


def _slot_token(entry):
    return entry & SLOT_MASK


def _slot_dest(entry):
    return (entry >> SLOT_SHIFT) * T_PAD + (entry & SLOT_MASK)


def _moe_kernel(te_ref, nu_ref, slot_ref, x_hbm, g_ref, wg_ref, wu_ref, wd_ref, y_hbm,
                xbuf, hbuf, acc, obuf, gsem, ssem):
    i = pl.program_id(0)
    f = pl.program_id(1)
    n_used = nu_ref[0]
    slot = i % 2
    other = 1 - slot

    @pl.when(i < n_used)
    def _():
        @pl.when((i == 0) & (f == 0))
        def _():
            def start(r, carry):
                _gather_row(x_hbm, xbuf, gsem, 0, _slot_token(slot_ref[TMM + r]), r).start()
                return carry
            lax.fori_loop(0, TMM, start, 0)
            obuf[...] = jnp.zeros_like(obuf)
            for s in range(2):
                spare = pltpu.make_async_copy(
                    obuf.at[s],
                    y_hbm.at[pl.ds((2 * T_PAD + s * TMM) * SUBLANES, TMM * SUBLANES), :],
                    ssem.at[s])
                spare.start()
                spare.wait()

        @pl.when(f == 0)
        def _():
            _gather_wait(x_hbm, xbuf, gsem, slot)
            xin = xbuf.at[slot]
            chunks = [xin[_chunk(s, TMM), :] for s in range(ROW_CHUNKS)]
            ssq = chunks[0] * chunks[0]
            for c in chunks[1:]:
                ssq = ssq + c * c
            scale = lax.rsqrt(jnp.sum(ssq, axis=-1, keepdims=True) * (1.0 / D_MODEL) + EPS)
            for s, c in enumerate(chunks):
                cols = slice(s * LANES, (s + 1) * LANES)
                hbuf[:, cols] = (c * scale * g_ref[:, cols]).astype(BF16)
            acc[...] = jnp.zeros_like(acc)

        row0 = f * ROWS_PER_STEP
        nxt = (i + 2) * TMM + row0
        prv = i * TMM + row0
        for r in range(ROWS_PER_STEP):
            _gather_row(x_hbm, xbuf, gsem, other, _slot_token(slot_ref[nxt + r]), row0 + r).start()
            _scatter_row(obuf, y_hbm, ssem, other, row0 + r, _slot_dest(slot_ref[prv + r])).start()

        h = hbuf[...]
        a = (_silu(_dot(h, wg_ref[...])) * _dot(h, wu_ref[...])).astype(BF16)
        acc[...] += _dot(a, wd_ref[...])

        @pl.when(f == NF - 1)
        def _():
            @pl.when(i >= 1)
            def _():
                _scatter_wait(obuf, y_hbm, ssem, slot)
            _store_row_tiles(obuf.at[slot], acc[...])

        @pl.when((f == NF - 1) & (i == n_used - 1))
        def _():
            _gather_wait(x_hbm, xbuf, gsem, other)
            _scatter_wait(obuf, y_hbm, ssem, other)
            base = (i + 1) * TMM

            def start(r, carry):
                _scatter_row(obuf, y_hbm, ssem, slot, r, _slot_dest(slot_ref[base + r])).start()
                return carry
            lax.fori_loop(0, TMM, start, 0)
            _scatter_wait(obuf, y_hbm, ssem, slot)


def _moe_layer(x, g, tile_expert, n_used, slots, layer, w_gate, w_up, w_down):
    def f_eff(i, f, nu):
        return jnp.where(i < nu[0], f, NF - 1)

    grid_spec = pltpu.PrefetchScalarGridSpec(
        num_scalar_prefetch=3,
        grid=(NT_MOE, NF),
        in_specs=[pl.BlockSpec(memory_space=pl.ANY),
                  pl.BlockSpec((1, D_MODEL), lambda i, f, te, nu, tok: (0, 0)),
                  pl.BlockSpec((None, None, D_MODEL, TF),
                               lambda i, f, te, nu, tok: (layer, te[i], 0, f_eff(i, f, nu))),
                  pl.BlockSpec((None, None, D_MODEL, TF),
                               lambda i, f, te, nu, tok: (layer, te[i], 0, f_eff(i, f, nu))),
                  pl.BlockSpec((None, None, TF, D_MODEL),
                               lambda i, f, te, nu, tok: (layer, te[i], f_eff(i, f, nu), 0))],
        out_specs=pl.BlockSpec(memory_space=pl.ANY),
        scratch_shapes=[pltpu.VMEM((2, TMM * SUBLANES, LANES), F32),
                        pltpu.VMEM((TMM, D_MODEL), BF16),
                        pltpu.VMEM((TMM, D_MODEL), F32),
                        pltpu.VMEM((2, TMM * SUBLANES, LANES), F32),
                        pltpu.SemaphoreType.DMA((2,)),
                        pltpu.SemaphoreType.DMA((2,))],
    )
    return pl.pallas_call(
        _moe_kernel,
        grid_spec=grid_spec,
        out_shape=jax.ShapeDtypeStruct((Y_ROWS * SUBLANES, LANES), F32),
        compiler_params=_params(2),
        name="moe_experts",
    )(tile_expert, n_used, slots, x, g.reshape(1, -1), w_gate, w_up, w_down)


def _final_kernel(x_ref, gate_ref, y0_ref, y1_ref, gf_ref, yp_ref, ys_ref):
    i = pl.program_id(0)
    y = _rms(_combined_rows(x_ref, gate_ref, y0_ref, y1_ref), gf_ref[...])

    @pl.when(i < PROMPT_TILES)
    def _():
        yp_ref[...] = y

    @pl.when(i == N_TILES - 1)
    def _():
        ys_ref[...] = y[:DEC_BATCH]


def _final_combine(x, gates, y_slots, g_final):
    return pl.pallas_call(
        _final_kernel,
        grid=(N_TILES,),
        in_specs=_row_tile_specs() + [_const_spec((1, D_MODEL))],
        out_specs=[pl.BlockSpec((TM, D_MODEL), lambda i: (jnp.minimum(i, PROMPT_TILES - 1), 0)),
                   pl.BlockSpec((DEC_BATCH, D_MODEL), lambda i: (0, 0))],
        out_shape=[jax.ShapeDtypeStruct((N_PROMPT, D_MODEL), F32),
                   jax.ShapeDtypeStruct((DEC_BATCH, D_MODEL), F32)],
        compiler_params=_params(),
        name="final_combine",
    )(x, gates, y_slots, y_slots, g_final.reshape(1, -1))


def _padding_slot_entries():
    p = np.arange(-TMM, P_TOTAL + TMM)
    spare_row = ((p // TMM) % 2) * TMM + p % TMM
    return ((2 << SLOT_SHIFT) | spare_row).astype(np.int32)


SLOT_ENTRIES = P_TOTAL + 2 * TMM
FILL_UNROLL = 8


def _slot_fill_kernel(pos_ref, init_hbm, slots_hbm, buf, sem):
    load = pltpu.make_async_copy(init_hbm, buf, sem)
    load.start()
    load.wait()

    def body(j, carry):
        for u in range(FILL_UNROLL):
            t = j * FILL_UNROLL + u
            buf[TMM + pos_ref[2 * t]] = t
            buf[TMM + pos_ref[2 * t + 1]] = t | (1 << SLOT_SHIFT)
        return carry

    lax.fori_loop(0, T_PAD // FILL_UNROLL, body, 0)
    store = pltpu.make_async_copy(buf, slots_hbm, sem)
    store.start()
    store.wait()


def _slot_fill(pos):
    return pl.pallas_call(
        _slot_fill_kernel,
        in_specs=[pl.BlockSpec(memory_space=pltpu.SMEM), pl.BlockSpec(memory_space=pl.ANY)],
        out_specs=pl.BlockSpec(memory_space=pl.ANY),
        out_shape=jax.ShapeDtypeStruct((SLOT_ENTRIES,), jnp.int32),
        scratch_shapes=[pltpu.SMEM((SLOT_ENTRIES,), jnp.int32), pltpu.SemaphoreType.DMA(())],
        name="slot_fill",
    )(pos.reshape(-1), jnp.asarray(_padding_slot_entries()))


def _slot_plan(idx, rank, counts):
    cnt = counts[0, :N_EXPERTS]
    tiles = (cnt + TMM - 1) // TMM
    tile_end = jnp.cumsum(tiles)
    start = (tile_end - tiles) * TMM
    n_used = tile_end[-1:].astype(jnp.int32)
    pos = (start[idx] + rank).astype(jnp.int32)
    slots = _slot_fill(pos)
    tile_ids = jnp.arange(NT_MOE, dtype=jnp.int32)
    tile_expert = jnp.minimum(jnp.sum(tile_end[None, :] <= tile_ids[:, None], axis=1),
                              N_EXPERTS - 1).astype(jnp.int32)
    last_expert = tile_expert[jnp.maximum(n_used[0] - 1, 0)]
    tile_expert = jnp.where(tile_ids < n_used[0], tile_expert, last_expert)
    return slots, tile_expert, n_used


def kernel(x_prompt, x_sample, cache_swa_k, cache_swa_v, norm_mix, norm_ffn, norm_final,
           sgu_w_in, sgu_b_in, sgu_ln_g, sgu_ln_b, sgu_w_s, sgu_b_s, sgu_w_out,
           attn_w_qkv, attn_sinks, attn_w_o,
           ffn_w_gate, ffn_w_up, ffn_w_down,
           moe_w_router, moe_w_gate, moe_w_up, moe_w_down):
    moe_wg, moe_wu, moe_wd = (w.astype(BF16) for w in (moe_w_gate, moe_w_up, moe_w_down))
    sgu_v_p, sgu_v_s, k_p, v_p, k_s, v_s = [], [], [], [], [], []
    rows_p = min(WINDOW, SEQ)
    rows, source = (x_prompt.reshape(N_PROMPT, D_MODEL), x_sample.reshape(DEC_BATCH, D_MODEL)), "inputs"
    for i in range(DEPTH):
        j = i // 2
        if i % 2 == 0:
            x, vlast = _sgu_layer(rows, source, norm_mix[i], sgu_w_in[j], sgu_b_in[j], sgu_ln_g[j],
                                  sgu_ln_b[j], sgu_w_s[j], sgu_b_s[j], sgu_w_out[j])
            sgu_v_p.append(vlast[:BATCH * CHUNK].reshape(BATCH, CHUNK, SGU_WIDTH))
            sgu_v_s.append(vlast[BATCH * CHUNK:].reshape(DEC_BATCH, 1, SGU_WIDTH))
            x = _ffn_layer(x, norm_ffn[i], ffn_w_gate[j], ffn_w_up[j], ffn_w_down[j])
        else:
            q, kv = _qkv_layer(x, norm_mix[i], attn_w_qkv[j])
            kv_p = kv[:N_PROMPT].reshape(BATCH, SEQ, 2 * KV_DIM)[:, SEQ - rows_p:]
            k_p.append(kv_p[..., :KV_DIM].reshape(BATCH, rows_p, N_KV_HEADS, HEAD_DIM))
            v_p.append(kv_p[..., KV_DIM:].reshape(BATCH, rows_p, N_KV_HEADS, HEAD_DIM))
            kv_s = kv[N_PROMPT:T_REAL]
            k_new = kv_s[:, None, :KV_DIM].reshape(DEC_BATCH, 1, N_KV_HEADS, HEAD_DIM)
            v_new = kv_s[:, None, KV_DIM:].reshape(DEC_BATCH, 1, N_KV_HEADS, HEAD_DIM)
            k_s.append(jnp.concatenate([cache_swa_k[j][:, 1:], k_new], axis=1))
            v_s.append(jnp.concatenate([cache_swa_v[j][:, 1:], v_new], axis=1))

            o = _swa_prompt(q, kv, attn_sinks[j].astype(F32))
            o_s = _swa_sample(q[N_PROMPT:T_REAL], kv_s, cache_swa_k[j], cache_swa_v[j], attn_sinks[j])
            x, idx, gates, rank, counts = _proj_route(x, o, o_s, attn_w_o[j], norm_ffn[i],
                                                      moe_w_router[j])
            slots, tile_expert, n_used = _slot_plan(idx, rank, counts)
            y_slots = _moe_layer(x, norm_ffn[i], tile_expert, n_used, slots, j,
                                 moe_wg, moe_wu, moe_wd)
            rows, source = (x, gates, y_slots), "experts"
    y_prompt, y_sample = _final_combine(*rows, norm_final)
    y_prompt = y_prompt.reshape(BATCH, SEQ, D_MODEL)
    y_sample = y_sample.reshape(DEC_BATCH, 1, D_MODEL)
    return (y_prompt, y_sample, jnp.stack(sgu_v_p), jnp.stack(sgu_v_s),
            jnp.stack(k_p), jnp.stack(v_p), jnp.stack(k_s), jnp.stack(v_s))
```

```python
import functools

import numpy as np
import jax
import jax.numpy as jnp
from jax import lax
from jax.experimental import pallas as pl
from jax.experimental.pallas import tpu as pltpu

D_MODEL = 1024
BATCH = 4
SEQ = 4096
DEPTH = 4
DEC_BATCH = 128
PAST_LEN = 8192
CHUNK = 128
SGU_WIDTH = 2 * D_MODEL
SGU_GROUPS = 8
SGU_GROUP_DIM = SGU_WIDTH // SGU_GROUPS
WINDOW = 128
BLOCK = 128
HEAD_DIM = 64
N_HEADS = D_MODEL // HEAD_DIM
N_KV_HEADS = 2
GQA_GROUP = N_HEADS // N_KV_HEADS
Q_DIM = N_HEADS * HEAD_DIM
KV_DIM = N_KV_HEADS * HEAD_DIM
D_FF = 2816
N_EXPERTS = 8
D_FF_EXPERT = 3584
EPS = 1e-6
LN_EPS = 1e-5

F32 = jnp.float32
BF16 = jnp.bfloat16

LANES = 128
TM = 512
N_PROMPT = BATCH * SEQ
T_REAL = N_PROMPT + DEC_BATCH
N_TILES = -(-T_REAL // TM)
T_PAD = N_TILES * TM
PROMPT_TILES = N_PROMPT // TM
TILES_PER_SEQ = SEQ // TM
CHUNKS_PER_TILE = TM // CHUNK
N_PAIRS = N_HEADS // 2

TMM = 512
TF = 1792
NF = D_FF_EXPERT // TF
ROWS_PER_STEP = TMM // NF
N_SLOTS = 2 * T_PAD
NT_MOE = (N_SLOTS + N_EXPERTS * (TMM - 1)) // TMM + 1
P_TOTAL = NT_MOE * TMM
Y_ROWS = 2 * T_PAD + 2 * TMM
SLOT_ENTRIES = P_TOTAL + 2 * TMM

VMEM_LIMIT = 56 * 1024 * 1024

_SLOPES = [2.0 ** (-8.0 * (h + 1) / N_HEADS) for h in range(N_HEADS)]


def _rms(x, g):
    return x * lax.rsqrt(jnp.mean(x * x, axis=-1, keepdims=True) + EPS) * g


def _gelu(x):
    k = -2.0 * np.sqrt(2.0 / np.pi) * np.log2(np.e)
    t = (x * x) * np.float32(0.044715 * k) + np.float32(k)
    return x * (1.0 / (1.0 + jnp.exp2(x * t)))


def _silu(x):
    return x * (1.0 / (1.0 + jnp.exp(-x)))


def _dot(a, b):
    return jnp.dot(a, b, preferred_element_type=F32)


def _split(x):
    hi = x.astype(BF16)
    return hi, (x - hi.astype(F32)).astype(BF16)


def _operand(x, precise):
    return _split(x) if precise else x.astype(BF16)


def _mm(x, w_hi, w_lo):
    if not isinstance(x, tuple):
        return _dot(x, w_hi)
    x_hi, x_lo = x
    return _dot(x_hi, w_hi) + (_dot(x_lo, w_hi) + _dot(x_hi, w_lo))


def _hi_lo(w):
    hi = lax.reduce_precision(w, exponent_bits=8, mantissa_bits=7)
    return hi.astype(BF16), (w - hi).astype(BF16)


def _dot_t(a, b):
    return lax.dot_general(a, b, (((1,), (1,)), ((), ())), preferred_element_type=F32)


SUBLANES = 8
ROW_CHUNKS = D_MODEL // LANES


def _chunk(s, rows):
    return pl.ds(s, rows, stride=SUBLANES)


def _load_row_tiles(ref, rows):
    return jnp.concatenate([ref[_chunk(s, rows), :] for s in range(ROW_CHUNKS)], axis=1)


def _store_row_tiles(ref, x):
    rows = x.shape[0]
    for s in range(ROW_CHUNKS):
        ref[_chunk(s, rows), :] = x[:, s * LANES:(s + 1) * LANES]


def _const_spec(shape):
    nd = len(shape)
    return pl.BlockSpec(shape, lambda *_: (0,) * nd, pipeline_mode=pl.Buffered(1))


def _params(n_axes=1):
    return pltpu.CompilerParams(dimension_semantics=("arbitrary",) * n_axes,
                                vmem_limit_bytes=VMEM_LIMIT)


def _combined_rows(x_ref, gate_ref, y0_ref, y1_ref, rows=TM):
    gate = gate_ref[:rows, :]
    g0, g1 = gate[:, 0:1], gate[:, 1:2]
    return jnp.concatenate(
        [x_ref[_chunk(s, rows), :] + (g0 * y0_ref[_chunk(s, rows), :] + g1 * y1_ref[_chunk(s, rows), :])
         for s in range(ROW_CHUNKS)], axis=1)


def _sgu_kernel(*refs, source):
    i = pl.program_id(0)
    is_sample = i == N_TILES - 1
    n_src = 2 if source == "inputs" else 4
    src, refs = refs[:n_src], refs[n_src:]
    (g_ref, win_hi, win_lo, bin_ref, lng_ref, lnb_ref, ws_ref, bs_ref, wout_hi, wout_lo,
     xo_ref, vlast_ref) = refs

    def mix(x, sample):
        h = _operand(_rms(x, g_ref[...]), sample)
        v = _gelu(_mm(h, win_hi[:, SGU_WIDTH:], win_lo[:, SGU_WIDTH:]) + bin_ref[:, SGU_WIDTH:])
        mu = jnp.mean(v, axis=-1, keepdims=True)
        vc = v - mu
        var = jnp.mean(vc * vc, axis=-1, keepdims=True)
        vn = vc * lax.rsqrt(var + LN_EPS) * lng_ref[...] + lnb_ref[...]
        vb = vn.astype(BF16)
        row = lax.broadcasted_iota(jnp.int32, (CHUNK, CHUNK), 0)
        col = lax.broadcasted_iota(jnp.int32, (CHUNK, CHUNK), 1)
        y = jnp.zeros_like(x)
        for g in range(SGU_GROUPS):
            lo, hi = g * SGU_GROUP_DIM, (g + 1) * SGU_GROUP_DIM
            u = _gelu(_mm(h, win_hi[:, lo:hi], win_lo[:, lo:hi]) + bin_ref[:, lo:hi])
            if sample:
                gate = vn[:, lo:hi] * ws_ref[g][0:1, 0:1] + bs_ref[g][0:1, 0:1]
            else:
                w_tril = jnp.where(row >= col, ws_ref[g], 0.0).astype(BF16)
                gate = jnp.concatenate(
                    [_dot(w_tril, vb[c * CHUNK:(c + 1) * CHUNK, lo:hi]) + bs_ref[g]
                     for c in range(x.shape[0] // CHUNK)], axis=0)
            y = y + _mm(_operand(u * gate, sample), wout_hi[lo:hi, :], wout_lo[lo:hi, :])
        return x + y, vn

    @pl.when(jnp.logical_not(is_sample))
    def _():
        x = src[0][...] if source == "inputs" else _combined_rows(*src)
        x_new, vn = mix(x, sample=False)
        xo_ref[...] = x_new
        vlast_ref[...] = vn[TM - CHUNK:]

    @pl.when(is_sample)
    def _():
        x = src[1][...] if source == "inputs" else _combined_rows(*src, rows=DEC_BATCH)
        x_new, vn = mix(x, sample=True)
        xo_ref[:DEC_BATCH, :] = x_new
        xo_ref[DEC_BATCH:, :] = jnp.zeros((TM - DEC_BATCH, D_MODEL), F32)
        vlast_ref[...] = vn


def _row_tile_specs():
    return [pl.BlockSpec((TM * SUBLANES, LANES), lambda i: (i, 0)),
            pl.BlockSpec((TM, 2), lambda i: (i, 0)),
            pl.BlockSpec((TM * SUBLANES, LANES), lambda i: (i, 0)),
            pl.BlockSpec((TM * SUBLANES, LANES), lambda i: (N_TILES + i, 0))]


def _sgu_layer(rows, source, g, w_in, b_in, ln_g, ln_b, w_s, b_s, w_out):
    tile = pl.BlockSpec((TM, D_MODEL), lambda i: (i, 0))
    vlast_spec = pl.BlockSpec(
        (CHUNK, SGU_WIDTH),
        lambda i: (jnp.where(i == N_TILES - 1, BATCH, i // TILES_PER_SEQ), 0))
    if source == "inputs":
        row_specs = [pl.BlockSpec((TM, D_MODEL), lambda i: (jnp.minimum(i, PROMPT_TILES - 1), 0)),
                     pl.BlockSpec((DEC_BATCH, D_MODEL), lambda i: (0, 0))]
        row_args = rows
    else:
        row_specs = _row_tile_specs()
        x, gates, y_slots = rows
        row_args = (x, gates, y_slots, y_slots)
    return pl.pallas_call(
        functools.partial(_sgu_kernel, source=source),
        grid=(N_TILES,),
        in_specs=row_specs + [
                  _const_spec((1, D_MODEL)),
                  _const_spec((D_MODEL, 2 * SGU_WIDTH)),
                  _const_spec((D_MODEL, 2 * SGU_WIDTH)),
                  _const_spec((1, 2 * SGU_WIDTH)),
                  _const_spec((1, SGU_WIDTH)),
                  _const_spec((1, SGU_WIDTH)),
                  _const_spec((SGU_GROUPS, CHUNK, CHUNK)),
                  _const_spec((SGU_GROUPS, CHUNK, 1)),
                  _const_spec((SGU_WIDTH, D_MODEL)),
                  _const_spec((SGU_WIDTH, D_MODEL))],
        out_specs=[tile, vlast_spec],
        out_shape=[jax.ShapeDtypeStruct((T_PAD, D_MODEL), F32),
                   jax.ShapeDtypeStruct(((BATCH + 1) * CHUNK, SGU_WIDTH), F32)],
        compiler_params=_params(),
        name="sgu_mixer",
    )(*row_args, g.reshape(1, -1), *_hi_lo(w_in), b_in.reshape(1, -1), ln_g.reshape(1, -1),
      ln_b.reshape(1, -1), w_s, b_s.reshape(SGU_GROUPS, CHUNK, 1), *_hi_lo(w_out))


FF_SPLIT = 2


def _ffn_kernel(x_ref, g_ref, wg_hi, wg_lo, wu_hi, wu_lo, wd_hi, wd_lo, xo_ref):
    is_sample = pl.program_id(0) == N_TILES - 1
    width = D_FF // FF_SPLIT

    def ffn(x, sample):
        h = _operand(_rms(x, g_ref[...]), sample)
        y = x
        for c in range(FF_SPLIT):
            cols = slice(c * width, (c + 1) * width)
            a = _silu(_mm(h, wg_hi[:, cols], wg_lo[:, cols])) * _mm(h, wu_hi[:, cols], wu_lo[:, cols])
            y = y + _mm(_operand(a, sample), wd_hi[cols, :], wd_lo[cols, :])
        return y

    @pl.when(jnp.logical_not(is_sample))
    def _():
        xo_ref[...] = ffn(x_ref[...], sample=False)

    @pl.when(is_sample)
    def _():
        xo_ref[:DEC_BATCH, :] = ffn(x_ref[:DEC_BATCH, :], sample=True)
        xo_ref[DEC_BATCH:, :] = jnp.zeros((TM - DEC_BATCH, D_MODEL), F32)


def _ffn_layer(x, g, w_gate, w_up, w_down):
    tile = pl.BlockSpec((TM, D_MODEL), lambda i: (i, 0))
    return pl.pallas_call(
        _ffn_kernel,
        grid=(N_TILES,),
        in_specs=[tile, _const_spec((1, D_MODEL))]
        + [_const_spec((D_MODEL, D_FF))] * 4 + [_const_spec((D_FF, D_MODEL))] * 2,
        out_specs=tile,
        out_shape=jax.ShapeDtypeStruct((T_PAD, D_MODEL), F32),
        compiler_params=_params(),
        name="dense_swiglu",
    )(x, g.reshape(1, -1), *_hi_lo(w_gate), *_hi_lo(w_up), *_hi_lo(w_down))


def _qkv_kernel(x_ref, g_ref, w_hi, w_lo, q_ref, qs_ref, kv_ref):
    is_sample = pl.program_id(0) == N_TILES - 1

    def qkv(x, sample):
        out = _mm(_operand(_rms(x, g_ref[...]), sample), w_hi[...], w_lo[...])
        return out[:, :Q_DIM] * (HEAD_DIM ** -0.5), out[:, Q_DIM:]

    @pl.when(jnp.logical_not(is_sample))
    def _():
        q, kv = qkv(x_ref[...], sample=False)
        q_ref[...] = q.astype(BF16)
        kv_ref[...] = kv

    @pl.when(is_sample)
    def _():
        q, kv = qkv(x_ref[:DEC_BATCH, :], sample=True)
        qs_ref[...] = q
        kv_ref[:DEC_BATCH, :] = kv
        kv_ref[DEC_BATCH:, :] = jnp.zeros((TM - DEC_BATCH, 2 * KV_DIM), F32)


def _qkv_layer(x, g, w_qkv):
    tile = pl.BlockSpec((TM, D_MODEL), lambda i: (i, 0))
    return pl.pallas_call(
        _qkv_kernel,
        grid=(N_TILES,),
        in_specs=[tile, _const_spec((1, D_MODEL)),
                  _const_spec((D_MODEL, Q_DIM + 2 * KV_DIM)), _const_spec((D_MODEL, Q_DIM + 2 * KV_DIM))],
        out_specs=[pl.BlockSpec((TM, Q_DIM), lambda i: (jnp.minimum(i, PROMPT_TILES - 1), 0)),
                   pl.BlockSpec((DEC_BATCH, Q_DIM), lambda i: (0, 0)),
                   pl.BlockSpec((TM, 2 * KV_DIM), lambda i: (i, 0))],
        out_shape=[jax.ShapeDtypeStruct((N_PROMPT, Q_DIM), BF16),
                   jax.ShapeDtypeStruct((DEC_BATCH, Q_DIM), F32),
                   jax.ShapeDtypeStruct((T_PAD, 2 * KV_DIM), F32)],
        compiler_params=_params(),
        name="swa_qkv",
    )(x, g.reshape(1, -1), *_hi_lo(w_qkv))


def _half_masks_f32(x):
    lane = lax.broadcasted_iota(jnp.int32, x.shape, 1)
    low = lane < HEAD_DIM
    xr = pltpu.roll(x, HEAD_DIM, 1)
    return ((jnp.where(low, x, 0.0), jnp.where(low, 0.0, xr)),
            (jnp.where(low, xr, 0.0), jnp.where(low, 0.0, x)))


def _half_masks(x):
    return tuple(tuple(m.astype(BF16) for m in pair) for pair in _half_masks_f32(x))


def _swa_prompt_kernel(sink_ref, q_ref, kv_ref, kvp_ref, o_ref, bias_ref):
    i = pl.program_id(0)
    t = lax.broadcasted_iota(jnp.int32, (BLOCK, BLOCK), 0)
    c = lax.broadcasted_iota(jnp.int32, (BLOCK, BLOCK), 1)
    own = c <= t
    diag = c == t

    @pl.when(i == 0)
    def _():
        dist = jnp.where(own, t - c, BLOCK + t - c).astype(F32)
        for hd in range(N_HEADS):
            bias_ref[hd] = _SLOPES[hd] * dist

    first_tile = (i % TILES_PER_SEQ) == 0
    kv_all = jnp.concatenate([kvp_ref[...], kv_ref[...]], axis=0)
    k_blk, v_blk, vf_blk = [], [], []
    for b in range(CHUNKS_PER_TILE + 1):
        blk = kv_all[b * BLOCK:(b + 1) * BLOCK]
        k_blk.append(_half_masks(blk[:, :KV_DIM]))
        v_blk.append(_half_masks(blk[:, KV_DIM:]))
        vf_blk.append(_half_masks_f32(blk[:, KV_DIM:]))
    has_prev = (jnp.zeros((BLOCK, BLOCK), jnp.int32) + jnp.where(first_tile, 0, 1)) == 1

    for b in range(CHUNKS_PER_TILE):
        rows = slice(b * BLOCK, (b + 1) * BLOCK)
        for p in range(N_PAIRS):
            kvh = (2 * p) // GQA_GROUP
            qp = q_ref[rows, p * LANES:(p + 1) * LANES]
            acc = None
            for par in range(2):
                hd = 2 * p + par
                sink = sink_ref[hd]
                keys = jnp.concatenate([k_blk[b][kvh][par], k_blk[b + 1][kvh][par]], axis=0)
                both = _dot_t(qp, keys)
                l_prev, l_own = both[:, :BLOCK], both[:, BLOCK:]
                if b == 0:
                    l_prev = jnp.where(has_prev, l_prev, -jnp.inf)
                logits = jnp.where(own, l_own, l_prev) - bias_ref[hd]
                extra = (jnp.sum(jnp.where(diag, l_prev, 0.0), axis=-1, keepdims=True)
                         - _SLOPES[hd] * BLOCK)
                m = jnp.maximum(jnp.maximum(jnp.max(logits, axis=-1, keepdims=True), extra), sink)
                e = jnp.exp(logits - m)
                e_extra = jnp.exp(extra - m)
                denom = jnp.sum(e, axis=-1, keepdims=True) + e_extra + jnp.exp(sink - m)
                probs = jnp.concatenate([jnp.where(own, 0.0, e).astype(BF16),
                                         jnp.where(own, e, 0.0).astype(BF16)], axis=1)
                vals = jnp.concatenate([v_blk[b][kvh][par], v_blk[b + 1][kvh][par]], axis=0)
                part = (_dot(probs, vals) + e_extra * vf_blk[b][kvh][par]) * (1.0 / denom)
                acc = part if acc is None else acc + part
            o_ref[rows, p * LANES:(p + 1) * LANES] = acc.astype(BF16)


def _swa_prompt(q, kv, sinks):
    return pl.pallas_call(
        _swa_prompt_kernel,
        grid=(PROMPT_TILES,),
        in_specs=[pl.BlockSpec(memory_space=pltpu.SMEM),
                  pl.BlockSpec((TM, Q_DIM), lambda i: (i, 0)),
                  pl.BlockSpec((TM, 2 * KV_DIM), lambda i: (i, 0)),
                  pl.BlockSpec((BLOCK, 2 * KV_DIM),
                               lambda i: (jnp.maximum(i * CHUNKS_PER_TILE - 1, 0), 0))],
        out_specs=pl.BlockSpec((TM, Q_DIM), lambda i: (i, 0)),
        out_shape=jax.ShapeDtypeStruct((N_PROMPT, Q_DIM), BF16),
        scratch_shapes=[pltpu.VMEM((N_HEADS, BLOCK, BLOCK), F32)],
        compiler_params=_params(),
        name="swa_prompt",
    )(sinks, q, kv, kv)


SAMPLE_TILE = 32


def _swa_sample_kernel(q_ref, kvn_ref, ck_ref, cv_ref, slope_ref, sink_ref, o_ref):
    shape = (SAMPLE_TILE, N_PAIRS, LANES)
    lane = lax.broadcasted_iota(jnp.int32, shape, 2)
    pair = lax.broadcasted_iota(jnp.int32, shape, 1)
    low = lane < HEAD_DIM
    kv0 = pair < (N_PAIRS // 2)

    def swap(x):
        return pltpu.roll(x, HEAD_DIM, 2)

    def three_pass(dims, a, b):
        dot = lambda u, v: lax.dot_general(u, v, dims, preferred_element_type=F32)
        return dot(a[0], b[0]) + (dot(a[1], b[0]) + dot(a[0], b[1]))

    qk_dims = (((2,), (2,)), ((0,), (0,)))
    pv_dims = (((2,), (1,)), ((0,), (0,)))

    q = q_ref[...]
    q_even = jnp.where(low, q, 0.0)
    q_odd = jnp.where(low, 0.0, q)
    q_al = (jnp.where(kv0, q_even, swap(q_even)), jnp.where(kv0, swap(q_odd), q_odd))

    ck = _split(ck_ref[...])
    cv = _split(cv_ref[...])
    kvn = kvn_ref[...]
    k_new = kvn[:, :, :KV_DIM]
    v_new = kvn[:, :, KV_DIM:]
    r = lax.broadcasted_iota(jnp.int32, (SAMPLE_TILE, N_PAIRS, WINDOW), 2)
    dist = (WINDOW - r).astype(F32)

    outs = []
    for par in range(2):
        qa = q_al[par]
        slope = slope_ref[par]
        sink = sink_ref[par]
        logits = three_pass(qk_dims, _split(qa), ck) - slope * dist
        l_self = jnp.sum(qa * k_new, axis=-1, keepdims=True)
        m = jnp.maximum(jnp.maximum(jnp.max(logits, axis=-1, keepdims=True), l_self), sink)
        e = jnp.exp(logits - m)
        e_self = jnp.exp(l_self - m)
        inv = 1.0 / (jnp.sum(e, axis=-1, keepdims=True) + e_self + jnp.exp(sink - m))
        o = three_pass(pv_dims, _split(e * inv), cv) + (e_self * inv) * v_new
        outs.append(o)
    o_even = jnp.where(kv0, outs[0], swap(outs[0]))
    o_odd = jnp.where(kv0, swap(outs[1]), outs[1])
    o_ref[...] = jnp.where(low, o_even, o_odd)


def _swa_sample(q_s, kv_s, cache_k, cache_v, sinks):
    rows = cache_k.shape[1]
    slopes = np.asarray(_SLOPES, np.float32).reshape(N_PAIRS, 2).T.reshape(2, N_PAIRS, 1)
    sink_arr = sinks.astype(F32).reshape(N_PAIRS, 2).T.reshape(2, N_PAIRS, 1)
    blk = lambda *shape: pl.BlockSpec((SAMPLE_TILE,) + shape, lambda i: (i,) + (0,) * len(shape))
    o3 = pl.pallas_call(
        _swa_sample_kernel,
        grid=(DEC_BATCH // SAMPLE_TILE,),
        in_specs=[blk(N_PAIRS, LANES), blk(1, 2 * KV_DIM), blk(rows, KV_DIM), blk(rows, KV_DIM),
                  _const_spec((2, N_PAIRS, 1)), _const_spec((2, N_PAIRS, 1))],
        out_specs=blk(N_PAIRS, LANES),
        out_shape=jax.ShapeDtypeStruct((DEC_BATCH, N_PAIRS, LANES), F32),
        compiler_params=_params(),
        name="swa_sample",
    )(q_s.reshape(DEC_BATCH, N_PAIRS, LANES), kv_s.reshape(DEC_BATCH, 1, 2 * KV_DIM),
      cache_k.reshape(DEC_BATCH, rows, KV_DIM), cache_v.reshape(DEC_BATCH, rows, KV_DIM),
      jnp.asarray(slopes), sink_arr)
    return o3.reshape(DEC_BATCH, Q_DIM)


def _proj_route_kernel(x_ref, op_ref, os_ref, wo_hi, wo_lo, g_ref, wr_hi, wr_lo, xo_ref, idx_ref,
                       gate_ref, rank_ref, cnt_ref, carry_ref, xnew_ref):
    i = pl.program_id(0)
    is_sample = i == N_TILES - 1

    @pl.when(i == 0)
    def _():
        carry_ref[...] = jnp.zeros_like(carry_ref)

    @pl.when(jnp.logical_not(is_sample))
    def _():
        xnew_ref[...] = x_ref[...] + _dot(op_ref[...], wo_hi[...])

    @pl.when(is_sample)
    def _():
        xnew_ref[:DEC_BATCH, :] = x_ref[:DEC_BATCH, :] + _mm(_split(os_ref[...]), wo_hi[...], wo_lo[...])
        xnew_ref[DEC_BATCH:, :] = jnp.zeros((TM - DEC_BATCH, D_MODEL), F32)

    x = xnew_ref[...]
    _store_row_tiles(xo_ref, x)
    h = _rms(x, g_ref[...])
    logits = _mm(_split(h), wr_hi[...], wr_lo[...])
    lane = lax.broadcasted_iota(jnp.int32, (TM, LANES), 1)
    logits = jnp.where(lane < N_EXPERTS, logits, -jnp.inf)
    m0 = jnp.max(logits, axis=-1, keepdims=True)
    i0 = jnp.min(jnp.where(logits == m0, lane, LANES), axis=-1, keepdims=True)
    rest = jnp.where(lane == i0, -jnp.inf, logits)
    m1 = jnp.max(rest, axis=-1, keepdims=True)
    i1 = jnp.min(jnp.where(rest == m1, lane, LANES), axis=-1, keepdims=True)
    e1 = jnp.exp(m1 - m0)
    g0 = 1.0 / (1.0 + e1)
    g1 = e1 * g0
    idx_ref[...] = jnp.concatenate([i0, i1], axis=1)
    gate_ref[...] = jnp.concatenate([g0, g1], axis=1)

    onehot = jnp.where((lane == i0) | (lane == i1), 1.0, 0.0)
    r = lax.broadcasted_iota(jnp.int32, (TM, TM), 0)
    c = lax.broadcasted_iota(jnp.int32, (TM, TM), 1)
    before = jnp.where(c < r, 1.0, 0.0).astype(BF16)
    ranks = _dot(before, onehot.astype(BF16)) + carry_ref[...]
    r0 = jnp.sum(jnp.where(lane == i0, ranks, 0.0), axis=-1, keepdims=True)
    r1 = jnp.sum(jnp.where(lane == i1, ranks, 0.0), axis=-1, keepdims=True)
    rank_ref[...] = jnp.concatenate([r0, r1], axis=1).astype(jnp.int32)
    carry_ref[...] = carry_ref[...] + jnp.sum(onehot, axis=0, keepdims=True)
    cnt_ref[...] = carry_ref[...].astype(jnp.int32)


def _proj_route(x, o_prompt, o_sample, w_o, g, w_router):
    tile = pl.BlockSpec((TM, D_MODEL), lambda i: (i, 0))
    pair = pl.BlockSpec((TM, 2), lambda i: (i, 0))
    wr = jnp.zeros((D_MODEL, LANES), F32).at[:, :N_EXPERTS].set(w_router)
    return pl.pallas_call(
        _proj_route_kernel,
        grid=(N_TILES,),
        in_specs=[tile,
                  pl.BlockSpec((TM, Q_DIM), lambda i: (jnp.minimum(i, PROMPT_TILES - 1), 0)),
                  pl.BlockSpec((DEC_BATCH, Q_DIM), lambda i: (0, 0)),
                  _const_spec((Q_DIM, D_MODEL)), _const_spec((Q_DIM, D_MODEL)),
                  _const_spec((1, D_MODEL)), _const_spec((D_MODEL, LANES)),
                  _const_spec((D_MODEL, LANES))],
        out_specs=[pl.BlockSpec((TM * SUBLANES, LANES), lambda i: (i, 0)), pair, pair, pair,
                   pl.BlockSpec((1, LANES), lambda i: (0, 0))],
        out_shape=[jax.ShapeDtypeStruct((T_PAD * SUBLANES, LANES), F32),
                   jax.ShapeDtypeStruct((T_PAD, 2), jnp.int32),
                   jax.ShapeDtypeStruct((T_PAD, 2), F32),
                   jax.ShapeDtypeStruct((T_PAD, 2), jnp.int32),
                   jax.ShapeDtypeStruct((1, LANES), jnp.int32)],
        scratch_shapes=[pltpu.VMEM((1, LANES), F32), pltpu.VMEM((TM, D_MODEL), F32)],
        compiler_params=_params(),
        name="proj_route",
    )(x, o_prompt, o_sample, *_hi_lo(w_o), g.reshape(1, -1), *_hi_lo(wr))


def _tile_at(start):
    return pl.ds(start if isinstance(start, int) else pl.multiple_of(start, SUBLANES), SUBLANES)


def _gather_row(x_hbm, xbuf, sem, slot, src, r):
    return pltpu.make_async_copy(x_hbm.at[_tile_at(src), :], xbuf.at[slot, _tile_at(r * SUBLANES), :],
                                 sem.at[slot])


def _scatter_row(obuf, y_hbm, sem, slot, r, dst):
    return pltpu.make_async_copy(obuf.at[slot, _tile_at(r * SUBLANES), :], y_hbm.at[_tile_at(dst), :],
                                 sem.at[slot])


def _gather_wait(x_hbm, xbuf, sem, slot):
    pltpu.make_async_copy(x_hbm.at[pl.ds(0, TMM * SUBLANES), :], xbuf.at[slot], sem.at[slot]).wait()


def _scatter_wait(obuf, y_hbm, sem, slot):
    pltpu.make_async_copy(obuf.at[slot], y_hbm.at[pl.ds(0, TMM * SUBLANES), :], sem.at[slot]).wait()


def _slot_src(slot_ref, q):
    return slot_ref[q]


def _slot_dst(slot_ref, q):
    return slot_ref[SLOT_ENTRIES + q]


def _moe_kernel(te_ref, nu_ref, slot_ref, x_hbm, g_ref, wg_ref, wu_ref, wd_ref, y_hbm,
                xbuf, hbuf, acc, obuf, gsem, ssem):
    i = pl.program_id(0)
    f = pl.program_id(1)
    n_used = nu_ref[0]
    slot = i % 2
    other = 1 - slot

    @pl.when(i < n_used)
    def _():
        @pl.when((i == 0) & (f == 0))
        def _():
            def start(r, carry):
                _gather_row(x_hbm, xbuf, gsem, 0, _slot_src(slot_ref, TMM + r), r).start()
                return carry
            lax.fori_loop(0, TMM, start, 0)
            obuf[...] = jnp.zeros_like(obuf)
            for s in range(2):
                spare = pltpu.make_async_copy(
                    obuf.at[s],
                    y_hbm.at[pl.ds((2 * T_PAD + s * TMM) * SUBLANES, TMM * SUBLANES), :],
                    ssem.at[s])
                spare.start()
                spare.wait()

        @pl.when(f == 0)
        def _():
            _gather_wait(x_hbm, xbuf, gsem, slot)
            xin = xbuf.at[slot]
            chunks = [xin[_chunk(s, TMM), :] for s in range(ROW_CHUNKS)]
            ssq = chunks[0] * chunks[0]
            for c in chunks[1:]:
                ssq = ssq + c * c
            scale = lax.rsqrt(jnp.sum(ssq, axis=-1, keepdims=True) * (1.0 / D_MODEL) + EPS)
            for s, c in enumerate(chunks):
                cols = slice(s * LANES, (s + 1) * LANES)
                hbuf[:, cols] = (c * scale * g_ref[:, cols]).astype(BF16)
            acc[...] = jnp.zeros_like(acc)

        h = hbuf[...]
        a = (_silu(_dot(h, wg_ref[...])) * _dot(h, wu_ref[...])).astype(BF16)
        acc[...] += _dot(a, wd_ref[...])

        row0 = f * ROWS_PER_STEP
        nxt = (i + 2) * TMM + row0
        prv = i * TMM + row0
        for r in range(ROWS_PER_STEP):
            _gather_row(x_hbm, xbuf, gsem, other, _slot_src(slot_ref, nxt + r), row0 + r).start()
            _scatter_row(obuf, y_hbm, ssem, other, row0 + r, _slot_dst(slot_ref, prv + r)).start()

        @pl.when(f == NF - 1)
        def _():
            @pl.when(i >= 1)
            def _():
                _scatter_wait(obuf, y_hbm, ssem, slot)
            _store_row_tiles(obuf.at[slot], acc[...])

        @pl.when((f == NF - 1) & (i == n_used - 1))
        def _():
            _gather_wait(x_hbm, xbuf, gsem, other)
            _scatter_wait(obuf, y_hbm, ssem, other)
            base = (i + 1) * TMM

            def start(r, carry):
                _scatter_row(obuf, y_hbm, ssem, slot, r, _slot_dst(slot_ref, base + r)).start()
                return carry
            lax.fori_loop(0, TMM, start, 0)
            _scatter_wait(obuf, y_hbm, ssem, slot)


def _moe_layer(x, g, tile_expert, n_used, slots, layer, w_gate, w_up, w_down):
    def f_eff(i, f, nu):
        return jnp.where(i < nu[0], f, NF - 1)

    grid_spec = pltpu.PrefetchScalarGridSpec(
        num_scalar_prefetch=3,
        grid=(NT_MOE, NF),
        in_specs=[pl.BlockSpec(memory_space=pl.ANY),
                  pl.BlockSpec((1, D_MODEL), lambda i, f, te, nu, tok: (0, 0)),
                  pl.BlockSpec((None, None, D_MODEL, TF),
                               lambda i, f, te, nu, tok: (layer, te[i], 0, f_eff(i, f, nu))),
                  pl.BlockSpec((None, None, D_MODEL, TF),
                               lambda i, f, te, nu, tok: (layer, te[i], 0, f_eff(i, f, nu))),
                  pl.BlockSpec((None, None, TF, D_MODEL),
                               lambda i, f, te, nu, tok: (layer, te[i], f_eff(i, f, nu), 0))],
        out_specs=pl.BlockSpec(memory_space=pl.ANY),
        scratch_shapes=[pltpu.VMEM((2, TMM * SUBLANES, LANES), F32),
                        pltpu.VMEM((TMM, D_MODEL), BF16),
                        pltpu.VMEM((TMM, D_MODEL), F32),
                        pltpu.VMEM((2, TMM * SUBLANES, LANES), F32),
                        pltpu.SemaphoreType.DMA((2,)),
                        pltpu.SemaphoreType.DMA((2,))],
    )
    return pl.pallas_call(
        _moe_kernel,
        grid_spec=grid_spec,
        out_shape=jax.ShapeDtypeStruct((Y_ROWS * SUBLANES, LANES), F32),
        compiler_params=_params(2),
        name="moe_experts",
    )(tile_expert, n_used, slots, x, g.reshape(1, -1), w_gate, w_up, w_down)


def _final_kernel(x_ref, gate_ref, y0_ref, y1_ref, gf_ref, yp_ref, ys_ref):
    i = pl.program_id(0)
    y = _rms(_combined_rows(x_ref, gate_ref, y0_ref, y1_ref), gf_ref[...])

    @pl.when(i < PROMPT_TILES)
    def _():
        yp_ref[...] = y

    @pl.when(i == N_TILES - 1)
    def _():
        ys_ref[...] = y[:DEC_BATCH]


def _final_combine(x, gates, y_slots, g_final):
    return pl.pallas_call(
        _final_kernel,
        grid=(N_TILES,),
        in_specs=_row_tile_specs() + [_const_spec((1, D_MODEL))],
        out_specs=[pl.BlockSpec((TM, D_MODEL), lambda i: (jnp.minimum(i, PROMPT_TILES - 1), 0)),
                   pl.BlockSpec((DEC_BATCH, D_MODEL), lambda i: (0, 0))],
        out_shape=[jax.ShapeDtypeStruct((N_PROMPT, D_MODEL), F32),
                   jax.ShapeDtypeStruct((DEC_BATCH, D_MODEL), F32)],
        compiler_params=_params(),
        name="final_combine",
    )(x, gates, y_slots, y_slots, g_final.reshape(1, -1))


def _padding_slot_entries():
    p = np.arange(-TMM, P_TOTAL + TMM)
    spare_row = ((p // TMM) % 2) * TMM + p % TMM
    return np.concatenate([spare_row, 2 * T_PAD + spare_row]).astype(np.int32) * SUBLANES


FILL_UNROLL = 8


def _slot_fill_kernel(pos_ref, init_hbm, slots_hbm, buf, sem):
    load = pltpu.make_async_copy(init_hbm, buf, sem)
    load.start()
    load.wait()

    def body(j, carry):
        for u in range(FILL_UNROLL):
            t = j * FILL_UNROLL + u
            for k in range(2):
                q = TMM + pos_ref[2 * t + k]
                buf[q] = t * SUBLANES
                buf[SLOT_ENTRIES + q] = (k * T_PAD + t) * SUBLANES
        return carry

    lax.fori_loop(0, T_PAD // FILL_UNROLL, body, 0)
    store = pltpu.make_async_copy(buf, slots_hbm, sem)
    store.start()
    store.wait()


def _slot_fill(pos):
    return pl.pallas_call(
        _slot_fill_kernel,
        in_specs=[pl.BlockSpec(memory_space=pltpu.SMEM), pl.BlockSpec(memory_space=pl.ANY)],
        out_specs=pl.BlockSpec(memory_space=pl.ANY),
        out_shape=jax.ShapeDtypeStruct((2 * SLOT_ENTRIES,), jnp.int32),
        scratch_shapes=[pltpu.SMEM((2 * SLOT_ENTRIES,), jnp.int32), pltpu.SemaphoreType.DMA(())],
        name="slot_fill",
    )(pos.reshape(-1), jnp.asarray(_padding_slot_entries()))


def _slot_plan(idx, rank, counts):
    cnt = counts[0, :N_EXPERTS]
    tiles = (cnt + TMM - 1) // TMM
    tile_end = jnp.cumsum(tiles)
    start = (tile_end - tiles) * TMM
    n_used = tile_end[-1:].astype(jnp.int32)
    pos = (start[idx] + rank).astype(jnp.int32)
    slots = _slot_fill(pos)
    tile_ids = jnp.arange(NT_MOE, dtype=jnp.int32)
    tile_expert = jnp.minimum(jnp.sum(tile_end[None, :] <= tile_ids[:, None], axis=1),
                              N_EXPERTS - 1).astype(jnp.int32)
    last_expert = tile_expert[jnp.maximum(n_used[0] - 1, 0)]
    tile_expert = jnp.where(tile_ids < n_used[0], tile_expert, last_expert)
    return slots, tile_expert, n_used


def kernel(x_prompt, x_sample, cache_swa_k, cache_swa_v, norm_mix, norm_ffn, norm_final,
           sgu_w_in, sgu_b_in, sgu_ln_g, sgu_ln_b, sgu_w_s, sgu_b_s, sgu_w_out,
           attn_w_qkv, attn_sinks, attn_w_o,
           ffn_w_gate, ffn_w_up, ffn_w_down,
           moe_w_router, moe_w_gate, moe_w_up, moe_w_down):
    moe_wg, moe_wu, moe_wd = (w.astype(BF16) for w in (moe_w_gate, moe_w_up, moe_w_down))
    sgu_v_p, sgu_v_s, k_p, v_p, k_s, v_s = [], [], [], [], [], []
    rows_p = min(WINDOW, SEQ)
    rows, source = (x_prompt.reshape(N_PROMPT, D_MODEL), x_sample.reshape(DEC_BATCH, D_MODEL)), "inputs"
    for i in range(DEPTH):
        j = i // 2
        if i % 2 == 0:
            x, vlast = _sgu_layer(rows, source, norm_mix[i], sgu_w_in[j], sgu_b_in[j], sgu_ln_g[j],
                                  sgu_ln_b[j], sgu_w_s[j], sgu_b_s[j], sgu_w_out[j])
            sgu_v_p.append(vlast[:BATCH * CHUNK].reshape(BATCH, CHUNK, SGU_WIDTH))
            sgu_v_s.append(vlast[BATCH * CHUNK:].reshape(DEC_BATCH, 1, SGU_WIDTH))
            x = _ffn_layer(x, norm_ffn[i], ffn_w_gate[j], ffn_w_up[j], ffn_w_down[j])
        else:
            q, q_s, kv = _qkv_layer(x, norm_mix[i], attn_w_qkv[j])
            kv_p = kv[:N_PROMPT].reshape(BATCH, SEQ, 2 * KV_DIM)[:, SEQ - rows_p:]
            k_p.append(kv_p[..., :KV_DIM].reshape(BATCH, rows_p, N_KV_HEADS, HEAD_DIM))
            v_p.append(kv_p[..., KV_DIM:].reshape(BATCH, rows_p, N_KV_HEADS, HEAD_DIM))
            kv_s = kv[N_PROMPT:T_REAL]
            k_new = kv_s[:, None, :KV_DIM].reshape(DEC_BATCH, 1, N_KV_HEADS, HEAD_DIM)
            v_new = kv_s[:, None, KV_DIM:].reshape(DEC_BATCH, 1, N_KV_HEADS, HEAD_DIM)
            k_s.append(jnp.concatenate([cache_swa_k[j][:, 1:], k_new], axis=1))
            v_s.append(jnp.concatenate([cache_swa_v[j][:, 1:], v_new], axis=1))

            o = _swa_prompt(q, kv, attn_sinks[j].astype(F32))
            o_s = _swa_sample(q_s, kv_s, cache_swa_k[j], cache_swa_v[j], attn_sinks[j])
            x, idx, gates, rank, counts = _proj_route(x, o, o_s, attn_w_o[j], norm_ffn[i],
                                                      moe_w_router[j])
            slots, tile_expert, n_used = _slot_plan(idx, rank, counts)
            y_slots = _moe_layer(x, norm_ffn[i], tile_expert, n_used, slots, j,
                                 moe_wg, moe_wu, moe_wd)
            rows, source = (x, gates, y_slots), "experts"
    y_prompt, y_sample = _final_combine(*rows, norm_final)
    y_prompt = y_prompt.reshape(BATCH, SEQ, D_MODEL)
    y_sample = y_sample.reshape(DEC_BATCH, 1, D_MODEL)
    return (y_prompt, y_sample, jnp.stack(sgu_v_p), jnp.stack(sgu_v_s),
            jnp.stack(k_p), jnp.stack(v_p), jnp.stack(k_s), jnp.stack(v_s))
```

```python
import functools

import numpy as np
import jax
import jax.numpy as jnp
from jax import lax
from jax.experimental import pallas as pl
from jax.experimental.pallas import tpu as pltpu

D_MODEL = 1024
BATCH = 4
SEQ = 4096
DEPTH = 4
DEC_BATCH = 128
PAST_LEN = 8192
CHUNK = 128
SGU_WIDTH = 2 * D_MODEL
SGU_GROUPS = 8
SGU_GROUP_DIM = SGU_WIDTH // SGU_GROUPS
WINDOW = 128
BLOCK = 128
HEAD_DIM = 64
N_HEADS = D_MODEL // HEAD_DIM
N_KV_HEADS = 2
GQA_GROUP = N_HEADS // N_KV_HEADS
Q_DIM = N_HEADS * HEAD_DIM
KV_DIM = N_KV_HEADS * HEAD_DIM
D_FF = 2816
N_EXPERTS = 8
D_FF_EXPERT = 3584
EPS = 1e-6
LN_EPS = 1e-5

F32 = jnp.float32
BF16 = jnp.bfloat16

LANES = 128
TM = 512
N_PROMPT = BATCH * SEQ
T_REAL = N_PROMPT + DEC_BATCH
N_TILES = -(-T_REAL // TM)
T_PAD = N_TILES * TM
PROMPT_TILES = N_PROMPT // TM
TILES_PER_SEQ = SEQ // TM
CHUNKS_PER_TILE = TM // CHUNK
N_PAIRS = N_HEADS // 2

TMM = 512
TF = 1792
NF = D_FF_EXPERT // TF
ROWS_PER_STEP = TMM // NF
N_SLOTS = 2 * T_PAD
NT_MOE = (N_SLOTS + N_EXPERTS * (TMM - 1)) // TMM + 1
P_TOTAL = NT_MOE * TMM
Y_ROWS = 2 * T_PAD + 2 * TMM
SLOT_ENTRIES = P_TOTAL + 2 * TMM

VMEM_LIMIT = 56 * 1024 * 1024

_SLOPES = [2.0 ** (-8.0 * (h + 1) / N_HEADS) for h in range(N_HEADS)]


def _rms(x, g):
    return x * lax.rsqrt(jnp.mean(x * x, axis=-1, keepdims=True) + EPS) * g


def _gelu(x):
    k = -2.0 * np.sqrt(2.0 / np.pi) * np.log2(np.e)
    t = (x * x) * np.float32(0.044715 * k) + np.float32(k)
    return x * (1.0 / (1.0 + jnp.exp2(x * t)))


def _silu(x):
    return x * (1.0 / (1.0 + jnp.exp(-x)))


def _dot(a, b):
    return jnp.dot(a, b, preferred_element_type=F32)


def _split(x):
    hi = x.astype(BF16)
    return hi, (x - hi.astype(F32)).astype(BF16)


def _operand(x, precise):
    return _split(x) if precise else x.astype(BF16)


def _mm(x, w_hi, w_lo):
    if not isinstance(x, tuple):
        return _dot(x, w_hi)
    x_hi, x_lo = x
    return _dot(x_hi, w_hi) + (_dot(x_lo, w_hi) + _dot(x_hi, w_lo))


def _hi_lo(w):
    hi = lax.optimization_barrier(w.astype(BF16))
    return hi, (w - hi.astype(F32)).astype(BF16)


def _dot_t(a, b):
    return lax.dot_general(a, b, (((1,), (1,)), ((), ())), preferred_element_type=F32)


SUBLANES = 8
ROW_CHUNKS = D_MODEL // LANES


def _chunk(s, rows):
    return pl.ds(s, rows, stride=SUBLANES)


def _load_row_tiles(ref, rows):
    return jnp.concatenate([ref[_chunk(s, rows), :] for s in range(ROW_CHUNKS)], axis=1)


def _store_row_tiles(ref, x):
    rows = x.shape[0]
    for s in range(ROW_CHUNKS):
        ref[_chunk(s, rows), :] = x[:, s * LANES:(s + 1) * LANES]


def _const_spec(shape):
    nd = len(shape)
    return pl.BlockSpec(shape, lambda *_: (0,) * nd, pipeline_mode=pl.Buffered(1))


def _params(n_axes=1):
    return pltpu.CompilerParams(dimension_semantics=("arbitrary",) * n_axes,
                                vmem_limit_bytes=VMEM_LIMIT)


def _combined_rows(x_ref, gate_ref, y0_ref, y1_ref, rows=TM):
    gate = gate_ref[:rows, :]
    g0, g1 = gate[:, 0:1], gate[:, 1:2]
    return jnp.concatenate(
        [x_ref[_chunk(s, rows), :] + (g0 * y0_ref[_chunk(s, rows), :] + g1 * y1_ref[_chunk(s, rows), :])
         for s in range(ROW_CHUNKS)], axis=1)


def _sgu_kernel(*refs, source):
    i = pl.program_id(0)
    is_sample = i == N_TILES - 1
    n_src = 2 if source == "inputs" else 4
    src, refs = refs[:n_src], refs[n_src:]
    (g_ref, win_hi, win_lo, bin_ref, lng_ref, lnb_ref, ws_ref, bs_ref, wout_hi, wout_lo,
     xo_ref, vlast_ref, ug_ref) = refs

    def mix(x, sample):
        h = _operand(_rms(x, g_ref[...]), sample)
        v = _gelu(_mm(h, win_hi[:, SGU_WIDTH:], win_lo[:, SGU_WIDTH:]) + bin_ref[:, SGU_WIDTH:])
        mu = jnp.mean(v, axis=-1, keepdims=True)
        vc = v - mu
        var = jnp.mean(vc * vc, axis=-1, keepdims=True)
        vn = vc * lax.rsqrt(var + LN_EPS) * lng_ref[...] + lnb_ref[...]
        vb = vn.astype(BF16)
        row = lax.broadcasted_iota(jnp.int32, (CHUNK, CHUNK), 0)
        col = lax.broadcasted_iota(jnp.int32, (CHUNK, CHUNK), 1)
        y = jnp.zeros_like(x)
        for g in range(SGU_GROUPS):
            lo, hi = g * SGU_GROUP_DIM, (g + 1) * SGU_GROUP_DIM
            u = _gelu(_mm(h, win_hi[:, lo:hi], win_lo[:, lo:hi]) + bin_ref[:, lo:hi])
            if sample:
                gate = vn[:, lo:hi] * ws_ref[g][0:1, 0:1] + bs_ref[g][0:1, 0:1]
                y = y + _mm(_split(u * gate), wout_hi[lo:hi, :], wout_lo[lo:hi, :])
            else:
                w_tril = jnp.where(row >= col, ws_ref[g], 0.0).astype(BF16)
                gate = jnp.concatenate(
                    [_dot(w_tril, vb[c * CHUNK:(c + 1) * CHUNK, lo:hi]) + bs_ref[g]
                     for c in range(x.shape[0] // CHUNK)], axis=0)
                ug_ref[:, lo:hi] = (u * gate).astype(BF16)
        if not sample:
            y = _dot(ug_ref[...], wout_hi[...])
        return x + y, vn

    @pl.when(jnp.logical_not(is_sample))
    def _():
        x = src[0][...] if source == "inputs" else _combined_rows(*src)
        x_new, vn = mix(x, sample=False)
        xo_ref[...] = x_new
        vlast_ref[...] = vn[TM - CHUNK:]

    @pl.when(is_sample)
    def _():
        x = src[1][...] if source == "inputs" else _combined_rows(*src, rows=DEC_BATCH)
        x_new, vn = mix(x, sample=True)
        xo_ref[:DEC_BATCH, :] = x_new
        xo_ref[DEC_BATCH:, :] = jnp.zeros((TM - DEC_BATCH, D_MODEL), F32)
        vlast_ref[...] = vn


def _row_tile_specs():
    return [pl.BlockSpec((TM * SUBLANES, LANES), lambda i: (i, 0)),
            pl.BlockSpec((TM, 2), lambda i: (i, 0)),
            pl.BlockSpec((TM * SUBLANES, LANES), lambda i: (i, 0)),
            pl.BlockSpec((TM * SUBLANES, LANES), lambda i: (N_TILES + i, 0))]


def _sgu_layer(rows, source, g, w_in, b_in, ln_g, ln_b, w_s, b_s, w_out):
    tile = pl.BlockSpec((TM, D_MODEL), lambda i: (i, 0))
    vlast_spec = pl.BlockSpec(
        (CHUNK, SGU_WIDTH),
        lambda i: (jnp.where(i == N_TILES - 1, BATCH, i // TILES_PER_SEQ), 0))
    if source == "inputs":
        row_specs = [pl.BlockSpec((TM, D_MODEL), lambda i: (jnp.minimum(i, PROMPT_TILES - 1), 0)),
                     pl.BlockSpec((DEC_BATCH, D_MODEL), lambda i: (0, 0))]
        row_args = rows
    else:
        row_specs = _row_tile_specs()
        x, gates, y_slots = rows
        row_args = (x, gates, y_slots, y_slots)
    return pl.pallas_call(
        functools.partial(_sgu_kernel, source=source),
        grid=(N_TILES,),
        in_specs=row_specs + [
                  _const_spec((1, D_MODEL)),
                  _const_spec((D_MODEL, 2 * SGU_WIDTH)),
                  _const_spec((D_MODEL, 2 * SGU_WIDTH)),
                  _const_spec((1, 2 * SGU_WIDTH)),
                  _const_spec((1, SGU_WIDTH)),
                  _const_spec((1, SGU_WIDTH)),
                  _const_spec((SGU_GROUPS, CHUNK, CHUNK)),
                  _const_spec((SGU_GROUPS, CHUNK, 1)),
                  _const_spec((SGU_WIDTH, D_MODEL)),
                  _const_spec((SGU_WIDTH, D_MODEL))],
        out_specs=[tile, vlast_spec],
        out_shape=[jax.ShapeDtypeStruct((T_PAD, D_MODEL), F32),
                   jax.ShapeDtypeStruct(((BATCH + 1) * CHUNK, SGU_WIDTH), F32)],
        scratch_shapes=[pltpu.VMEM((TM, SGU_WIDTH), BF16)],
        compiler_params=_params(),
        name="sgu_mixer",
    )(*row_args, g.reshape(1, -1), *_hi_lo(w_in), b_in.reshape(1, -1), ln_g.reshape(1, -1),
      ln_b.reshape(1, -1), w_s, b_s.reshape(SGU_GROUPS, CHUNK, 1), *_hi_lo(w_out))


FF_SPLIT = 1


def _ffn_kernel(x_ref, g_ref, wg_hi, wg_lo, wu_hi, wu_lo, wd_hi, wd_lo, xo_ref):
    is_sample = pl.program_id(0) == N_TILES - 1
    width = D_FF // FF_SPLIT

    def ffn(x, sample):
        h = _operand(_rms(x, g_ref[...]), sample)
        y = x
        for c in range(FF_SPLIT):
            cols = slice(c * width, (c + 1) * width)
            a = _silu(_mm(h, wg_hi[:, cols], wg_lo[:, cols])) * _mm(h, wu_hi[:, cols], wu_lo[:, cols])
            y = y + _mm(_operand(a, sample), wd_hi[cols, :], wd_lo[cols, :])
        return y

    @pl.when(jnp.logical_not(is_sample))
    def _():
        xo_ref[...] = ffn(x_ref[...], sample=False)

    @pl.when(is_sample)
    def _():
        xo_ref[:DEC_BATCH, :] = ffn(x_ref[:DEC_BATCH, :], sample=True)
        xo_ref[DEC_BATCH:, :] = jnp.zeros((TM - DEC_BATCH, D_MODEL), F32)


def _ffn_layer(x, g, w_gate, w_up, w_down):
    tile = pl.BlockSpec((TM, D_MODEL), lambda i: (i, 0))
    return pl.pallas_call(
        _ffn_kernel,
        grid=(N_TILES,),
        in_specs=[tile, _const_spec((1, D_MODEL))]
        + [_const_spec((D_MODEL, D_FF))] * 4 + [_const_spec((D_FF, D_MODEL))] * 2,
        out_specs=tile,
        out_shape=jax.ShapeDtypeStruct((T_PAD, D_MODEL), F32),
        compiler_params=_params(),
        name="dense_swiglu",
    )(x, g.reshape(1, -1), *_hi_lo(w_gate), *_hi_lo(w_up), *_hi_lo(w_down))


def _qkv_kernel(x_ref, g_ref, w_hi, w_lo, q_ref, qs_ref, kv_ref):
    is_sample = pl.program_id(0) == N_TILES - 1

    def qkv(x, sample):
        out = _mm(_operand(_rms(x, g_ref[...]), sample), w_hi[...], w_lo[...])
        return out[:, :Q_DIM] * (HEAD_DIM ** -0.5), out[:, Q_DIM:]

    @pl.when(jnp.logical_not(is_sample))
    def _():
        q, kv = qkv(x_ref[...], sample=False)
        q_ref[...] = q.astype(BF16)
        kv_ref[...] = kv

    @pl.when(is_sample)
    def _():
        q, kv = qkv(x_ref[:DEC_BATCH, :], sample=True)
        qs_ref[...] = q
        kv_ref[:DEC_BATCH, :] = kv
        kv_ref[DEC_BATCH:, :] = jnp.zeros((TM - DEC_BATCH, 2 * KV_DIM), F32)


def _qkv_layer(x, g, w_qkv):
    tile = pl.BlockSpec((TM, D_MODEL), lambda i: (i, 0))
    return pl.pallas_call(
        _qkv_kernel,
        grid=(N_TILES,),
        in_specs=[tile, _const_spec((1, D_MODEL)),
                  _const_spec((D_MODEL, Q_DIM + 2 * KV_DIM)), _const_spec((D_MODEL, Q_DIM + 2 * KV_DIM))],
        out_specs=[pl.BlockSpec((TM, Q_DIM), lambda i: (jnp.minimum(i, PROMPT_TILES - 1), 0)),
                   pl.BlockSpec((DEC_BATCH, Q_DIM), lambda i: (0, 0)),
                   pl.BlockSpec((TM, 2 * KV_DIM), lambda i: (i, 0))],
        out_shape=[jax.ShapeDtypeStruct((N_PROMPT, Q_DIM), BF16),
                   jax.ShapeDtypeStruct((DEC_BATCH, Q_DIM), F32),
                   jax.ShapeDtypeStruct((T_PAD, 2 * KV_DIM), F32)],
        compiler_params=_params(),
        name="swa_qkv",
    )(x, g.reshape(1, -1), *_hi_lo(w_qkv))


def _half_masks_f32(x):
    lane = lax.broadcasted_iota(jnp.int32, x.shape, 1)
    low = lane < HEAD_DIM
    xr = pltpu.roll(x, HEAD_DIM, 1)
    return ((jnp.where(low, x, 0.0), jnp.where(low, 0.0, xr)),
            (jnp.where(low, xr, 0.0), jnp.where(low, 0.0, x)))


def _half_masks(x):
    return tuple(tuple(m.astype(BF16) for m in pair) for pair in _half_masks_f32(x))


def _swa_prompt_kernel(sink_ref, q_ref, kv_ref, kvp_ref, o_ref, bias_ref):
    i = pl.program_id(0)
    t = lax.broadcasted_iota(jnp.int32, (BLOCK, BLOCK), 0)
    c = lax.broadcasted_iota(jnp.int32, (BLOCK, BLOCK), 1)
    own = c <= t
    diag = c == t

    @pl.when(i == 0)
    def _():
        dist = jnp.where(own, t - c, BLOCK + t - c).astype(F32)
        for hd in range(N_HEADS):
            bias_ref[hd] = _SLOPES[hd] * dist

    first_tile = (i % TILES_PER_SEQ) == 0
    kv_all = jnp.concatenate([kvp_ref[...], kv_ref[...]], axis=0)
    k_blk, v_blk, vf_blk = [], [], []
    for b in range(CHUNKS_PER_TILE + 1):
        blk = kv_all[b * BLOCK:(b + 1) * BLOCK]
        k_blk.append(_half_masks(blk[:, :KV_DIM]))
        v_blk.append(_half_masks(blk[:, KV_DIM:]))
        vf_blk.append(_half_masks_f32(blk[:, KV_DIM:]))
    has_prev = (jnp.zeros((BLOCK, BLOCK), jnp.int32) + jnp.where(first_tile, 0, 1)) == 1

    for b in range(CHUNKS_PER_TILE):
        rows = slice(b * BLOCK, (b + 1) * BLOCK)
        for p in range(N_PAIRS):
            kvh = (2 * p) // GQA_GROUP
            qp = q_ref[rows, p * LANES:(p + 1) * LANES]
            acc = None
            for par in range(2):
                hd = 2 * p + par
                sink = sink_ref[hd]
                keys = jnp.concatenate([k_blk[b][kvh][par], k_blk[b + 1][kvh][par]], axis=0)
                both = _dot_t(qp, keys)
                l_prev, l_own = both[:, :BLOCK], both[:, BLOCK:]
                if b == 0:
                    l_prev = jnp.where(has_prev, l_prev, -jnp.inf)
                logits = jnp.where(own, l_own, l_prev) - bias_ref[hd]
                extra = (jnp.sum(jnp.where(diag, l_prev, 0.0), axis=-1, keepdims=True)
                         - _SLOPES[hd] * BLOCK)
                m = jnp.maximum(jnp.maximum(jnp.max(logits, axis=-1, keepdims=True), extra), sink)
                e = jnp.exp(logits - m)
                e_extra = jnp.exp(extra - m)
                denom = jnp.sum(e, axis=-1, keepdims=True) + e_extra + jnp.exp(sink - m)
                probs = jnp.concatenate([jnp.where(own, 0.0, e).astype(BF16),
                                         jnp.where(own, e, 0.0).astype(BF16)], axis=1)
                vals = jnp.concatenate([v_blk[b][kvh][par], v_blk[b + 1][kvh][par]], axis=0)
                part = (_dot(probs, vals) + e_extra * vf_blk[b][kvh][par]) * (1.0 / denom)
                acc = part if acc is None else acc + part
            o_ref[rows, p * LANES:(p + 1) * LANES] = acc.astype(BF16)


def _swa_prompt(q, kv, sinks):
    return pl.pallas_call(
        _swa_prompt_kernel,
        grid=(PROMPT_TILES,),
        in_specs=[pl.BlockSpec(memory_space=pltpu.SMEM),
                  pl.BlockSpec((TM, Q_DIM), lambda i: (i, 0)),
                  pl.BlockSpec((TM, 2 * KV_DIM), lambda i: (i, 0)),
                  pl.BlockSpec((BLOCK, 2 * KV_DIM),
                               lambda i: (jnp.maximum(i * CHUNKS_PER_TILE - 1, 0), 0))],
        out_specs=pl.BlockSpec((TM, Q_DIM), lambda i: (i, 0)),
        out_shape=jax.ShapeDtypeStruct((N_PROMPT, Q_DIM), BF16),
        scratch_shapes=[pltpu.VMEM((N_HEADS, BLOCK, BLOCK), F32)],
        compiler_params=_params(),
        name="swa_prompt",
    )(sinks, q, kv, kv)


SAMPLE_TILE = 32


def _swa_sample_kernel(q_ref, kvn_ref, ck_ref, cv_ref, slope_ref, sink_ref, o_ref):
    shape = (SAMPLE_TILE, N_PAIRS, LANES)
    lane = lax.broadcasted_iota(jnp.int32, shape, 2)
    pair = lax.broadcasted_iota(jnp.int32, shape, 1)
    low = lane < HEAD_DIM
    kv0 = pair < (N_PAIRS // 2)

    def swap(x):
        return pltpu.roll(x, HEAD_DIM, 2)

    def three_pass(dims, a, b):
        dot = lambda u, v: lax.dot_general(u, v, dims, preferred_element_type=F32)
        return dot(a[0], b[0]) + (dot(a[1], b[0]) + dot(a[0], b[1]))

    qk_dims = (((2,), (2,)), ((0,), (0,)))
    pv_dims = (((2,), (1,)), ((0,), (0,)))

    q = q_ref[...]
    q_even = jnp.where(low, q, 0.0)
    q_odd = jnp.where(low, 0.0, q)
    q_al = (jnp.where(kv0, q_even, swap(q_even)), jnp.where(kv0, swap(q_odd), q_odd))

    ck = _split(ck_ref[...])
    cv = _split(cv_ref[...])
    kvn = kvn_ref[...]
    k_new = kvn[:, :, :KV_DIM]
    v_new = kvn[:, :, KV_DIM:]
    r = lax.broadcasted_iota(jnp.int32, (SAMPLE_TILE, N_PAIRS, WINDOW), 2)
    dist = (WINDOW - r).astype(F32)

    outs = []
    for par in range(2):
        qa = q_al[par]
        slope = slope_ref[par]
        sink = sink_ref[par]
        logits = three_pass(qk_dims, _split(qa), ck) - slope * dist
        l_self = jnp.sum(qa * k_new, axis=-1, keepdims=True)
        m = jnp.maximum(jnp.maximum(jnp.max(logits, axis=-1, keepdims=True), l_self), sink)
        e = jnp.exp(logits - m)
        e_self = jnp.exp(l_self - m)
        inv = 1.0 / (jnp.sum(e, axis=-1, keepdims=True) + e_self + jnp.exp(sink - m))
        o = three_pass(pv_dims, _split(e * inv), cv) + (e_self * inv) * v_new
        outs.append(o)
    o_even = jnp.where(kv0, outs[0], swap(outs[0]))
    o_odd = jnp.where(kv0, swap(outs[1]), outs[1])
    o_ref[...] = jnp.where(low, o_even, o_odd)


def _swa_sample(q_s, kv_s, cache_k, cache_v, sinks):
    rows = cache_k.shape[1]
    slopes = np.asarray(_SLOPES, np.float32).reshape(N_PAIRS, 2).T.reshape(2, N_PAIRS, 1)
    sink_arr = sinks.astype(F32).reshape(N_PAIRS, 2).T.reshape(2, N_PAIRS, 1)
    blk = lambda *shape: pl.BlockSpec((SAMPLE_TILE,) + shape, lambda i: (i,) + (0,) * len(shape))
    o3 = pl.pallas_call(
        _swa_sample_kernel,
        grid=(DEC_BATCH // SAMPLE_TILE,),
        in_specs=[blk(N_PAIRS, LANES), blk(1, 2 * KV_DIM), blk(rows, KV_DIM), blk(rows, KV_DIM),
                  _const_spec((2, N_PAIRS, 1)), _const_spec((2, N_PAIRS, 1))],
        out_specs=blk(N_PAIRS, LANES),
        out_shape=jax.ShapeDtypeStruct((DEC_BATCH, N_PAIRS, LANES), F32),
        compiler_params=_params(),
        name="swa_sample",
    )(q_s.reshape(DEC_BATCH, N_PAIRS, LANES), kv_s.reshape(DEC_BATCH, 1, 2 * KV_DIM),
      cache_k.reshape(DEC_BATCH, rows, KV_DIM), cache_v.reshape(DEC_BATCH, rows, KV_DIM),
      jnp.asarray(slopes), sink_arr)
    return o3.reshape(DEC_BATCH, Q_DIM)


def _proj_route_kernel(x_ref, op_ref, os_ref, wo_hi, wo_lo, g_ref, wr_hi, wr_lo, xo_ref, idx_ref,
                       gate_ref, rank_ref, cnt_ref, carry_ref, xnew_ref):
    i = pl.program_id(0)
    is_sample = i == N_TILES - 1

    @pl.when(i == 0)
    def _():
        carry_ref[...] = jnp.zeros_like(carry_ref)

    @pl.when(jnp.logical_not(is_sample))
    def _():
        xnew_ref[...] = x_ref[...] + _dot(op_ref[...], wo_hi[...])

    @pl.when(is_sample)
    def _():
        xnew_ref[:DEC_BATCH, :] = x_ref[:DEC_BATCH, :] + _mm(_split(os_ref[...]), wo_hi[...], wo_lo[...])
        xnew_ref[DEC_BATCH:, :] = jnp.zeros((TM - DEC_BATCH, D_MODEL), F32)

    x = xnew_ref[...]
    _store_row_tiles(xo_ref, x)
    h = _rms(x, g_ref[...])
    logits = _mm(_split(h), wr_hi[...], wr_lo[...])
    lane = lax.broadcasted_iota(jnp.int32, (TM, LANES), 1)
    logits = jnp.where(lane < N_EXPERTS, logits, -jnp.inf)
    m0 = jnp.max(logits, axis=-1, keepdims=True)
    i0 = jnp.min(jnp.where(logits == m0, lane, LANES), axis=-1, keepdims=True)
    rest = jnp.where(lane == i0, -jnp.inf, logits)
    m1 = jnp.max(rest, axis=-1, keepdims=True)
    i1 = jnp.min(jnp.where(rest == m1, lane, LANES), axis=-1, keepdims=True)
    e1 = jnp.exp(m1 - m0)
    g0 = 1.0 / (1.0 + e1)
    g1 = e1 * g0
    idx_ref[...] = jnp.concatenate([i0, i1], axis=1)
    gate_ref[...] = jnp.concatenate([g0, g1], axis=1)

    onehot = jnp.where((lane == i0) | (lane == i1), 1.0, 0.0)
    r = lax.broadcasted_iota(jnp.int32, (TM, TM), 0)
    c = lax.broadcasted_iota(jnp.int32, (TM, TM), 1)
    before = jnp.where(c < r, 1.0, 0.0).astype(BF16)
    ranks = _dot(before, onehot.astype(BF16)) + carry_ref[...]
    r0 = jnp.sum(jnp.where(lane == i0, ranks, 0.0), axis=-1, keepdims=True)
    r1 = jnp.sum(jnp.where(lane == i1, ranks, 0.0), axis=-1, keepdims=True)
    rank_ref[...] = jnp.concatenate([r0, r1], axis=1).astype(jnp.int32)
    carry_ref[...] = carry_ref[...] + jnp.sum(onehot, axis=0, keepdims=True)
    cnt_ref[...] = carry_ref[...].astype(jnp.int32)


def _proj_route(x, o_prompt, o_sample, w_o, g, w_router):
    tile = pl.BlockSpec((TM, D_MODEL), lambda i: (i, 0))
    pair = pl.BlockSpec((TM, 2), lambda i: (i, 0))
    wr = jnp.zeros((D_MODEL, LANES), F32).at[:, :N_EXPERTS].set(w_router)
    return pl.pallas_call(
        _proj_route_kernel,
        grid=(N_TILES,),
        in_specs=[tile,
                  pl.BlockSpec((TM, Q_DIM), lambda i: (jnp.minimum(i, PROMPT_TILES - 1), 0)),
                  pl.BlockSpec((DEC_BATCH, Q_DIM), lambda i: (0, 0)),
                  _const_spec((Q_DIM, D_MODEL)), _const_spec((Q_DIM, D_MODEL)),
                  _const_spec((1, D_MODEL)), _const_spec((D_MODEL, LANES)),
                  _const_spec((D_MODEL, LANES))],
        out_specs=[pl.BlockSpec((TM * SUBLANES, LANES), lambda i: (i, 0)), pair, pair, pair,
                   pl.BlockSpec((1, LANES), lambda i: (0, 0))],
        out_shape=[jax.ShapeDtypeStruct((T_PAD * SUBLANES, LANES), F32),
                   jax.ShapeDtypeStruct((T_PAD, 2), jnp.int32),
                   jax.ShapeDtypeStruct((T_PAD, 2), F32),
                   jax.ShapeDtypeStruct((T_PAD, 2), jnp.int32),
                   jax.ShapeDtypeStruct((1, LANES), jnp.int32)],
        scratch_shapes=[pltpu.VMEM((1, LANES), F32), pltpu.VMEM((TM, D_MODEL), F32)],
        compiler_params=_params(),
        name="proj_route",
    )(x, o_prompt, o_sample, *_hi_lo(w_o), g.reshape(1, -1), *_hi_lo(wr))


def _tile_at(start):
    return pl.ds(start if isinstance(start, int) else pl.multiple_of(start, SUBLANES), SUBLANES)


def _gather_row(x_hbm, xbuf, sem, slot, src, r):
    return pltpu.make_async_copy(x_hbm.at[_tile_at(src), :], xbuf.at[slot, _tile_at(r * SUBLANES), :],
                                 sem.at[slot])


def _scatter_row(obuf, y_hbm, sem, slot, r, dst):
    return pltpu.make_async_copy(obuf.at[slot, _tile_at(r * SUBLANES), :], y_hbm.at[_tile_at(dst), :],
                                 sem.at[slot])


def _gather_wait(x_hbm, xbuf, sem, slot):
    pltpu.make_async_copy(x_hbm.at[pl.ds(0, TMM * SUBLANES), :], xbuf.at[slot], sem.at[slot]).wait()


def _scatter_wait(obuf, y_hbm, sem, slot):
    pltpu.make_async_copy(obuf.at[slot], y_hbm.at[pl.ds(0, TMM * SUBLANES), :], sem.at[slot]).wait()


def _slot_src(slot_ref, q):
    return slot_ref[q]


def _slot_dst(slot_ref, q):
    return slot_ref[SLOT_ENTRIES + q]


def _moe_kernel(te_ref, nu_ref, slot_ref, x_hbm, g_ref, wg_ref, wu_ref, wd_ref, y_hbm,
                xbuf, hbuf, acc, obuf, gsem, ssem):
    i = pl.program_id(0)
    f = pl.program_id(1)
    n_used = nu_ref[0]
    slot = i % 2
    other = 1 - slot

    @pl.when(i < n_used)
    def _():
        @pl.when((i == 0) & (f == 0))
        def _():
            def start(r, carry):
                _gather_row(x_hbm, xbuf, gsem, 0, _slot_src(slot_ref, TMM + r), r).start()
                return carry
            lax.fori_loop(0, TMM, start, 0)
            obuf[...] = jnp.zeros_like(obuf)
            for s in range(2):
                spare = pltpu.make_async_copy(
                    obuf.at[s],
                    y_hbm.at[pl.ds((2 * T_PAD + s * TMM) * SUBLANES, TMM * SUBLANES), :],
                    ssem.at[s])
                spare.start()
                spare.wait()

        @pl.when(f == 0)
        def _():
            _gather_wait(x_hbm, xbuf, gsem, slot)
            xin = xbuf.at[slot]
            chunks = [xin[_chunk(s, TMM), :] for s in range(ROW_CHUNKS)]
            ssq = chunks[0] * chunks[0]
            for c in chunks[1:]:
                ssq = ssq + c * c
            scale = lax.rsqrt(jnp.sum(ssq, axis=-1, keepdims=True) * (1.0 / D_MODEL) + EPS)
            for s, c in enumerate(chunks):
                cols = slice(s * LANES, (s + 1) * LANES)
                hbuf[:, cols] = (c * scale * g_ref[:, cols]).astype(BF16)
            acc[...] = jnp.zeros_like(acc)

        h = hbuf[...]
        a = (_silu(_dot(h, wg_ref[...])) * _dot(h, wu_ref[...])).astype(BF16)
        acc[...] += _dot(a, wd_ref[...])

        row0 = f * ROWS_PER_STEP
        nxt = (i + 2) * TMM + row0
        prv = i * TMM + row0
        for r in range(ROWS_PER_STEP):
            _gather_row(x_hbm, xbuf, gsem, other, _slot_src(slot_ref, nxt + r), row0 + r).start()
            _scatter_row(obuf, y_hbm, ssem, other, row0 + r, _slot_dst(slot_ref, prv + r)).start()

        @pl.when(f == NF - 1)
        def _():
            @pl.when(i >= 1)
            def _():
                _scatter_wait(obuf, y_hbm, ssem, slot)
            _store_row_tiles(obuf.at[slot], acc[...])

        @pl.when((f == NF - 1) & (i == n_used - 1))
        def _():
            _gather_wait(x_hbm, xbuf, gsem, other)
            _scatter_wait(obuf, y_hbm, ssem, other)
            base = (i + 1) * TMM

            def start(r, carry):
                _scatter_row(obuf, y_hbm, ssem, slot, r, _slot_dst(slot_ref, base + r)).start()
                return carry
            lax.fori_loop(0, TMM, start, 0)
            _scatter_wait(obuf, y_hbm, ssem, slot)


def _moe_layer(x, g, tile_expert, n_used, slots, layer, w_gate, w_up, w_down):
    def f_eff(i, f, nu):
        return jnp.where(i < nu[0], f, NF - 1)

    grid_spec = pltpu.PrefetchScalarGridSpec(
        num_scalar_prefetch=3,
        grid=(NT_MOE, NF),
        in_specs=[pl.BlockSpec(memory_space=pl.ANY),
                  pl.BlockSpec((1, D_MODEL), lambda i, f, te, nu, tok: (0, 0)),
                  pl.BlockSpec((None, None, D_MODEL, TF),
                               lambda i, f, te, nu, tok: (layer, te[i], 0, f_eff(i, f, nu))),
                  pl.BlockSpec((None, None, D_MODEL, TF),
                               lambda i, f, te, nu, tok: (layer, te[i], 0, f_eff(i, f, nu))),
                  pl.BlockSpec((None, None, TF, D_MODEL),
                               lambda i, f, te, nu, tok: (layer, te[i], f_eff(i, f, nu), 0))],
        out_specs=pl.BlockSpec(memory_space=pl.ANY),
        scratch_shapes=[pltpu.VMEM((2, TMM * SUBLANES, LANES), F32),
                        pltpu.VMEM((TMM, D_MODEL), BF16),
                        pltpu.VMEM((TMM, D_MODEL), F32),
                        pltpu.VMEM((2, TMM * SUBLANES, LANES), F32),
                        pltpu.SemaphoreType.DMA((2,)),
                        pltpu.SemaphoreType.DMA((2,))],
    )
    return pl.pallas_call(
        _moe_kernel,
        grid_spec=grid_spec,
        out_shape=jax.ShapeDtypeStruct((Y_ROWS * SUBLANES, LANES), F32),
        compiler_params=_params(2),
        name="moe_experts",
    )(tile_expert, n_used, slots, x, g.reshape(1, -1), w_gate, w_up, w_down)


def _final_kernel(x_ref, gate_ref, y0_ref, y1_ref, gf_ref, yp_ref, ys_ref):
    i = pl.program_id(0)
    y = _rms(_combined_rows(x_ref, gate_ref, y0_ref, y1_ref), gf_ref[...])

    @pl.when(i < PROMPT_TILES)
    def _():
        yp_ref[...] = y

    @pl.when(i == N_TILES - 1)
    def _():
        ys_ref[...] = y[:DEC_BATCH]


def _final_combine(x, gates, y_slots, g_final):
    return pl.pallas_call(
        _final_kernel,
        grid=(N_TILES,),
        in_specs=_row_tile_specs() + [_const_spec((1, D_MODEL))],
        out_specs=[pl.BlockSpec((TM, D_MODEL), lambda i: (jnp.minimum(i, PROMPT_TILES - 1), 0)),
                   pl.BlockSpec((DEC_BATCH, D_MODEL), lambda i: (0, 0))],
        out_shape=[jax.ShapeDtypeStruct((N_PROMPT, D_MODEL), F32),
                   jax.ShapeDtypeStruct((DEC_BATCH, D_MODEL), F32)],
        compiler_params=_params(),
        name="final_combine",
    )(x, gates, y_slots, y_slots, g_final.reshape(1, -1))


def _padding_slot_entries():
    p = np.arange(-TMM, P_TOTAL + TMM)
    spare_row = ((p // TMM) % 2) * TMM + p % TMM
    return np.concatenate([spare_row, 2 * T_PAD + spare_row]).astype(np.int32) * SUBLANES


FILL_UNROLL = 8


def _slot_fill_kernel(pos_ref, init_hbm, slots_hbm, buf, sem):
    load = pltpu.make_async_copy(init_hbm, buf, sem)
    load.start()
    load.wait()

    def body(j, carry):
        for u in range(FILL_UNROLL):
            t = j * FILL_UNROLL + u
            for k in range(2):
                q = TMM + pos_ref[2 * t + k]
                buf[q] = t * SUBLANES
                buf[SLOT_ENTRIES + q] = (k * T_PAD + t) * SUBLANES
        return carry

    lax.fori_loop(0, T_PAD // FILL_UNROLL, body, 0)
    store = pltpu.make_async_copy(buf, slots_hbm, sem)
    store.start()
    store.wait()


def _slot_fill(pos):
    return pl.pallas_call(
        _slot_fill_kernel,
        in_specs=[pl.BlockSpec(memory_space=pltpu.SMEM), pl.BlockSpec(memory_space=pl.ANY)],
        out_specs=pl.BlockSpec(memory_space=pl.ANY),
        out_shape=jax.ShapeDtypeStruct((2 * SLOT_ENTRIES,), jnp.int32),
        scratch_shapes=[pltpu.SMEM((2 * SLOT_ENTRIES,), jnp.int32), pltpu.SemaphoreType.DMA(())],
        name="slot_fill",
    )(pos.reshape(-1), jnp.asarray(_padding_slot_entries()))


def _slot_plan(idx, rank, counts):
    cnt = counts[0, :N_EXPERTS]
    tiles = (cnt + TMM - 1) // TMM
    tile_end = jnp.cumsum(tiles)
    start = (tile_end - tiles) * TMM
    n_used = tile_end[-1:].astype(jnp.int32)
    pos = (start[idx] + rank).astype(jnp.int32)
    slots = _slot_fill(pos)
    tile_ids = jnp.arange(NT_MOE, dtype=jnp.int32)
    tile_expert = jnp.minimum(jnp.sum(tile_end[None, :] <= tile_ids[:, None], axis=1),
                              N_EXPERTS - 1).astype(jnp.int32)
    last_expert = tile_expert[jnp.maximum(n_used[0] - 1, 0)]
    tile_expert = jnp.where(tile_ids < n_used[0], tile_expert, last_expert)
    return slots, tile_expert, n_used


def kernel(x_prompt, x_sample, cache_swa_k, cache_swa_v, norm_mix, norm_ffn, norm_final,
           sgu_w_in, sgu_b_in, sgu_ln_g, sgu_ln_b, sgu_w_s, sgu_b_s, sgu_w_out,
           attn_w_qkv, attn_sinks, attn_w_o,
           ffn_w_gate, ffn_w_up, ffn_w_down,
           moe_w_router, moe_w_gate, moe_w_up, moe_w_down):
    moe_wg, moe_wu, moe_wd = (w.astype(BF16) for w in (moe_w_gate, moe_w_up, moe_w_down))
    sgu_v_p, sgu_v_s, k_p, v_p, k_s, v_s = [], [], [], [], [], []
    rows_p = min(WINDOW, SEQ)
    rows, source = (x_prompt.reshape(N_PROMPT, D_MODEL), x_sample.reshape(DEC_BATCH, D_MODEL)), "inputs"
    for i in range(DEPTH):
        j = i // 2
        if i % 2 == 0:
            x, vlast = _sgu_layer(rows, source, norm_mix[i], sgu_w_in[j], sgu_b_in[j], sgu_ln_g[j],
                                  sgu_ln_b[j], sgu_w_s[j], sgu_b_s[j], sgu_w_out[j])
            sgu_v_p.append(vlast[:BATCH * CHUNK].reshape(BATCH, CHUNK, SGU_WIDTH))
            sgu_v_s.append(vlast[BATCH * CHUNK:].reshape(DEC_BATCH, 1, SGU_WIDTH))
            x = _ffn_layer(x, norm_ffn[i], ffn_w_gate[j], ffn_w_up[j], ffn_w_down[j])
        else:
            q, q_s, kv = _qkv_layer(x, norm_mix[i], attn_w_qkv[j])
            kv_p = kv[:N_PROMPT].reshape(BATCH, SEQ, 2 * KV_DIM)[:, SEQ - rows_p:]
            k_p.append(kv_p[..., :KV_DIM].reshape(BATCH, rows_p, N_KV_HEADS, HEAD_DIM))
            v_p.append(kv_p[..., KV_DIM:].reshape(BATCH, rows_p, N_KV_HEADS, HEAD_DIM))
            kv_s = kv[N_PROMPT:T_REAL]
            k_new = kv_s[:, None, :KV_DIM].reshape(DEC_BATCH, 1, N_KV_HEADS, HEAD_DIM)
            v_new = kv_s[:, None, KV_DIM:].reshape(DEC_BATCH, 1, N_KV_HEADS, HEAD_DIM)
            k_s.append(jnp.concatenate([cache_swa_k[j][:, 1:], k_new], axis=1))
            v_s.append(jnp.concatenate([cache_swa_v[j][:, 1:], v_new], axis=1))

            o = _swa_prompt(q, kv, attn_sinks[j].astype(F32))
            o_s = _swa_sample(q_s, kv_s, cache_swa_k[j], cache_swa_v[j], attn_sinks[j])
            x, idx, gates, rank, counts = _proj_route(x, o, o_s, attn_w_o[j], norm_ffn[i],
                                                      moe_w_router[j])
            slots, tile_expert, n_used = _slot_plan(idx, rank, counts)
            y_slots = _moe_layer(x, norm_ffn[i], tile_expert, n_used, slots, j,
                                 moe_wg, moe_wu, moe_wd)
            rows, source = (x, gates, y_slots), "experts"
    y_prompt, y_sample = _final_combine(*rows, norm_final)
    y_prompt = y_prompt.reshape(BATCH, SEQ, D_MODEL)
    y_sample = y_sample.reshape(DEC_BATCH, 1, D_MODEL)
    return (y_prompt, y_sample, jnp.stack(sgu_v_p), jnp.stack(sgu_v_s),
            jnp.stack(k_p), jnp.stack(v_p), jnp.stack(k_s), jnp.stack(v_s))
```

```python
import functools

import numpy as np
import jax
import jax.numpy as jnp
from jax import lax
from jax.experimental import pallas as pl
from jax.experimental.pallas import tpu as pltpu

D_MODEL = 1024
BATCH = 4
SEQ = 4096
DEPTH = 4
DEC_BATCH = 128
PAST_LEN = 8192
CHUNK = 128
SGU_WIDTH = 2 * D_MODEL
SGU_GROUPS = 8
SGU_GROUP_DIM = SGU_WIDTH // SGU_GROUPS
WINDOW = 128
BLOCK = 128
HEAD_DIM = 64
N_HEADS = D_MODEL // HEAD_DIM
N_KV_HEADS = 2
GQA_GROUP = N_HEADS // N_KV_HEADS
Q_DIM = N_HEADS * HEAD_DIM
KV_DIM = N_KV_HEADS * HEAD_DIM
D_FF = 2816
N_EXPERTS = 8
D_FF_EXPERT = 3584
EPS = 1e-6
LN_EPS = 1e-5

F32 = jnp.float32
BF16 = jnp.bfloat16

LANES = 128
TM = 512
N_PROMPT = BATCH * SEQ
T_REAL = N_PROMPT + DEC_BATCH
N_TILES = -(-T_REAL // TM)
T_PAD = N_TILES * TM
PROMPT_TILES = N_PROMPT // TM
TILES_PER_SEQ = SEQ // TM
CHUNKS_PER_TILE = TM // CHUNK
N_PAIRS = N_HEADS // 2

TMM = 512
TF = 1792
NF = D_FF_EXPERT // TF
ROWS_PER_STEP = TMM // NF
N_SLOTS = 2 * T_PAD
NT_MOE = (N_SLOTS + N_EXPERTS * (TMM - 1)) // TMM + 1
P_TOTAL = NT_MOE * TMM
Y_ROWS = 2 * T_PAD + 2 * TMM
SLOT_ENTRIES = P_TOTAL + 2 * TMM

VMEM_LIMIT = 56 * 1024 * 1024

_SLOPES = [2.0 ** (-8.0 * (h + 1) / N_HEADS) for h in range(N_HEADS)]


def _rms(x, g):
    return x * lax.rsqrt(jnp.mean(x * x, axis=-1, keepdims=True) + EPS) * g


def _gelu(x):
    k = -2.0 * np.sqrt(2.0 / np.pi) * np.log2(np.e)
    t = (x * x) * np.float32(0.044715 * k) + np.float32(k)
    return x * (1.0 / (1.0 + jnp.exp2(x * t)))


def _silu(x):
    return x * (1.0 / (1.0 + jnp.exp(-x)))


def _dot(a, b):
    return jnp.dot(a, b, preferred_element_type=F32)


def _split(x):
    hi = x.astype(BF16)
    return hi, (x - hi.astype(F32)).astype(BF16)


def _operand(x, precise):
    return _split(x) if precise else x.astype(BF16)


def _mm(x, w_hi, w_lo):
    if not isinstance(x, tuple):
        return _dot(x, w_hi)
    x_hi, x_lo = x
    return _dot(x_hi, w_hi) + (_dot(x_lo, w_hi) + _dot(x_hi, w_lo))


def _hi_lo(w):
    hi = lax.optimization_barrier(w.astype(BF16))
    return hi, (w - hi.astype(F32)).astype(BF16)


def _dot_t(a, b):
    return lax.dot_general(a, b, (((1,), (1,)), ((), ())), preferred_element_type=F32)


SUBLANES = 8
ROW_CHUNKS = D_MODEL // LANES


def _chunk(s, rows):
    return pl.ds(s, rows, stride=SUBLANES)


def _load_row_tiles(ref, rows):
    return jnp.concatenate([ref[_chunk(s, rows), :] for s in range(ROW_CHUNKS)], axis=1)


def _store_row_tiles(ref, x):
    rows = x.shape[0]
    for s in range(ROW_CHUNKS):
        ref[_chunk(s, rows), :] = x[:, s * LANES:(s + 1) * LANES]


def _const_spec(shape):
    nd = len(shape)
    return pl.BlockSpec(shape, lambda *_: (0,) * nd, pipeline_mode=pl.Buffered(1))


def _params(n_axes=1):
    return pltpu.CompilerParams(dimension_semantics=("arbitrary",) * n_axes,
                                vmem_limit_bytes=VMEM_LIMIT)


def _combined_rows(x_ref, gate_ref, y0_ref, y1_ref, rows=TM):
    gate = gate_ref[:rows, :]
    g0, g1 = gate[:, 0:1], gate[:, 1:2]
    return jnp.concatenate(
        [x_ref[_chunk(s, rows), :] + (g0 * y0_ref[_chunk(s, rows), :] + g1 * y1_ref[_chunk(s, rows), :])
         for s in range(ROW_CHUNKS)], axis=1)


def _sgu_kernel(*refs, source):
    i = pl.program_id(0)
    is_sample = i == N_TILES - 1
    n_src = 2 if source == "inputs" else 4
    src, refs = refs[:n_src], refs[n_src:]
    (g_ref, win_hi, win_lo, bin_ref, lng_ref, lnb_ref, ws_ref, bs_ref, wout_hi, wout_lo,
     xo_ref, vlast_ref, ug_ref) = refs

    def mix(x, sample):
        h = _operand(_rms(x, g_ref[...]), sample)
        v = _gelu(_mm(h, win_hi[:, SGU_WIDTH:], win_lo[:, SGU_WIDTH:]) + bin_ref[:, SGU_WIDTH:])
        mu = jnp.mean(v, axis=-1, keepdims=True)
        vc = v - mu
        var = jnp.mean(vc * vc, axis=-1, keepdims=True)
        vn = vc * lax.rsqrt(var + LN_EPS) * lng_ref[...] + lnb_ref[...]
        vb = vn.astype(BF16)
        row = lax.broadcasted_iota(jnp.int32, (CHUNK, CHUNK), 0)
        col = lax.broadcasted_iota(jnp.int32, (CHUNK, CHUNK), 1)
        y = jnp.zeros_like(x)
        for g in range(SGU_GROUPS):
            lo, hi = g * SGU_GROUP_DIM, (g + 1) * SGU_GROUP_DIM
            u = _gelu(_mm(h, win_hi[:, lo:hi], win_lo[:, lo:hi]) + bin_ref[:, lo:hi])
            if sample:
                gate = vn[:, lo:hi] * ws_ref[g][0:1, 0:1] + bs_ref[g][0:1, 0:1]
                y = y + _mm(_split(u * gate), wout_hi[lo:hi, :], wout_lo[lo:hi, :])
            else:
                w_tril = jnp.where(row >= col, ws_ref[g], 0.0).astype(BF16)
                gate = jnp.concatenate(
                    [_dot(w_tril, vb[c * CHUNK:(c + 1) * CHUNK, lo:hi]) + bs_ref[g]
                     for c in range(x.shape[0] // CHUNK)], axis=0)
                ug_ref[:, lo:hi] = (u * gate).astype(BF16)
        if not sample:
            y = _dot(ug_ref[...], wout_hi[...])
        return x + y, vn

    @pl.when(jnp.logical_not(is_sample))
    def _():
        x = src[0][...] if source == "inputs" else _combined_rows(*src)
        x_new, vn = mix(x, sample=False)
        xo_ref[...] = x_new
        vlast_ref[...] = vn[TM - CHUNK:]

    @pl.when(is_sample)
    def _():
        x = src[1][...] if source == "inputs" else _combined_rows(*src, rows=DEC_BATCH)
        x_new, vn = mix(x, sample=True)
        xo_ref[:DEC_BATCH, :] = x_new
        xo_ref[DEC_BATCH:, :] = jnp.zeros((TM - DEC_BATCH, D_MODEL), F32)
        vlast_ref[...] = vn


def _row_tile_specs():
    return [pl.BlockSpec((TM * SUBLANES, LANES), lambda i: (i, 0)),
            pl.BlockSpec((TM, 2), lambda i: (i, 0)),
            pl.BlockSpec((TM * SUBLANES, LANES), lambda i: (i, 0)),
            pl.BlockSpec((TM * SUBLANES, LANES), lambda i: (N_TILES + i, 0))]


def _sgu_layer(rows, source, g, w_in, b_in, ln_g, ln_b, w_s, b_s, w_out):
    tile = pl.BlockSpec((TM, D_MODEL), lambda i: (i, 0))
    vlast_spec = pl.BlockSpec(
        (CHUNK, SGU_WIDTH),
        lambda i: (jnp.where(i == N_TILES - 1, BATCH, i // TILES_PER_SEQ), 0))
    if source == "inputs":
        row_specs = [pl.BlockSpec((TM, D_MODEL), lambda i: (jnp.minimum(i, PROMPT_TILES - 1), 0)),
                     pl.BlockSpec((DEC_BATCH, D_MODEL), lambda i: (0, 0))]
        row_args = rows
    else:
        row_specs = _row_tile_specs()
        x, gates, y_slots = rows
        row_args = (x, gates, y_slots, y_slots)
    return pl.pallas_call(
        functools.partial(_sgu_kernel, source=source),
        grid=(N_TILES,),
        in_specs=row_specs + [
                  _const_spec((1, D_MODEL)),
                  _const_spec((D_MODEL, 2 * SGU_WIDTH)),
                  _const_spec((D_MODEL, 2 * SGU_WIDTH)),
                  _const_spec((1, 2 * SGU_WIDTH)),
                  _const_spec((1, SGU_WIDTH)),
                  _const_spec((1, SGU_WIDTH)),
                  _const_spec((SGU_GROUPS, CHUNK, CHUNK)),
                  _const_spec((SGU_GROUPS, CHUNK, 1)),
                  _const_spec((SGU_WIDTH, D_MODEL)),
                  _const_spec((SGU_WIDTH, D_MODEL))],
        out_specs=[tile, vlast_spec],
        out_shape=[jax.ShapeDtypeStruct((T_PAD, D_MODEL), F32),
                   jax.ShapeDtypeStruct(((BATCH + 1) * CHUNK, SGU_WIDTH), F32)],
        scratch_shapes=[pltpu.VMEM((TM, SGU_WIDTH), BF16)],
        compiler_params=_params(),
        name="sgu_mixer",
    )(*row_args, g.reshape(1, -1), *_hi_lo(w_in), b_in.reshape(1, -1), ln_g.reshape(1, -1),
      ln_b.reshape(1, -1), w_s, b_s.reshape(SGU_GROUPS, CHUNK, 1), *_hi_lo(w_out))


FF_SPLIT = 1


def _ffn_kernel(x_ref, g_ref, wg_hi, wg_lo, wu_hi, wu_lo, wd_hi, wd_lo, xo_ref):
    is_sample = pl.program_id(0) == N_TILES - 1
    width = D_FF // FF_SPLIT

    def ffn(x, sample):
        h = _operand(_rms(x, g_ref[...]), sample)
        y = x
        for c in range(FF_SPLIT):
            cols = slice(c * width, (c + 1) * width)
            a = _silu(_mm(h, wg_hi[:, cols], wg_lo[:, cols])) * _mm(h, wu_hi[:, cols], wu_lo[:, cols])
            y = y + _mm(_operand(a, sample), wd_hi[cols, :], wd_lo[cols, :])
        return y

    @pl.when(jnp.logical_not(is_sample))
    def _():
        xo_ref[...] = ffn(x_ref[...], sample=False)

    @pl.when(is_sample)
    def _():
        xo_ref[:DEC_BATCH, :] = ffn(x_ref[:DEC_BATCH, :], sample=True)
        xo_ref[DEC_BATCH:, :] = jnp.zeros((TM - DEC_BATCH, D_MODEL), F32)


def _ffn_layer(x, g, w_gate, w_up, w_down):
    tile = pl.BlockSpec((TM, D_MODEL), lambda i: (i, 0))
    return pl.pallas_call(
        _ffn_kernel,
        grid=(N_TILES,),
        in_specs=[tile, _const_spec((1, D_MODEL))]
        + [_const_spec((D_MODEL, D_FF))] * 4 + [_const_spec((D_FF, D_MODEL))] * 2,
        out_specs=tile,
        out_shape=jax.ShapeDtypeStruct((T_PAD, D_MODEL), F32),
        compiler_params=_params(),
        name="dense_swiglu",
    )(x, g.reshape(1, -1), *_hi_lo(w_gate), *_hi_lo(w_up), *_hi_lo(w_down))


def _qkv_kernel(x_ref, g_ref, w_hi, w_lo, q_ref, qs_ref, kv_ref):
    is_sample = pl.program_id(0) == N_TILES - 1

    def qkv(x, sample):
        out = _mm(_operand(_rms(x, g_ref[...]), sample), w_hi[...], w_lo[...])
        return out[:, :Q_DIM] * (HEAD_DIM ** -0.5), out[:, Q_DIM:]

    @pl.when(jnp.logical_not(is_sample))
    def _():
        q, kv = qkv(x_ref[...], sample=False)
        q_ref[...] = q.astype(BF16)
        kv_ref[...] = kv

    @pl.when(is_sample)
    def _():
        q, kv = qkv(x_ref[:DEC_BATCH, :], sample=True)
        qs_ref[...] = q
        kv_ref[:DEC_BATCH, :] = kv
        kv_ref[DEC_BATCH:, :] = jnp.zeros((TM - DEC_BATCH, 2 * KV_DIM), F32)


def _qkv_layer(x, g, w_qkv):
    tile = pl.BlockSpec((TM, D_MODEL), lambda i: (i, 0))
    return pl.pallas_call(
        _qkv_kernel,
        grid=(N_TILES,),
        in_specs=[tile, _const_spec((1, D_MODEL)),
                  _const_spec((D_MODEL, Q_DIM + 2 * KV_DIM)), _const_spec((D_MODEL, Q_DIM + 2 * KV_DIM))],
        out_specs=[pl.BlockSpec((TM, Q_DIM), lambda i: (jnp.minimum(i, PROMPT_TILES - 1), 0)),
                   pl.BlockSpec((DEC_BATCH, Q_DIM), lambda i: (0, 0)),
                   pl.BlockSpec((TM, 2 * KV_DIM), lambda i: (i, 0))],
        out_shape=[jax.ShapeDtypeStruct((N_PROMPT, Q_DIM), BF16),
                   jax.ShapeDtypeStruct((DEC_BATCH, Q_DIM), F32),
                   jax.ShapeDtypeStruct((T_PAD, 2 * KV_DIM), F32)],
        compiler_params=_params(),
        name="swa_qkv",
    )(x, g.reshape(1, -1), *_hi_lo(w_qkv))


def _half_masks_f32(x):
    lane = lax.broadcasted_iota(jnp.int32, x.shape, 1)
    low = lane < HEAD_DIM
    xr = pltpu.roll(x, HEAD_DIM, 1)
    return ((jnp.where(low, x, 0.0), jnp.where(low, 0.0, xr)),
            (jnp.where(low, xr, 0.0), jnp.where(low, 0.0, x)))


def _half_masks(x):
    return tuple(tuple(m.astype(BF16) for m in pair) for pair in _half_masks_f32(x))


def _swa_prompt_kernel(sink_ref, q_ref, kv_ref, kvp_ref, wg_ref, wu_ref, wd_ref,
                       o_ref, wg_out, wu_out, wd_out, bias_ref):
    wg_out[...] = wg_ref[...].astype(BF16)
    wu_out[...] = wu_ref[...].astype(BF16)
    wd_out[...] = wd_ref[...].astype(BF16)
    i = pl.program_id(0)
    t = lax.broadcasted_iota(jnp.int32, (BLOCK, BLOCK), 0)
    c = lax.broadcasted_iota(jnp.int32, (BLOCK, BLOCK), 1)
    own = c <= t
    diag = c == t

    @pl.when(i == 0)
    def _():
        dist = jnp.where(own, t - c, BLOCK + t - c).astype(F32)
        for hd in range(N_HEADS):
            bias_ref[hd] = _SLOPES[hd] * dist

    first_tile = (i % TILES_PER_SEQ) == 0
    kv_all = jnp.concatenate([kvp_ref[...], kv_ref[...]], axis=0)
    k_blk, v_blk, vf_blk = [], [], []
    for b in range(CHUNKS_PER_TILE + 1):
        blk = kv_all[b * BLOCK:(b + 1) * BLOCK]
        k_blk.append(_half_masks(blk[:, :KV_DIM]))
        v_blk.append(_half_masks(blk[:, KV_DIM:]))
        vf_blk.append(_half_masks_f32(blk[:, KV_DIM:]))
    has_prev = (jnp.zeros((BLOCK, BLOCK), jnp.int32) + jnp.where(first_tile, 0, 1)) == 1

    for b in range(CHUNKS_PER_TILE):
        rows = slice(b * BLOCK, (b + 1) * BLOCK)
        for p in range(N_PAIRS):
            kvh = (2 * p) // GQA_GROUP
            qp = q_ref[rows, p * LANES:(p + 1) * LANES]
            acc = None
            for par in range(2):
                hd = 2 * p + par
                sink = sink_ref[hd]
                keys = jnp.concatenate([k_blk[b][kvh][par], k_blk[b + 1][kvh][par]], axis=0)
                both = _dot_t(qp, keys)
                l_prev, l_own = both[:, :BLOCK], both[:, BLOCK:]
                if b == 0:
                    l_prev = jnp.where(has_prev, l_prev, -jnp.inf)
                logits = jnp.where(own, l_own, l_prev) - bias_ref[hd]
                extra = (jnp.sum(jnp.where(diag, l_prev, 0.0), axis=-1, keepdims=True)
                         - _SLOPES[hd] * BLOCK)
                m = jnp.maximum(jnp.maximum(jnp.max(logits, axis=-1, keepdims=True), extra), sink)
                e = jnp.exp(logits - m)
                e_extra = jnp.exp(extra - m)
                denom = jnp.sum(e, axis=-1, keepdims=True) + e_extra + jnp.exp(sink - m)
                probs = jnp.concatenate([jnp.where(own, 0.0, e).astype(BF16),
                                         jnp.where(own, e, 0.0).astype(BF16)], axis=1)
                vals = jnp.concatenate([v_blk[b][kvh][par], v_blk[b + 1][kvh][par]], axis=0)
                part = (_dot(probs, vals) + e_extra * vf_blk[b][kvh][par]) * (1.0 / denom)
                acc = part if acc is None else acc + part
            o_ref[rows, p * LANES:(p + 1) * LANES] = acc.astype(BF16)


def _swa_prompt(q, kv, sinks, layer, w_gate, w_up, w_down):
    n_layers = w_gate.shape[0]
    up_rows = N_EXPERTS * D_MODEL // PROMPT_TILES
    down_rows = N_EXPERTS * D_FF_EXPERT // PROMPT_TILES
    up_spec = pl.BlockSpec((None, up_rows, D_FF_EXPERT), lambda i: (layer, i, 0))
    down_spec = pl.BlockSpec((None, down_rows, D_MODEL), lambda i: (layer, i, 0))
    o, wg, wu, wd = pl.pallas_call(
        _swa_prompt_kernel,
        grid=(PROMPT_TILES,),
        in_specs=[pl.BlockSpec(memory_space=pltpu.SMEM),
                  pl.BlockSpec((TM, Q_DIM), lambda i: (i, 0)),
                  pl.BlockSpec((TM, 2 * KV_DIM), lambda i: (i, 0)),
                  pl.BlockSpec((BLOCK, 2 * KV_DIM),
                               lambda i: (jnp.maximum(i * CHUNKS_PER_TILE - 1, 0), 0)),
                  up_spec, up_spec, down_spec],
        out_specs=[pl.BlockSpec((TM, Q_DIM), lambda i: (i, 0)),
                   pl.BlockSpec((up_rows, D_FF_EXPERT), lambda i: (i, 0)),
                   pl.BlockSpec((up_rows, D_FF_EXPERT), lambda i: (i, 0)),
                   pl.BlockSpec((down_rows, D_MODEL), lambda i: (i, 0))],
        out_shape=[jax.ShapeDtypeStruct((N_PROMPT, Q_DIM), BF16),
                   jax.ShapeDtypeStruct((N_EXPERTS * D_MODEL, D_FF_EXPERT), BF16),
                   jax.ShapeDtypeStruct((N_EXPERTS * D_MODEL, D_FF_EXPERT), BF16),
                   jax.ShapeDtypeStruct((N_EXPERTS * D_FF_EXPERT, D_MODEL), BF16)],
        scratch_shapes=[pltpu.VMEM((N_HEADS, BLOCK, BLOCK), F32)],
        compiler_params=_params(),
        name="swa_prompt",
    )(sinks, q, kv, kv,
      w_gate.reshape(n_layers, N_EXPERTS * D_MODEL, D_FF_EXPERT),
      w_up.reshape(n_layers, N_EXPERTS * D_MODEL, D_FF_EXPERT),
      w_down.reshape(n_layers, N_EXPERTS * D_FF_EXPERT, D_MODEL))
    return (o, wg.reshape(N_EXPERTS, D_MODEL, D_FF_EXPERT), wu.reshape(N_EXPERTS, D_MODEL, D_FF_EXPERT),
            wd.reshape(N_EXPERTS, D_FF_EXPERT, D_MODEL))


SAMPLE_TILE = 32


def _swa_sample_kernel(q_ref, kvn_ref, ck_ref, cv_ref, slope_ref, sink_ref, o_ref):
    shape = (SAMPLE_TILE, N_PAIRS, LANES)
    lane = lax.broadcasted_iota(jnp.int32, shape, 2)
    pair = lax.broadcasted_iota(jnp.int32, shape, 1)
    low = lane < HEAD_DIM
    kv0 = pair < (N_PAIRS // 2)

    def swap(x):
        return pltpu.roll(x, HEAD_DIM, 2)

    def three_pass(dims, a, b):
        dot = lambda u, v: lax.dot_general(u, v, dims, preferred_element_type=F32)
        return dot(a[0], b[0]) + (dot(a[1], b[0]) + dot(a[0], b[1]))

    qk_dims = (((2,), (2,)), ((0,), (0,)))
    pv_dims = (((2,), (1,)), ((0,), (0,)))

    q = q_ref[...]
    q_even = jnp.where(low, q, 0.0)
    q_odd = jnp.where(low, 0.0, q)
    q_al = (jnp.where(kv0, q_even, swap(q_even)), jnp.where(kv0, swap(q_odd), q_odd))

    ck = _split(ck_ref[...])
    cv = _split(cv_ref[...])
    kvn = kvn_ref[...]
    k_new = kvn[:, :, :KV_DIM]
    v_new = kvn[:, :, KV_DIM:]
    r = lax.broadcasted_iota(jnp.int32, (SAMPLE_TILE, N_PAIRS, WINDOW), 2)
    dist = (WINDOW - r).astype(F32)

    outs = []
    for par in range(2):
        qa = q_al[par]
        slope = slope_ref[par]
        sink = sink_ref[par]
        logits = three_pass(qk_dims, _split(qa), ck) - slope * dist
        l_self = jnp.sum(qa * k_new, axis=-1, keepdims=True)
        m = jnp.maximum(jnp.maximum(jnp.max(logits, axis=-1, keepdims=True), l_self), sink)
        e = jnp.exp(logits - m)
        e_self = jnp.exp(l_self - m)
        inv = 1.0 / (jnp.sum(e, axis=-1, keepdims=True) + e_self + jnp.exp(sink - m))
        o = three_pass(pv_dims, _split(e * inv), cv) + (e_self * inv) * v_new
        outs.append(o)
    o_even = jnp.where(kv0, outs[0], swap(outs[0]))
    o_odd = jnp.where(kv0, swap(outs[1]), outs[1])
    o_ref[...] = jnp.where(low, o_even, o_odd)


def _swa_sample(q_s, kv_s, cache_k, cache_v, sinks):
    rows = cache_k.shape[1]
    slopes = np.asarray(_SLOPES, np.float32).reshape(N_PAIRS, 2).T.reshape(2, N_PAIRS, 1)
    sink_arr = sinks.astype(F32).reshape(N_PAIRS, 2).T.reshape(2, N_PAIRS, 1)
    blk = lambda *shape: pl.BlockSpec((SAMPLE_TILE,) + shape, lambda i: (i,) + (0,) * len(shape))
    o3 = pl.pallas_call(
        _swa_sample_kernel,
        grid=(DEC_BATCH // SAMPLE_TILE,),
        in_specs=[blk(N_PAIRS, LANES), blk(1, 2 * KV_DIM), blk(rows, KV_DIM), blk(rows, KV_DIM),
                  _const_spec((2, N_PAIRS, 1)), _const_spec((2, N_PAIRS, 1))],
        out_specs=blk(N_PAIRS, LANES),
        out_shape=jax.ShapeDtypeStruct((DEC_BATCH, N_PAIRS, LANES), F32),
        compiler_params=_params(),
        name="swa_sample",
    )(q_s.reshape(DEC_BATCH, N_PAIRS, LANES), kv_s.reshape(DEC_BATCH, 1, 2 * KV_DIM),
      cache_k.reshape(DEC_BATCH, rows, KV_DIM), cache_v.reshape(DEC_BATCH, rows, KV_DIM),
      jnp.asarray(slopes), sink_arr)
    return o3.reshape(DEC_BATCH, Q_DIM)


def _proj_route_kernel(x_ref, op_ref, os_ref, wo_hi, wo_lo, g_ref, wr_hi, wr_lo, xo_ref, idx_ref,
                       gate_ref, rank_ref, cnt_ref, carry_ref, xnew_ref):
    i = pl.program_id(0)
    is_sample = i == N_TILES - 1

    @pl.when(i == 0)
    def _():
        carry_ref[...] = jnp.zeros_like(carry_ref)

    @pl.when(jnp.logical_not(is_sample))
    def _():
        xnew_ref[...] = x_ref[...] + _dot(op_ref[...], wo_hi[...])

    @pl.when(is_sample)
    def _():
        xnew_ref[:DEC_BATCH, :] = x_ref[:DEC_BATCH, :] + _mm(_split(os_ref[...]), wo_hi[...], wo_lo[...])
        xnew_ref[DEC_BATCH:, :] = jnp.zeros((TM - DEC_BATCH, D_MODEL), F32)

    x = xnew_ref[...]
    _store_row_tiles(xo_ref, x)
    h = _rms(x, g_ref[...])
    logits = _mm(_split(h), wr_hi[...], wr_lo[...])
    lane = lax.broadcasted_iota(jnp.int32, (TM, LANES), 1)
    logits = jnp.where(lane < N_EXPERTS, logits, -jnp.inf)
    m0 = jnp.max(logits, axis=-1, keepdims=True)
    i0 = jnp.min(jnp.where(logits == m0, lane, LANES), axis=-1, keepdims=True)
    rest = jnp.where(lane == i0, -jnp.inf, logits)
    m1 = jnp.max(rest, axis=-1, keepdims=True)
    i1 = jnp.min(jnp.where(rest == m1, lane, LANES), axis=-1, keepdims=True)
    e1 = jnp.exp(m1 - m0)
    g0 = 1.0 / (1.0 + e1)
    g1 = e1 * g0
    idx_ref[...] = jnp.concatenate([i0, i1], axis=1)
    gate_ref[...] = jnp.concatenate([g0, g1], axis=1)

    onehot = jnp.where((lane == i0) | (lane == i1), 1.0, 0.0)
    r = lax.broadcasted_iota(jnp.int32, (TM, TM), 0)
    c = lax.broadcasted_iota(jnp.int32, (TM, TM), 1)
    before = jnp.where(c < r, 1.0, 0.0).astype(BF16)
    ranks = _dot(before, onehot.astype(BF16)) + carry_ref[...]
    r0 = jnp.sum(jnp.where(lane == i0, ranks, 0.0), axis=-1, keepdims=True)
    r1 = jnp.sum(jnp.where(lane == i1, ranks, 0.0), axis=-1, keepdims=True)
    rank_ref[...] = jnp.concatenate([r0, r1], axis=1).astype(jnp.int32)
    carry_ref[...] = carry_ref[...] + jnp.sum(onehot, axis=0, keepdims=True)
    cnt_ref[...] = carry_ref[...].astype(jnp.int32)


def _proj_route(x, o_prompt, o_sample, w_o, g, w_router):
    tile = pl.BlockSpec((TM, D_MODEL), lambda i: (i, 0))
    pair = pl.BlockSpec((TM, 2), lambda i: (i, 0))
    wr = jnp.zeros((D_MODEL, LANES), F32).at[:, :N_EXPERTS].set(w_router)
    return pl.pallas_call(
        _proj_route_kernel,
        grid=(N_TILES,),
        in_specs=[tile,
                  pl.BlockSpec((TM, Q_DIM), lambda i: (jnp.minimum(i, PROMPT_TILES - 1), 0)),
                  pl.BlockSpec((DEC_BATCH, Q_DIM), lambda i: (0, 0)),
                  _const_spec((Q_DIM, D_MODEL)), _const_spec((Q_DIM, D_MODEL)),
                  _const_spec((1, D_MODEL)), _const_spec((D_MODEL, LANES)),
                  _const_spec((D_MODEL, LANES))],
        out_specs=[pl.BlockSpec((TM * SUBLANES, LANES), lambda i: (i, 0)), pair, pair, pair,
                   pl.BlockSpec((1, LANES), lambda i: (0, 0))],
        out_shape=[jax.ShapeDtypeStruct((T_PAD * SUBLANES, LANES), F32),
                   jax.ShapeDtypeStruct((T_PAD, 2), jnp.int32),
                   jax.ShapeDtypeStruct((T_PAD, 2), F32),
                   jax.ShapeDtypeStruct((T_PAD, 2), jnp.int32),
                   jax.ShapeDtypeStruct((1, LANES), jnp.int32)],
        scratch_shapes=[pltpu.VMEM((1, LANES), F32), pltpu.VMEM((TM, D_MODEL), F32)],
        compiler_params=_params(),
        name="proj_route",
    )(x, o_prompt, o_sample, *_hi_lo(w_o), g.reshape(1, -1), *_hi_lo(wr))


def _tile_at(start):
    return pl.ds(start if isinstance(start, int) else pl.multiple_of(start, SUBLANES), SUBLANES)


def _gather_row(x_hbm, xbuf, sem, slot, src, r):
    return pltpu.make_async_copy(x_hbm.at[_tile_at(src), :], xbuf.at[slot, _tile_at(r * SUBLANES), :],
                                 sem.at[slot])


def _scatter_row(obuf, y_hbm, sem, slot, r, dst):
    return pltpu.make_async_copy(obuf.at[slot, _tile_at(r * SUBLANES), :], y_hbm.at[_tile_at(dst), :],
                                 sem.at[slot])


def _gather_wait(x_hbm, xbuf, sem, slot):
    pltpu.make_async_copy(x_hbm.at[pl.ds(0, TMM * SUBLANES), :], xbuf.at[slot], sem.at[slot]).wait()


def _scatter_wait(obuf, y_hbm, sem, slot):
    pltpu.make_async_copy(obuf.at[slot], y_hbm.at[pl.ds(0, TMM * SUBLANES), :], sem.at[slot]).wait()


def _slot_src(slot_ref, q):
    return slot_ref[q]


def _slot_dst(slot_ref, q):
    return slot_ref[SLOT_ENTRIES + q]


def _moe_kernel(te_ref, nu_ref, slot_ref, x_hbm, g_ref, wg_ref, wu_ref, wd_ref, y_hbm,
                xbuf, hbuf, acc, obuf, gsem, ssem):
    i = pl.program_id(0)
    f = pl.program_id(1)
    n_used = nu_ref[0]
    slot = i % 2
    other = 1 - slot

    @pl.when(i < n_used)
    def _():
        @pl.when((i == 0) & (f == 0))
        def _():
            def start(r, carry):
                _gather_row(x_hbm, xbuf, gsem, 0, _slot_src(slot_ref, TMM + r), r).start()
                return carry
            lax.fori_loop(0, TMM, start, 0)
            obuf[...] = jnp.zeros_like(obuf)
            for s in range(2):
                spare = pltpu.make_async_copy(
                    obuf.at[s],
                    y_hbm.at[pl.ds((2 * T_PAD + s * TMM) * SUBLANES, TMM * SUBLANES), :],
                    ssem.at[s])
                spare.start()
                spare.wait()

        @pl.when(f == 0)
        def _():
            _gather_wait(x_hbm, xbuf, gsem, slot)
            xin = xbuf.at[slot]
            chunks = [xin[_chunk(s, TMM), :] for s in range(ROW_CHUNKS)]
            ssq = chunks[0] * chunks[0]
            for c in chunks[1:]:
                ssq = ssq + c * c
            scale = lax.rsqrt(jnp.sum(ssq, axis=-1, keepdims=True) * (1.0 / D_MODEL) + EPS)
            for s, c in enumerate(chunks):
                cols = slice(s * LANES, (s + 1) * LANES)
                hbuf[:, cols] = (c * scale * g_ref[:, cols]).astype(BF16)
            acc[...] = jnp.zeros_like(acc)

        h = hbuf[...]
        a = (_silu(_dot(h, wg_ref[...])) * _dot(h, wu_ref[...])).astype(BF16)
        acc[...] += _dot(a, wd_ref[...])

        row0 = f * ROWS_PER_STEP
        nxt = (i + 2) * TMM + row0
        prv = i * TMM + row0
        for r in range(ROWS_PER_STEP):
            _gather_row(x_hbm, xbuf, gsem, other, _slot_src(slot_ref, nxt + r), row0 + r).start()
            _scatter_row(obuf, y_hbm, ssem, other, row0 + r, _slot_dst(slot_ref, prv + r)).start()

        @pl.when(f == NF - 1)
        def _():
            @pl.when(i >= 1)
            def _():
                _scatter_wait(obuf, y_hbm, ssem, slot)
            _store_row_tiles(obuf.at[slot], acc[...])

        @pl.when((f == NF - 1) & (i == n_used - 1))
        def _():
            _gather_wait(x_hbm, xbuf, gsem, other)
            _scatter_wait(obuf, y_hbm, ssem, other)
            base = (i + 1) * TMM

            def start(r, carry):
                _scatter_row(obuf, y_hbm, ssem, slot, r, _slot_dst(slot_ref, base + r)).start()
                return carry
            lax.fori_loop(0, TMM, start, 0)
            _scatter_wait(obuf, y_hbm, ssem, slot)


def _moe_layer(x, g, tile_expert, n_used, slots, w_gate, w_up, w_down):
    def f_eff(i, f, nu):
        return jnp.where(i < nu[0], f, NF - 1)

    grid_spec = pltpu.PrefetchScalarGridSpec(
        num_scalar_prefetch=3,
        grid=(NT_MOE, NF),
        in_specs=[pl.BlockSpec(memory_space=pl.ANY),
                  pl.BlockSpec((1, D_MODEL), lambda i, f, te, nu, tok: (0, 0)),
                  pl.BlockSpec((None, D_MODEL, TF),
                               lambda i, f, te, nu, tok: (te[i], 0, f_eff(i, f, nu))),
                  pl.BlockSpec((None, D_MODEL, TF),
                               lambda i, f, te, nu, tok: (te[i], 0, f_eff(i, f, nu))),
                  pl.BlockSpec((None, TF, D_MODEL),
                               lambda i, f, te, nu, tok: (te[i], f_eff(i, f, nu), 0))],
        out_specs=pl.BlockSpec(memory_space=pl.ANY),
        scratch_shapes=[pltpu.VMEM((2, TMM * SUBLANES, LANES), F32),
                        pltpu.VMEM((TMM, D_MODEL), BF16),
                        pltpu.VMEM((TMM, D_MODEL), F32),
                        pltpu.VMEM((2, TMM * SUBLANES, LANES), F32),
                        pltpu.SemaphoreType.DMA((2,)),
                        pltpu.SemaphoreType.DMA((2,))],
    )
    return pl.pallas_call(
        _moe_kernel,
        grid_spec=grid_spec,
        out_shape=jax.ShapeDtypeStruct((Y_ROWS * SUBLANES, LANES), F32),
        compiler_params=_params(2),
        name="moe_experts",
    )(tile_expert, n_used, slots, x, g.reshape(1, -1), w_gate, w_up, w_down)


def _final_kernel(x_ref, gate_ref, y0_ref, y1_ref, gf_ref, yp_ref, ys_ref):
    i = pl.program_id(0)
    y = _rms(_combined_rows(x_ref, gate_ref, y0_ref, y1_ref), gf_ref[...])

    @pl.when(i < PROMPT_TILES)
    def _():
        yp_ref[...] = y

    @pl.when(i == N_TILES - 1)
    def _():
        ys_ref[...] = y[:DEC_BATCH]


def _final_combine(x, gates, y_slots, g_final):
    return pl.pallas_call(
        _final_kernel,
        grid=(N_TILES,),
        in_specs=_row_tile_specs() + [_const_spec((1, D_MODEL))],
        out_specs=[pl.BlockSpec((TM, D_MODEL), lambda i: (jnp.minimum(i, PROMPT_TILES - 1), 0)),
                   pl.BlockSpec((DEC_BATCH, D_MODEL), lambda i: (0, 0))],
        out_shape=[jax.ShapeDtypeStruct((N_PROMPT, D_MODEL), F32),
                   jax.ShapeDtypeStruct((DEC_BATCH, D_MODEL), F32)],
        compiler_params=_params(),
        name="final_combine",
    )(x, gates, y_slots, y_slots, g_final.reshape(1, -1))


def _padding_slot_entries():
    p = np.arange(-TMM, P_TOTAL + TMM)
    spare_row = ((p // TMM) % 2) * TMM + p % TMM
    return np.concatenate([spare_row, 2 * T_PAD + spare_row]).astype(np.int32) * SUBLANES


FILL_UNROLL = 8


def _slot_fill_kernel(pos_ref, init_hbm, slots_hbm, buf, sem):
    load = pltpu.make_async_copy(init_hbm, buf, sem)
    load.start()
    load.wait()

    def body(j, carry):
        for u in range(FILL_UNROLL):
            t = j * FILL_UNROLL + u
            for k in range(2):
                q = TMM + pos_ref[2 * t + k]
                buf[q] = t * SUBLANES
                buf[SLOT_ENTRIES + q] = (k * T_PAD + t) * SUBLANES
        return carry

    lax.fori_loop(0, T_PAD // FILL_UNROLL, body, 0)
    store = pltpu.make_async_copy(buf, slots_hbm, sem)
    store.start()
    store.wait()


def _slot_fill(pos):
    return pl.pallas_call(
        _slot_fill_kernel,
        in_specs=[pl.BlockSpec(memory_space=pltpu.SMEM), pl.BlockSpec(memory_space=pl.ANY)],
        out_specs=pl.BlockSpec(memory_space=pl.ANY),
        out_shape=jax.ShapeDtypeStruct((2 * SLOT_ENTRIES,), jnp.int32),
        scratch_shapes=[pltpu.SMEM((2 * SLOT_ENTRIES,), jnp.int32), pltpu.SemaphoreType.DMA(())],
        name="slot_fill",
    )(pos.reshape(-1), jnp.asarray(_padding_slot_entries()))


def _slot_plan(idx, rank, counts):
    cnt = counts[0, :N_EXPERTS]
    tiles = (cnt + TMM - 1) // TMM
    tile_end = jnp.cumsum(tiles)
    start = (tile_end - tiles) * TMM
    n_used = tile_end[-1:].astype(jnp.int32)
    pos = (start[idx] + rank).astype(jnp.int32)
    slots = _slot_fill(pos)
    tile_ids = jnp.arange(NT_MOE, dtype=jnp.int32)
    tile_expert = jnp.minimum(jnp.sum(tile_end[None, :] <= tile_ids[:, None], axis=1),
                              N_EXPERTS - 1).astype(jnp.int32)
    last_expert = tile_expert[jnp.maximum(n_used[0] - 1, 0)]
    tile_expert = jnp.where(tile_ids < n_used[0], tile_expert, last_expert)
    return slots, tile_expert, n_used


def kernel(x_prompt, x_sample, cache_swa_k, cache_swa_v, norm_mix, norm_ffn, norm_final,
           sgu_w_in, sgu_b_in, sgu_ln_g, sgu_ln_b, sgu_w_s, sgu_b_s, sgu_w_out,
           attn_w_qkv, attn_sinks, attn_w_o,
           ffn_w_gate, ffn_w_up, ffn_w_down,
           moe_w_router, moe_w_gate, moe_w_up, moe_w_down):
    sgu_v_p, sgu_v_s, k_p, v_p, k_s, v_s = [], [], [], [], [], []
    rows_p = min(WINDOW, SEQ)
    rows, source = (x_prompt.reshape(N_PROMPT, D_MODEL), x_sample.reshape(DEC_BATCH, D_MODEL)), "inputs"
    for i in range(DEPTH):
        j = i // 2
        if i % 2 == 0:
            x, vlast = _sgu_layer(rows, source, norm_mix[i], sgu_w_in[j], sgu_b_in[j], sgu_ln_g[j],
                                  sgu_ln_b[j], sgu_w_s[j], sgu_b_s[j], sgu_w_out[j])
            sgu_v_p.append(vlast[:BATCH * CHUNK].reshape(BATCH, CHUNK, SGU_WIDTH))
            sgu_v_s.append(vlast[BATCH * CHUNK:].reshape(DEC_BATCH, 1, SGU_WIDTH))
            x = _ffn_layer(x, norm_ffn[i], ffn_w_gate[j], ffn_w_up[j], ffn_w_down[j])
        else:
            q, q_s, kv = _qkv_layer(x, norm_mix[i], attn_w_qkv[j])
            kv_p = kv[:N_PROMPT].reshape(BATCH, SEQ, 2 * KV_DIM)[:, SEQ - rows_p:]
            k_p.append(kv_p[..., :KV_DIM].reshape(BATCH, rows_p, N_KV_HEADS, HEAD_DIM))
            v_p.append(kv_p[..., KV_DIM:].reshape(BATCH, rows_p, N_KV_HEADS, HEAD_DIM))
            kv_s = kv[N_PROMPT:T_REAL]
            k_new = kv_s[:, None, :KV_DIM].reshape(DEC_BATCH, 1, N_KV_HEADS, HEAD_DIM)
            v_new = kv_s[:, None, KV_DIM:].reshape(DEC_BATCH, 1, N_KV_HEADS, HEAD_DIM)
            k_s.append(jnp.concatenate([cache_swa_k[j][:, 1:], k_new], axis=1))
            v_s.append(jnp.concatenate([cache_swa_v[j][:, 1:], v_new], axis=1))

            o, moe_wg, moe_wu, moe_wd = _swa_prompt(q, kv, attn_sinks[j].astype(F32), j,
                                                    moe_w_gate, moe_w_up, moe_w_down)
            o_s = _swa_sample(q_s, kv_s, cache_swa_k[j], cache_swa_v[j], attn_sinks[j])
            x, idx, gates, rank, counts = _proj_route(x, o, o_s, attn_w_o[j], norm_ffn[i],
                                                      moe_w_router[j])
            slots, tile_expert, n_used = _slot_plan(idx, rank, counts)
            y_slots = _moe_layer(x, norm_ffn[i], tile_expert, n_used, slots, moe_wg, moe_wu, moe_wd)
            rows, source = (x, gates, y_slots), "experts"
    y_prompt, y_sample = _final_combine(*rows, norm_final)
    y_prompt = y_prompt.reshape(BATCH, SEQ, D_MODEL)
    y_sample = y_sample.reshape(DEC_BATCH, 1, D_MODEL)
    return (y_prompt, y_sample, jnp.stack(sgu_v_p), jnp.stack(sgu_v_s),
            jnp.stack(k_p), jnp.stack(v_p), jnp.stack(k_s), jnp.stack(v_s))
```

```python
import functools

import numpy as np
import jax
import jax.numpy as jnp
from jax import lax
from jax.experimental import pallas as pl
from jax.experimental.pallas import tpu as pltpu

D_MODEL = 1024
BATCH = 4
SEQ = 4096
DEPTH = 4
DEC_BATCH = 128
PAST_LEN = 8192
CHUNK = 128
SGU_WIDTH = 2 * D_MODEL
SGU_GROUPS = 8
SGU_GROUP_DIM = SGU_WIDTH // SGU_GROUPS
WINDOW = 128
BLOCK = 128
HEAD_DIM = 64
N_HEADS = D_MODEL // HEAD_DIM
N_KV_HEADS = 2
GQA_GROUP = N_HEADS // N_KV_HEADS
Q_DIM = N_HEADS * HEAD_DIM
KV_DIM = N_KV_HEADS * HEAD_DIM
D_FF = 2816
N_EXPERTS = 8
D_FF_EXPERT = 3584
EPS = 1e-6
LN_EPS = 1e-5

F32 = jnp.float32
BF16 = jnp.bfloat16

LANES = 128
TM = 512
N_PROMPT = BATCH * SEQ
T_REAL = N_PROMPT + DEC_BATCH
N_TILES = -(-T_REAL // TM)
T_PAD = N_TILES * TM
PROMPT_TILES = N_PROMPT // TM
TILES_PER_SEQ = SEQ // TM
CHUNKS_PER_TILE = TM // CHUNK
N_PAIRS = N_HEADS // 2

TMM = 512
TF = 1792
NF = D_FF_EXPERT // TF
ROWS_PER_STEP = TMM // NF
N_SLOTS = 2 * T_REAL
NT_MOE = (N_SLOTS + N_EXPERTS * (TMM - 1)) // TMM + 1
P_TOTAL = NT_MOE * TMM
Y_ROWS = 2 * T_PAD + 2 * TMM
SLOT_ENTRIES = P_TOTAL + 2 * TMM

VMEM_LIMIT = 56 * 1024 * 1024

_SLOPES = [2.0 ** (-8.0 * (h + 1) / N_HEADS) for h in range(N_HEADS)]


def _rms(x, g):
    return x * lax.rsqrt(jnp.mean(x * x, axis=-1, keepdims=True) + EPS) * g


def _gelu(x):
    k = -2.0 * np.sqrt(2.0 / np.pi) * np.log2(np.e)
    t = (x * x) * np.float32(0.044715 * k) + np.float32(k)
    return x * (1.0 / (1.0 + jnp.exp2(x * t)))


def _silu(x):
    return x * (1.0 / (1.0 + jnp.exp(-x)))


def _dot(a, b):
    return jnp.dot(a, b, preferred_element_type=F32)


def _split(x):
    hi = x.astype(BF16)
    return hi, (x - hi.astype(F32)).astype(BF16)


def _operand(x, precise):
    return _split(x) if precise else x.astype(BF16)


def _mm(x, w_hi, w_lo):
    if not isinstance(x, tuple):
        return _dot(x, w_hi)
    x_hi, x_lo = x
    return _dot(x_hi, w_hi) + (_dot(x_lo, w_hi) + _dot(x_hi, w_lo))


def _hi_lo(w):
    hi = lax.optimization_barrier(w.astype(BF16))
    return hi, (w - hi.astype(F32)).astype(BF16)


def _dot_t(a, b):
    return lax.dot_general(a, b, (((1,), (1,)), ((), ())), preferred_element_type=F32)


SUBLANES = 8
ROW_CHUNKS = D_MODEL // LANES


def _chunk(s, rows):
    return pl.ds(s, rows, stride=SUBLANES)


def _store_row_tiles(ref, x):
    rows = x.shape[0]
    for s in range(ROW_CHUNKS):
        ref[_chunk(s, rows), :] = x[:, s * LANES:(s + 1) * LANES]


def _const_spec(shape):
    nd = len(shape)
    return pl.BlockSpec(shape, lambda *_: (0,) * nd, pipeline_mode=pl.Buffered(1))


def _params(n_axes=1):
    return pltpu.CompilerParams(dimension_semantics=("arbitrary",) * n_axes,
                                vmem_limit_bytes=VMEM_LIMIT)


def _combined_rows(x_ref, gate_ref, y0_ref, y1_ref, rows=TM):
    gate = gate_ref[:rows, :]
    g0, g1 = gate[:, 0:1], gate[:, 1:2]
    return jnp.concatenate(
        [x_ref[_chunk(s, rows), :] + (g0 * y0_ref[_chunk(s, rows), :] + g1 * y1_ref[_chunk(s, rows), :])
         for s in range(ROW_CHUNKS)], axis=1)


def _sgu_kernel(*refs, source):
    i = pl.program_id(0)
    is_sample = i == N_TILES - 1
    n_src = 2 if source == "inputs" else 4
    src, refs = refs[:n_src], refs[n_src:]
    (g_ref, win_hi, win_lo, bin_ref, lng_ref, lnb_ref, ws_ref, bs_ref, wout_hi, wout_lo,
     xo_ref, vlast_ref, ug_ref) = refs

    def mix(x, sample):
        h = _operand(_rms(x, g_ref[...]), sample)
        v = _gelu(_mm(h, win_hi[:, SGU_WIDTH:], win_lo[:, SGU_WIDTH:]) + bin_ref[:, SGU_WIDTH:])
        mu = jnp.mean(v, axis=-1, keepdims=True)
        vc = v - mu
        var = jnp.mean(vc * vc, axis=-1, keepdims=True)
        vn = vc * lax.rsqrt(var + LN_EPS) * lng_ref[...] + lnb_ref[...]
        vb = vn.astype(BF16)
        row = lax.broadcasted_iota(jnp.int32, (CHUNK, CHUNK), 0)
        col = lax.broadcasted_iota(jnp.int32, (CHUNK, CHUNK), 1)
        y = jnp.zeros_like(x)
        for g in range(SGU_GROUPS):
            lo, hi = g * SGU_GROUP_DIM, (g + 1) * SGU_GROUP_DIM
            u = _gelu(_mm(h, win_hi[:, lo:hi], win_lo[:, lo:hi]) + bin_ref[:, lo:hi])
            if sample:
                gate = vn[:, lo:hi] * ws_ref[g][0:1, 0:1] + bs_ref[g][0:1, 0:1]
                y = y + _mm(_split(u * gate), wout_hi[lo:hi, :], wout_lo[lo:hi, :])
            else:
                w_tril = jnp.where(row >= col, ws_ref[g], 0.0).astype(BF16)
                gate = jnp.concatenate(
                    [_dot(w_tril, vb[c * CHUNK:(c + 1) * CHUNK, lo:hi]) + bs_ref[g]
                     for c in range(x.shape[0] // CHUNK)], axis=0)
                ug_ref[:, lo:hi] = (u * gate).astype(BF16)
        if not sample:
            y = _dot(ug_ref[...], wout_hi[...])
        return x + y, vn

    @pl.when(jnp.logical_not(is_sample))
    def _():
        x = src[0][...] if source == "inputs" else _combined_rows(*src)
        x_new, vn = mix(x, sample=False)
        xo_ref[...] = x_new
        vlast_ref[...] = vn[TM - CHUNK:]

    @pl.when(is_sample)
    def _():
        x = src[1][...] if source == "inputs" else _combined_rows(*src, rows=DEC_BATCH)
        x_new, vn = mix(x, sample=True)
        xo_ref[:DEC_BATCH, :] = x_new
        xo_ref[DEC_BATCH:, :] = jnp.zeros((TM - DEC_BATCH, D_MODEL), F32)
        vlast_ref[...] = vn


def _row_tile_specs():
    return [pl.BlockSpec((TM * SUBLANES, LANES), lambda i: (i, 0)),
            pl.BlockSpec((TM, 2), lambda i: (i, 0)),
            pl.BlockSpec((TM * SUBLANES, LANES), lambda i: (i, 0)),
            pl.BlockSpec((TM * SUBLANES, LANES), lambda i: (N_TILES + i, 0))]


def _sgu_layer(rows, source, g, w_in, b_in, ln_g, ln_b, w_s, b_s, w_out):
    tile = pl.BlockSpec((TM, D_MODEL), lambda i: (i, 0))
    vlast_spec = pl.BlockSpec(
        (CHUNK, SGU_WIDTH),
        lambda i: (jnp.where(i == N_TILES - 1, BATCH, i // TILES_PER_SEQ), 0))
    if source == "inputs":
        row_specs = [pl.BlockSpec((TM, D_MODEL), lambda i: (jnp.minimum(i, PROMPT_TILES - 1), 0)),
                     pl.BlockSpec((DEC_BATCH, D_MODEL), lambda i: (0, 0))]
        row_args = rows
    else:
        row_specs = _row_tile_specs()
        x, gates, y_slots = rows
        row_args = (x, gates, y_slots, y_slots)
    return pl.pallas_call(
        functools.partial(_sgu_kernel, source=source),
        grid=(N_TILES,),
        in_specs=row_specs + [
                  _const_spec((1, D_MODEL)),
                  _const_spec((D_MODEL, 2 * SGU_WIDTH)),
                  _const_spec((D_MODEL, 2 * SGU_WIDTH)),
                  _const_spec((1, 2 * SGU_WIDTH)),
                  _const_spec((1, SGU_WIDTH)),
                  _const_spec((1, SGU_WIDTH)),
                  _const_spec((SGU_GROUPS, CHUNK, CHUNK)),
                  _const_spec((SGU_GROUPS, CHUNK, 1)),
                  _const_spec((SGU_WIDTH, D_MODEL)),
                  _const_spec((SGU_WIDTH, D_MODEL))],
        out_specs=[tile, vlast_spec],
        out_shape=[jax.ShapeDtypeStruct((T_PAD, D_MODEL), F32),
                   jax.ShapeDtypeStruct(((BATCH + 1) * CHUNK, SGU_WIDTH), F32)],
        scratch_shapes=[pltpu.VMEM((TM, SGU_WIDTH), BF16)],
        compiler_params=_params(),
        name="sgu_mixer",
    )(*row_args, g.reshape(1, -1), *_hi_lo(w_in), b_in.reshape(1, -1), ln_g.reshape(1, -1),
      ln_b.reshape(1, -1), w_s, b_s.reshape(SGU_GROUPS, CHUNK, 1), *_hi_lo(w_out))


def _ffn_kernel(x_ref, g_ref, wg_hi, wg_lo, wu_hi, wu_lo, wd_hi, wd_lo, xo_ref):
    is_sample = pl.program_id(0) == N_TILES - 1

    def ffn(x, sample):
        h = _operand(_rms(x, g_ref[...]), sample)
        a = _silu(_mm(h, wg_hi[...], wg_lo[...])) * _mm(h, wu_hi[...], wu_lo[...])
        return x + _mm(_operand(a, sample), wd_hi[...], wd_lo[...])

    @pl.when(jnp.logical_not(is_sample))
    def _():
        xo_ref[...] = ffn(x_ref[...], sample=False)

    @pl.when(is_sample)
    def _():
        xo_ref[:DEC_BATCH, :] = ffn(x_ref[:DEC_BATCH, :], sample=True)
        xo_ref[DEC_BATCH:, :] = jnp.zeros((TM - DEC_BATCH, D_MODEL), F32)


def _ffn_layer(x, g, w_gate, w_up, w_down):
    tile = pl.BlockSpec((TM, D_MODEL), lambda i: (i, 0))
    return pl.pallas_call(
        _ffn_kernel,
        grid=(N_TILES,),
        in_specs=[tile, _const_spec((1, D_MODEL))]
        + [_const_spec((D_MODEL, D_FF))] * 4 + [_const_spec((D_FF, D_MODEL))] * 2,
        out_specs=tile,
        out_shape=jax.ShapeDtypeStruct((T_PAD, D_MODEL), F32),
        compiler_params=_params(),
        name="dense_swiglu",
    )(x, g.reshape(1, -1), *_hi_lo(w_gate), *_hi_lo(w_up), *_hi_lo(w_down))


def _qkv_kernel(x_ref, g_ref, w_hi, w_lo, q_ref, qs_ref, kv_ref):
    is_sample = pl.program_id(0) == N_TILES - 1

    def qkv(x, sample):
        out = _mm(_operand(_rms(x, g_ref[...]), sample), w_hi[...], w_lo[...])
        return out[:, :Q_DIM] * (HEAD_DIM ** -0.5), out[:, Q_DIM:]

    @pl.when(jnp.logical_not(is_sample))
    def _():
        q, kv = qkv(x_ref[...], sample=False)
        q_ref[...] = q.astype(BF16)
        kv_ref[...] = kv

    @pl.when(is_sample)
    def _():
        q, kv = qkv(x_ref[:DEC_BATCH, :], sample=True)
        qs_ref[...] = q
        kv_ref[:DEC_BATCH, :] = kv
        kv_ref[DEC_BATCH:, :] = jnp.zeros((TM - DEC_BATCH, 2 * KV_DIM), F32)


def _qkv_layer(x, g, w_qkv):
    tile = pl.BlockSpec((TM, D_MODEL), lambda i: (i, 0))
    return pl.pallas_call(
        _qkv_kernel,
        grid=(N_TILES,),
        in_specs=[tile, _const_spec((1, D_MODEL)),
                  _const_spec((D_MODEL, Q_DIM + 2 * KV_DIM)), _const_spec((D_MODEL, Q_DIM + 2 * KV_DIM))],
        out_specs=[pl.BlockSpec((TM, Q_DIM), lambda i: (jnp.minimum(i, PROMPT_TILES - 1), 0)),
                   pl.BlockSpec((DEC_BATCH, Q_DIM), lambda i: (0, 0)),
                   pl.BlockSpec((TM, 2 * KV_DIM), lambda i: (i, 0))],
        out_shape=[jax.ShapeDtypeStruct((N_PROMPT, Q_DIM), BF16),
                   jax.ShapeDtypeStruct((DEC_BATCH, Q_DIM), F32),
                   jax.ShapeDtypeStruct((T_PAD, 2 * KV_DIM), F32)],
        compiler_params=_params(),
        name="swa_qkv",
    )(x, g.reshape(1, -1), *_hi_lo(w_qkv))


def _half_masks_f32(x):
    lane = lax.broadcasted_iota(jnp.int32, x.shape, 1)
    low = lane < HEAD_DIM
    xr = pltpu.roll(x, HEAD_DIM, 1)
    return ((jnp.where(low, x, 0.0), jnp.where(low, 0.0, xr)),
            (jnp.where(low, xr, 0.0), jnp.where(low, 0.0, x)))


def _half_masks(x):
    return tuple(tuple(m.astype(BF16) for m in pair) for pair in _half_masks_f32(x))


def _swa_prompt_kernel(sink_ref, q_ref, kv_ref, kvp_ref, wg_ref, wu_ref, wd_ref,
                       o_ref, wg_out, wu_out, wd_out, bias_ref):
    wg_out[...] = wg_ref[...].astype(BF16)
    wu_out[...] = wu_ref[...].astype(BF16)
    wd_out[...] = wd_ref[...].astype(BF16)
    i = pl.program_id(0)
    t = lax.broadcasted_iota(jnp.int32, (BLOCK, BLOCK), 0)
    c = lax.broadcasted_iota(jnp.int32, (BLOCK, BLOCK), 1)
    own = c <= t
    diag = c == t

    @pl.when(i == 0)
    def _():
        dist = jnp.where(own, t - c, BLOCK + t - c).astype(F32)
        for hd in range(N_HEADS):
            bias_ref[hd] = _SLOPES[hd] * dist

    first_tile = (i % TILES_PER_SEQ) == 0
    kv_all = jnp.concatenate([kvp_ref[...], kv_ref[...]], axis=0)
    k_blk, v_blk, vf_blk = [], [], []
    for b in range(CHUNKS_PER_TILE + 1):
        blk = kv_all[b * BLOCK:(b + 1) * BLOCK]
        k_blk.append(_half_masks(blk[:, :KV_DIM]))
        v_blk.append(_half_masks(blk[:, KV_DIM:]))
        vf_blk.append(_half_masks_f32(blk[:, KV_DIM:]))
    has_prev = (jnp.zeros((BLOCK, BLOCK), jnp.int32) + jnp.where(first_tile, 0, 1)) == 1

    for b in range(CHUNKS_PER_TILE):
        rows = slice(b * BLOCK, (b + 1) * BLOCK)
        for p in range(N_PAIRS):
            kvh = (2 * p) // GQA_GROUP
            qp = q_ref[rows, p * LANES:(p + 1) * LANES]
            acc = None
            for par in range(2):
                hd = 2 * p + par
                sink = sink_ref[hd]
                keys = jnp.concatenate([k_blk[b][kvh][par], k_blk[b + 1][kvh][par]], axis=0)
                both = _dot_t(qp, keys)
                l_prev, l_own = both[:, :BLOCK], both[:, BLOCK:]
                if b == 0:
                    l_prev = jnp.where(has_prev, l_prev, -jnp.inf)
                logits = jnp.where(own, l_own, l_prev) - bias_ref[hd]
                extra = (jnp.sum(jnp.where(diag, l_prev, 0.0), axis=-1, keepdims=True)
                         - _SLOPES[hd] * BLOCK)
                m = jnp.maximum(jnp.maximum(jnp.max(logits, axis=-1, keepdims=True), extra), sink)
                e = jnp.exp(logits - m)
                e_extra = jnp.exp(extra - m)
                denom = jnp.sum(e, axis=-1, keepdims=True) + e_extra + jnp.exp(sink - m)
                probs = jnp.concatenate([jnp.where(own, 0.0, e).astype(BF16),
                                         jnp.where(own, e, 0.0).astype(BF16)], axis=1)
                vals = jnp.concatenate([v_blk[b][kvh][par], v_blk[b + 1][kvh][par]], axis=0)
                part = (_dot(probs, vals) + e_extra * vf_blk[b][kvh][par]) * (1.0 / denom)
                acc = part if acc is None else acc + part
            o_ref[rows, p * LANES:(p + 1) * LANES] = acc.astype(BF16)


def _swa_prompt(q, kv, sinks, layer, w_gate, w_up, w_down):
    n_layers = w_gate.shape[0]
    up_rows = N_EXPERTS * D_MODEL // PROMPT_TILES
    down_rows = N_EXPERTS * D_FF_EXPERT // PROMPT_TILES
    up_spec = pl.BlockSpec((None, up_rows, D_FF_EXPERT), lambda i: (layer, i, 0))
    down_spec = pl.BlockSpec((None, down_rows, D_MODEL), lambda i: (layer, i, 0))
    o, wg, wu, wd = pl.pallas_call(
        _swa_prompt_kernel,
        grid=(PROMPT_TILES,),
        in_specs=[pl.BlockSpec(memory_space=pltpu.SMEM),
                  pl.BlockSpec((TM, Q_DIM), lambda i: (i, 0)),
                  pl.BlockSpec((TM, 2 * KV_DIM), lambda i: (i, 0)),
                  pl.BlockSpec((BLOCK, 2 * KV_DIM),
                               lambda i: (jnp.maximum(i * CHUNKS_PER_TILE - 1, 0), 0)),
                  up_spec, up_spec, down_spec],
        out_specs=[pl.BlockSpec((TM, Q_DIM), lambda i: (i, 0)),
                   pl.BlockSpec((up_rows, D_FF_EXPERT), lambda i: (i, 0)),
                   pl.BlockSpec((up_rows, D_FF_EXPERT), lambda i: (i, 0)),
                   pl.BlockSpec((down_rows, D_MODEL), lambda i: (i, 0))],
        out_shape=[jax.ShapeDtypeStruct((N_PROMPT, Q_DIM), BF16),
                   jax.ShapeDtypeStruct((N_EXPERTS * D_MODEL, D_FF_EXPERT), BF16),
                   jax.ShapeDtypeStruct((N_EXPERTS * D_MODEL, D_FF_EXPERT), BF16),
                   jax.ShapeDtypeStruct((N_EXPERTS * D_FF_EXPERT, D_MODEL), BF16)],
        scratch_shapes=[pltpu.VMEM((N_HEADS, BLOCK, BLOCK), F32)],
        compiler_params=_params(),
        name="swa_prompt",
    )(sinks, q, kv, kv,
      w_gate.reshape(n_layers, N_EXPERTS * D_MODEL, D_FF_EXPERT),
      w_up.reshape(n_layers, N_EXPERTS * D_MODEL, D_FF_EXPERT),
      w_down.reshape(n_layers, N_EXPERTS * D_FF_EXPERT, D_MODEL))
    return (o, wg.reshape(N_EXPERTS, D_MODEL, D_FF_EXPERT), wu.reshape(N_EXPERTS, D_MODEL, D_FF_EXPERT),
            wd.reshape(N_EXPERTS, D_FF_EXPERT, D_MODEL))


SAMPLE_TILE = 32


def _swa_sample_kernel(q_ref, kvn_ref, ck_ref, cv_ref, slope_ref, sink_ref, o_ref):
    shape = (SAMPLE_TILE, N_PAIRS, LANES)
    lane = lax.broadcasted_iota(jnp.int32, shape, 2)
    pair = lax.broadcasted_iota(jnp.int32, shape, 1)
    low = lane < HEAD_DIM
    kv0 = pair < (N_PAIRS // 2)

    def swap(x):
        return pltpu.roll(x, HEAD_DIM, 2)

    def three_pass(dims, a, b):
        dot = lambda u, v: lax.dot_general(u, v, dims, preferred_element_type=F32)
        return dot(a[0], b[0]) + (dot(a[1], b[0]) + dot(a[0], b[1]))

    qk_dims = (((2,), (2,)), ((0,), (0,)))
    pv_dims = (((2,), (1,)), ((0,), (0,)))

    q = q_ref[...]
    q_even = jnp.where(low, q, 0.0)
    q_odd = jnp.where(low, 0.0, q)
    q_al = (jnp.where(kv0, q_even, swap(q_even)), jnp.where(kv0, swap(q_odd), q_odd))

    ck = _split(ck_ref[...])
    cv = _split(cv_ref[...])
    kvn = kvn_ref[...]
    k_new = kvn[:, :, :KV_DIM]
    v_new = kvn[:, :, KV_DIM:]
    r = lax.broadcasted_iota(jnp.int32, (SAMPLE_TILE, N_PAIRS, WINDOW), 2)
    dist = (WINDOW - r).astype(F32)

    outs = []
    for par in range(2):
        qa = q_al[par]
        slope = slope_ref[par]
        sink = sink_ref[par]
        logits = three_pass(qk_dims, _split(qa), ck) - slope * dist
        l_self = jnp.sum(qa * k_new, axis=-1, keepdims=True)
        m = jnp.maximum(jnp.maximum(jnp.max(logits, axis=-1, keepdims=True), l_self), sink)
        e = jnp.exp(logits - m)
        e_self = jnp.exp(l_self - m)
        inv = 1.0 / (jnp.sum(e, axis=-1, keepdims=True) + e_self + jnp.exp(sink - m))
        o = three_pass(pv_dims, _split(e * inv), cv) + (e_self * inv) * v_new
        outs.append(o)
    o_even = jnp.where(kv0, outs[0], swap(outs[0]))
    o_odd = jnp.where(kv0, swap(outs[1]), outs[1])
    o_ref[...] = jnp.where(low, o_even, o_odd)


def _swa_sample(q_s, kv_s, cache_k, cache_v, sinks):
    rows = cache_k.shape[1]
    slopes = np.asarray(_SLOPES, np.float32).reshape(N_PAIRS, 2).T.reshape(2, N_PAIRS, 1)
    sink_arr = sinks.astype(F32).reshape(N_PAIRS, 2).T.reshape(2, N_PAIRS, 1)
    blk = lambda *shape: pl.BlockSpec((SAMPLE_TILE,) + shape, lambda i: (i,) + (0,) * len(shape))
    o3 = pl.pallas_call(
        _swa_sample_kernel,
        grid=(DEC_BATCH // SAMPLE_TILE,),
        in_specs=[blk(N_PAIRS, LANES), blk(1, 2 * KV_DIM), blk(rows, KV_DIM), blk(rows, KV_DIM),
                  _const_spec((2, N_PAIRS, 1)), _const_spec((2, N_PAIRS, 1))],
        out_specs=blk(N_PAIRS, LANES),
        out_shape=jax.ShapeDtypeStruct((DEC_BATCH, N_PAIRS, LANES), F32),
        compiler_params=_params(),
        name="swa_sample",
    )(q_s.reshape(DEC_BATCH, N_PAIRS, LANES), kv_s.reshape(DEC_BATCH, 1, 2 * KV_DIM),
      cache_k.reshape(DEC_BATCH, rows, KV_DIM), cache_v.reshape(DEC_BATCH, rows, KV_DIM),
      jnp.asarray(slopes), sink_arr)
    return o3.reshape(DEC_BATCH, Q_DIM)


def _proj_route_kernel(x_ref, op_ref, os_ref, wo_hi, wo_lo, g_ref, wr_hi, wr_lo, xo_ref, idx_ref,
                       gate_ref, rank_ref, cnt_ref, carry_ref, xnew_ref):
    i = pl.program_id(0)
    is_sample = i == N_TILES - 1

    @pl.when(i == 0)
    def _():
        carry_ref[...] = jnp.zeros_like(carry_ref)

    @pl.when(jnp.logical_not(is_sample))
    def _():
        xnew_ref[...] = x_ref[...] + _dot(op_ref[...], wo_hi[...])

    @pl.when(is_sample)
    def _():
        xnew_ref[:DEC_BATCH, :] = x_ref[:DEC_BATCH, :] + _mm(_split(os_ref[...]), wo_hi[...], wo_lo[...])
        xnew_ref[DEC_BATCH:, :] = jnp.zeros((TM - DEC_BATCH, D_MODEL), F32)

    x = xnew_ref[...]
    _store_row_tiles(xo_ref, x)
    h = _rms(x, g_ref[...])
    logits = _mm(_split(h), wr_hi[...], wr_lo[...])
    lane = lax.broadcasted_iota(jnp.int32, (TM, LANES), 1)
    logits = jnp.where(lane < N_EXPERTS, logits, -jnp.inf)
    m0 = jnp.max(logits, axis=-1, keepdims=True)
    i0 = jnp.min(jnp.where(logits == m0, lane, LANES), axis=-1, keepdims=True)
    rest = jnp.where(lane == i0, -jnp.inf, logits)
    m1 = jnp.max(rest, axis=-1, keepdims=True)
    i1 = jnp.min(jnp.where(rest == m1, lane, LANES), axis=-1, keepdims=True)
    e1 = jnp.exp(m1 - m0)
    g0 = 1.0 / (1.0 + e1)
    g1 = e1 * g0
    idx_ref[...] = jnp.concatenate([i0, i1], axis=1)
    gate_ref[...] = jnp.concatenate([g0, g1], axis=1)

    row_id = i * TM + lax.broadcasted_iota(jnp.int32, (TM, 1), 0)
    onehot = jnp.where(((lane == i0) | (lane == i1)) & (row_id < T_REAL), 1.0, 0.0)
    r = lax.broadcasted_iota(jnp.int32, (TM, TM), 0)
    c = lax.broadcasted_iota(jnp.int32, (TM, TM), 1)
    before = jnp.where(c < r, 1.0, 0.0).astype(BF16)
    ranks = _dot(before, onehot.astype(BF16)) + carry_ref[...]
    r0 = jnp.sum(jnp.where(lane == i0, ranks, 0.0), axis=-1, keepdims=True)
    r1 = jnp.sum(jnp.where(lane == i1, ranks, 0.0), axis=-1, keepdims=True)
    rank_ref[...] = jnp.concatenate([r0, r1], axis=1).astype(jnp.int32)
    carry_ref[...] = carry_ref[...] + jnp.sum(onehot, axis=0, keepdims=True)
    cnt_ref[...] = carry_ref[...].astype(jnp.int32)


def _proj_route(x, o_prompt, o_sample, w_o, g, w_router):
    tile = pl.BlockSpec((TM, D_MODEL), lambda i: (i, 0))
    pair = pl.BlockSpec((TM, 2), lambda i: (i, 0))
    wr = jnp.zeros((D_MODEL, LANES), F32).at[:, :N_EXPERTS].set(w_router)
    return pl.pallas_call(
        _proj_route_kernel,
        grid=(N_TILES,),
        in_specs=[tile,
                  pl.BlockSpec((TM, Q_DIM), lambda i: (jnp.minimum(i, PROMPT_TILES - 1), 0)),
                  pl.BlockSpec((DEC_BATCH, Q_DIM), lambda i: (0, 0)),
                  _const_spec((Q_DIM, D_MODEL)), _const_spec((Q_DIM, D_MODEL)),
                  _const_spec((1, D_MODEL)), _const_spec((D_MODEL, LANES)),
                  _const_spec((D_MODEL, LANES))],
        out_specs=[pl.BlockSpec((TM * SUBLANES, LANES), lambda i: (i, 0)), pair, pair, pair,
                   pl.BlockSpec((1, LANES), lambda i: (0, 0))],
        out_shape=[jax.ShapeDtypeStruct((T_PAD * SUBLANES, LANES), F32),
                   jax.ShapeDtypeStruct((T_PAD, 2), jnp.int32),
                   jax.ShapeDtypeStruct((T_PAD, 2), F32),
                   jax.ShapeDtypeStruct((T_PAD, 2), jnp.int32),
                   jax.ShapeDtypeStruct((1, LANES), jnp.int32)],
        scratch_shapes=[pltpu.VMEM((1, LANES), F32), pltpu.VMEM((TM, D_MODEL), F32)],
        compiler_params=_params(),
        name="proj_route",
    )(x, o_prompt, o_sample, *_hi_lo(w_o), g.reshape(1, -1), *_hi_lo(wr))


def _tile_at(start):
    return pl.ds(start if isinstance(start, int) else pl.multiple_of(start, SUBLANES), SUBLANES)


def _gather_row(x_hbm, xbuf, sem, slot, src, r):
    return pltpu.make_async_copy(x_hbm.at[_tile_at(src), :], xbuf.at[slot, _tile_at(r * SUBLANES), :],
                                 sem.at[slot])


def _scatter_row(obuf, y_hbm, sem, slot, r, dst):
    return pltpu.make_async_copy(obuf.at[slot, _tile_at(r * SUBLANES), :], y_hbm.at[_tile_at(dst), :],
                                 sem.at[slot])


def _gather_wait(x_hbm, xbuf, sem, slot):
    pltpu.make_async_copy(x_hbm.at[pl.ds(0, TMM * SUBLANES), :], xbuf.at[slot], sem.at[slot]).wait()


def _scatter_wait(obuf, y_hbm, sem, slot):
    pltpu.make_async_copy(obuf.at[slot], y_hbm.at[pl.ds(0, TMM * SUBLANES), :], sem.at[slot]).wait()


def _slot_src(slot_ref, q):
    return slot_ref[q]


def _slot_dst(slot_ref, q):
    return slot_ref[SLOT_ENTRIES + q]


def _moe_kernel(te_ref, nu_ref, slot_ref, x_hbm, g_ref, wg_ref, wu_ref, wd_ref, y_hbm,
                xbuf, hbuf, acc, obuf, gsem, ssem):
    i = pl.program_id(0)
    f = pl.program_id(1)
    n_used = nu_ref[0]
    slot = i % 2
    other = 1 - slot

    @pl.when(i < n_used)
    def _():
        @pl.when((i == 0) & (f == 0))
        def _():
            def start(r, carry):
                _gather_row(x_hbm, xbuf, gsem, 0, _slot_src(slot_ref, TMM + r), r).start()
                return carry
            lax.fori_loop(0, TMM, start, 0)
            obuf[...] = jnp.zeros_like(obuf)
            for s in range(2):
                spare = pltpu.make_async_copy(
                    obuf.at[s],
                    y_hbm.at[pl.ds((2 * T_PAD + s * TMM) * SUBLANES, TMM * SUBLANES), :],
                    ssem.at[s])
                spare.start()
                spare.wait()
                n_pad = (T_PAD - T_REAL) * SUBLANES
                pad = pltpu.make_async_copy(
                    obuf.at[s, pl.ds(0, n_pad), :],
                    y_hbm.at[pl.ds((s * T_PAD + T_REAL) * SUBLANES, n_pad), :],
                    ssem.at[s])
                pad.start()
                pad.wait()

        @pl.when(f == 0)
        def _():
            _gather_wait(x_hbm, xbuf, gsem, slot)
            xin = xbuf.at[slot]
            chunks = [xin[_chunk(s, TMM), :] for s in range(ROW_CHUNKS)]
            ssq = chunks[0] * chunks[0]
            for c in chunks[1:]:
                ssq = ssq + c * c
            scale = lax.rsqrt(jnp.sum(ssq, axis=-1, keepdims=True) * (1.0 / D_MODEL) + EPS)
            for s, c in enumerate(chunks):
                cols = slice(s * LANES, (s + 1) * LANES)
                hbuf[:, cols] = (c * scale * g_ref[:, cols]).astype(BF16)
            acc[...] = jnp.zeros_like(acc)

        h = hbuf[...]
        a = (_silu(_dot(h, wg_ref[...])) * _dot(h, wu_ref[...])).astype(BF16)
        acc[...] += _dot(a, wd_ref[...])

        row0 = f * ROWS_PER_STEP
        nxt = (i + 2) * TMM + row0
        prv = i * TMM + row0
        for r in range(ROWS_PER_STEP):
            _gather_row(x_hbm, xbuf, gsem, other, _slot_src(slot_ref, nxt + r), row0 + r).start()
            _scatter_row(obuf, y_hbm, ssem, other, row0 + r, _slot_dst(slot_ref, prv + r)).start()

        @pl.when(f == NF - 1)
        def _():
            @pl.when(i >= 1)
            def _():
                _scatter_wait(obuf, y_hbm, ssem, slot)
            _store_row_tiles(obuf.at[slot], acc[...])

        @pl.when((f == NF - 1) & (i == n_used - 1))
        def _():
            _gather_wait(x_hbm, xbuf, gsem, other)
            _scatter_wait(obuf, y_hbm, ssem, other)
            base = (i + 1) * TMM

            def start(r, carry):
                _scatter_row(obuf, y_hbm, ssem, slot, r, _slot_dst(slot_ref, base + r)).start()
                return carry
            lax.fori_loop(0, TMM, start, 0)
            _scatter_wait(obuf, y_hbm, ssem, slot)


def _moe_layer(x, g, tile_expert, n_used, slots, w_gate, w_up, w_down):
    def f_eff(i, f, nu):
        return jnp.where(i < nu[0], f, NF - 1)

    grid_spec = pltpu.PrefetchScalarGridSpec(
        num_scalar_prefetch=3,
        grid=(NT_MOE, NF),
        in_specs=[pl.BlockSpec(memory_space=pl.ANY),
                  pl.BlockSpec((1, D_MODEL), lambda i, f, te, nu, tok: (0, 0)),
                  pl.BlockSpec((None, D_MODEL, TF),
                               lambda i, f, te, nu, tok: (te[i], 0, f_eff(i, f, nu))),
                  pl.BlockSpec((None, D_MODEL, TF),
                               lambda i, f, te, nu, tok: (te[i], 0, f_eff(i, f, nu))),
                  pl.BlockSpec((None, TF, D_MODEL),
                               lambda i, f, te, nu, tok: (te[i], f_eff(i, f, nu), 0))],
        out_specs=pl.BlockSpec(memory_space=pl.ANY),
        scratch_shapes=[pltpu.VMEM((2, TMM * SUBLANES, LANES), F32),
                        pltpu.VMEM((TMM, D_MODEL), BF16),
                        pltpu.VMEM((TMM, D_MODEL), F32),
                        pltpu.VMEM((2, TMM * SUBLANES, LANES), F32),
                        pltpu.SemaphoreType.DMA((2,)),
                        pltpu.SemaphoreType.DMA((2,))],
    )
    return pl.pallas_call(
        _moe_kernel,
        grid_spec=grid_spec,
        out_shape=jax.ShapeDtypeStruct((Y_ROWS * SUBLANES, LANES), F32),
        compiler_params=_params(2),
        name="moe_experts",
    )(tile_expert, n_used, slots, x, g.reshape(1, -1), w_gate, w_up, w_down)


def _final_kernel(x_ref, gate_ref, y0_ref, y1_ref, gf_ref, yp_ref, ys_ref):
    i = pl.program_id(0)
    y = _rms(_combined_rows(x_ref, gate_ref, y0_ref, y1_ref), gf_ref[...])

    @pl.when(i < PROMPT_TILES)
    def _():
        yp_ref[...] = y

    @pl.when(i == N_TILES - 1)
    def _():
        ys_ref[...] = y[:DEC_BATCH]


def _final_combine(x, gates, y_slots, g_final):
    return pl.pallas_call(
        _final_kernel,
        grid=(N_TILES,),
        in_specs=_row_tile_specs() + [_const_spec((1, D_MODEL))],
        out_specs=[pl.BlockSpec((TM, D_MODEL), lambda i: (jnp.minimum(i, PROMPT_TILES - 1), 0)),
                   pl.BlockSpec((DEC_BATCH, D_MODEL), lambda i: (0, 0))],
        out_shape=[jax.ShapeDtypeStruct((N_PROMPT, D_MODEL), F32),
                   jax.ShapeDtypeStruct((DEC_BATCH, D_MODEL), F32)],
        compiler_params=_params(),
        name="final_combine",
    )(x, gates, y_slots, y_slots, g_final.reshape(1, -1))


def _padding_slot_entries():
    p = np.arange(-TMM, P_TOTAL + TMM)
    spare_row = ((p // TMM) % 2) * TMM + p % TMM
    return np.concatenate([spare_row, 2 * T_PAD + spare_row]).astype(np.int32) * SUBLANES


FILL_UNROLL = 8


def _slot_fill_kernel(pos_ref, init_hbm, slots_hbm, buf, sem):
    load = pltpu.make_async_copy(init_hbm, buf, sem)
    load.start()
    load.wait()

    def body(j, carry):
        for u in range(FILL_UNROLL):
            t = j * FILL_UNROLL + u
            for k in range(2):
                q = TMM + pos_ref[2 * t + k]
                buf[q] = t * SUBLANES
                buf[SLOT_ENTRIES + q] = (k * T_PAD + t) * SUBLANES
        return carry

    lax.fori_loop(0, T_REAL // FILL_UNROLL, body, 0)
    store = pltpu.make_async_copy(buf, slots_hbm, sem)
    store.start()
    store.wait()


def _slot_fill(pos):
    return pl.pallas_call(
        _slot_fill_kernel,
        in_specs=[pl.BlockSpec(memory_space=pltpu.SMEM), pl.BlockSpec(memory_space=pl.ANY)],
        out_specs=pl.BlockSpec(memory_space=pl.ANY),
        out_shape=jax.ShapeDtypeStruct((2 * SLOT_ENTRIES,), jnp.int32),
        scratch_shapes=[pltpu.SMEM((2 * SLOT_ENTRIES,), jnp.int32), pltpu.SemaphoreType.DMA(())],
        name="slot_fill",
    )(pos.reshape(-1), jnp.asarray(_padding_slot_entries()))


def _slot_plan(idx, rank, counts):
    cnt = counts[0, :N_EXPERTS]
    tiles = (cnt + TMM - 1) // TMM
    tile_end = jnp.cumsum(tiles)
    start = (tile_end - tiles) * TMM
    n_used = tile_end[-1:].astype(jnp.int32)
    pos = (start[idx] + rank).astype(jnp.int32)
    slots = _slot_fill(pos)
    tile_ids = jnp.arange(NT_MOE, dtype=jnp.int32)
    tile_expert = jnp.minimum(jnp.sum(tile_end[None, :] <= tile_ids[:, None], axis=1),
                              N_EXPERTS - 1).astype(jnp.int32)
    last_expert = tile_expert[jnp.maximum(n_used[0] - 1, 0)]
    tile_expert = jnp.where(tile_ids < n_used[0], tile_expert, last_expert)
    return slots, tile_expert, n_used


def kernel(x_prompt, x_sample, cache_swa_k, cache_swa_v, norm_mix, norm_ffn, norm_final,
           sgu_w_in, sgu_b_in, sgu_ln_g, sgu_ln_b, sgu_w_s, sgu_b_s, sgu_w_out,
           attn_w_qkv, attn_sinks, attn_w_o,
           ffn_w_gate, ffn_w_up, ffn_w_down,
           moe_w_router, moe_w_gate, moe_w_up, moe_w_down):
    sgu_v_p, sgu_v_s, k_p, v_p, k_s, v_s = [], [], [], [], [], []
    rows_p = min(WINDOW, SEQ)
    rows, source = (x_prompt.reshape(N_PROMPT, D_MODEL), x_sample.reshape(DEC_BATCH, D_MODEL)), "inputs"
    for i in range(DEPTH):
        j = i // 2
        if i % 2 == 0:
            x, vlast = _sgu_layer(rows, source, norm_mix[i], sgu_w_in[j], sgu_b_in[j], sgu_ln_g[j],
                                  sgu_ln_b[j], sgu_w_s[j], sgu_b_s[j], sgu_w_out[j])
            sgu_v_p.append(vlast[:BATCH * CHUNK].reshape(BATCH, CHUNK, SGU_WIDTH))
            sgu_v_s.append(vlast[BATCH * CHUNK:].reshape(DEC_BATCH, 1, SGU_WIDTH))
            x = _ffn_layer(x, norm_ffn[i], ffn_w_gate[j], ffn_w_up[j], ffn_w_down[j])
        else:
            q, q_s, kv = _qkv_layer(x, norm_mix[i], attn_w_qkv[j])
            kv_p = kv[:N_PROMPT].reshape(BATCH, SEQ, 2 * KV_DIM)[:, SEQ - rows_p:]
            k_p.append(kv_p[..., :KV_DIM].reshape(BATCH, rows_p, N_KV_HEADS, HEAD_DIM))
            v_p.append(kv_p[..., KV_DIM:].reshape(BATCH, rows_p, N_KV_HEADS, HEAD_DIM))
            kv_s = kv[N_PROMPT:T_REAL]
            k_new = kv_s[:, None, :KV_DIM].reshape(DEC_BATCH, 1, N_KV_HEADS, HEAD_DIM)
            v_new = kv_s[:, None, KV_DIM:].reshape(DEC_BATCH, 1, N_KV_HEADS, HEAD_DIM)
            k_s.append(jnp.concatenate([cache_swa_k[j][:, 1:], k_new], axis=1))
            v_s.append(jnp.concatenate([cache_swa_v[j][:, 1:], v_new], axis=1))

            o, moe_wg, moe_wu, moe_wd = _swa_prompt(q, kv, attn_sinks[j].astype(F32), j,
                                                    moe_w_gate, moe_w_up, moe_w_down)
            o_s = _swa_sample(q_s, kv_s, cache_swa_k[j], cache_swa_v[j], attn_sinks[j])
            x, idx, gates, rank, counts = _proj_route(x, o, o_s, attn_w_o[j], norm_ffn[i],
                                                      moe_w_router[j])
            slots, tile_expert, n_used = _slot_plan(idx, rank, counts)
            y_slots = _moe_layer(x, norm_ffn[i], tile_expert, n_used, slots, moe_wg, moe_wu, moe_wd)
            rows, source = (x, gates, y_slots), "experts"
    y_prompt, y_sample = _final_combine(*rows, norm_final)
    y_prompt = y_prompt.reshape(BATCH, SEQ, D_MODEL)
    y_sample = y_sample.reshape(DEC_BATCH, 1, D_MODEL)
    return (y_prompt, y_sample, jnp.stack(sgu_v_p), jnp.stack(sgu_v_s),
            jnp.stack(k_p), jnp.stack(v_p), jnp.stack(k_s), jnp.stack(v_s))
```

```python
import functools

import numpy as np
import jax
import jax.numpy as jnp
from jax import lax
from jax.experimental import pallas as pl
from jax.experimental.pallas import tpu as pltpu

D_MODEL = 1024
BATCH = 4
SEQ = 4096
DEPTH = 4
DEC_BATCH = 128
PAST_LEN = 8192
CHUNK = 128
SGU_WIDTH = 2 * D_MODEL
SGU_GROUPS = 8
SGU_GROUP_DIM = SGU_WIDTH // SGU_GROUPS
WINDOW = 128
BLOCK = 128
HEAD_DIM = 64
N_HEADS = D_MODEL // HEAD_DIM
N_KV_HEADS = 2
GQA_GROUP = N_HEADS // N_KV_HEADS
Q_DIM = N_HEADS * HEAD_DIM
KV_DIM = N_KV_HEADS * HEAD_DIM
D_FF = 2816
N_EXPERTS = 8
D_FF_EXPERT = 3584
EPS = 1e-6
LN_EPS = 1e-5

F32 = jnp.float32
BF16 = jnp.bfloat16

LANES = 128
TM = 512
N_PROMPT = BATCH * SEQ
T_REAL = N_PROMPT + DEC_BATCH
N_TILES = -(-T_REAL // TM)
T_PAD = N_TILES * TM
PROMPT_TILES = N_PROMPT // TM
TILES_PER_SEQ = SEQ // TM
CHUNKS_PER_TILE = TM // CHUNK
N_PAIRS = N_HEADS // 2

TMM = 512
N_SLOTS = 2 * T_REAL
NT_MOE = (N_SLOTS + N_EXPERTS * (TMM - 1)) // TMM + 1
P_TOTAL = NT_MOE * TMM
Y_ROWS = 2 * T_PAD + 2 * TMM
SLOT_ENTRIES = P_TOTAL + 2 * TMM

VMEM_LIMIT = 56 * 1024 * 1024

_SLOPES = [2.0 ** (-8.0 * (h + 1) / N_HEADS) for h in range(N_HEADS)]


def _rms(x, g):
    return x * lax.rsqrt(jnp.mean(x * x, axis=-1, keepdims=True) + EPS) * g


def _gelu(x):
    k = -2.0 * np.sqrt(2.0 / np.pi) * np.log2(np.e)
    t = (x * x) * np.float32(0.044715 * k) + np.float32(k)
    return x * (1.0 / (1.0 + jnp.exp2(x * t)))


def _silu(x):
    return x * (1.0 / (1.0 + jnp.exp(-x)))


def _dot(a, b):
    return jnp.dot(a, b, preferred_element_type=F32)


def _split(x):
    hi = x.astype(BF16)
    return hi, (x - hi.astype(F32)).astype(BF16)


def _operand(x, precise):
    return _split(x) if precise else x.astype(BF16)


def _mm(x, w_hi, w_lo):
    if not isinstance(x, tuple):
        return _dot(x, w_hi)
    x_hi, x_lo = x
    return _dot(x_hi, w_hi) + (_dot(x_lo, w_hi) + _dot(x_hi, w_lo))


def _hi_lo(w):
    hi = lax.optimization_barrier(w.astype(BF16))
    return hi, (w - hi.astype(F32)).astype(BF16)


def _dot_t(a, b):
    return lax.dot_general(a, b, (((1,), (1,)), ((), ())), preferred_element_type=F32)


SUBLANES = 8
ROW_CHUNKS = D_MODEL // LANES


def _chunk(s, rows):
    return pl.ds(s, rows, stride=SUBLANES)


def _store_row_tiles(ref, x):
    rows = x.shape[0]
    for s in range(ROW_CHUNKS):
        ref[_chunk(s, rows), :] = x[:, s * LANES:(s + 1) * LANES]


def _const_spec(shape):
    nd = len(shape)
    return pl.BlockSpec(shape, lambda *_: (0,) * nd, pipeline_mode=pl.Buffered(1))


def _params(n_axes=1):
    return pltpu.CompilerParams(dimension_semantics=("arbitrary",) * n_axes,
                                vmem_limit_bytes=VMEM_LIMIT)


def _combined_rows(x_ref, gate_ref, y0_ref, y1_ref, rows=TM):
    gate = gate_ref[:rows, :]
    g0, g1 = gate[:, 0:1], gate[:, 1:2]
    return jnp.concatenate(
        [x_ref[_chunk(s, rows), :] + (g0 * y0_ref[_chunk(s, rows), :] + g1 * y1_ref[_chunk(s, rows), :])
         for s in range(ROW_CHUNKS)], axis=1)


def _sgu_kernel(*refs, source):
    i = pl.program_id(0)
    is_sample = i == N_TILES - 1
    n_src = 2 if source == "inputs" else 4
    src, refs = refs[:n_src], refs[n_src:]
    (g_ref, win_hi, win_lo, bin_ref, lng_ref, lnb_ref, ws_ref, bs_ref, wout_hi, wout_lo,
     xo_ref, vlast_ref, ug_ref) = refs

    def mix(x, sample):
        h = _operand(_rms(x, g_ref[...]), sample)
        v = _gelu(_mm(h, win_hi[:, SGU_WIDTH:], win_lo[:, SGU_WIDTH:]) + bin_ref[:, SGU_WIDTH:])
        mu = jnp.mean(v, axis=-1, keepdims=True)
        vc = v - mu
        var = jnp.mean(vc * vc, axis=-1, keepdims=True)
        vn = vc * lax.rsqrt(var + LN_EPS) * lng_ref[...] + lnb_ref[...]
        vb = vn.astype(BF16)
        row = lax.broadcasted_iota(jnp.int32, (CHUNK, CHUNK), 0)
        col = lax.broadcasted_iota(jnp.int32, (CHUNK, CHUNK), 1)
        y = jnp.zeros_like(x)
        for g in range(SGU_GROUPS):
            lo, hi = g * SGU_GROUP_DIM, (g + 1) * SGU_GROUP_DIM
            u = _gelu(_mm(h, win_hi[:, lo:hi], win_lo[:, lo:hi]) + bin_ref[:, lo:hi])
            if sample:
                gate = vn[:, lo:hi] * ws_ref[g][0:1, 0:1] + bs_ref[g][0:1, 0:1]
                y = y + _mm(_split(u * gate), wout_hi[lo:hi, :], wout_lo[lo:hi, :])
            else:
                w_tril = jnp.where(row >= col, ws_ref[g], 0.0).astype(BF16)
                gate = jnp.concatenate(
                    [_dot(w_tril, vb[c * CHUNK:(c + 1) * CHUNK, lo:hi]) + bs_ref[g]
                     for c in range(x.shape[0] // CHUNK)], axis=0)
                ug_ref[:, lo:hi] = (u * gate).astype(BF16)
        if not sample:
            y = _dot(ug_ref[...], wout_hi[...])
        return x + y, vn

    @pl.when(jnp.logical_not(is_sample))
    def _():
        x = src[0][...] if source == "inputs" else _combined_rows(*src)
        x_new, vn = mix(x, sample=False)
        xo_ref[...] = x_new
        vlast_ref[...] = vn[TM - CHUNK:]

    @pl.when(is_sample)
    def _():
        x = src[1][...] if source == "inputs" else _combined_rows(*src, rows=DEC_BATCH)
        x_new, vn = mix(x, sample=True)
        xo_ref[:DEC_BATCH, :] = x_new
        xo_ref[DEC_BATCH:, :] = jnp.zeros((TM - DEC_BATCH, D_MODEL), F32)
        vlast_ref[...] = vn


def _row_tile_specs():
    return [pl.BlockSpec((TM * SUBLANES, LANES), lambda i: (i, 0)),
            pl.BlockSpec((TM, 2), lambda i: (i, 0)),
            pl.BlockSpec((TM * SUBLANES, LANES), lambda i: (i, 0)),
            pl.BlockSpec((TM * SUBLANES, LANES), lambda i: (N_TILES + i, 0))]


def _sgu_layer(rows, source, g, w_in, b_in, ln_g, ln_b, w_s, b_s, w_out):
    tile = pl.BlockSpec((TM, D_MODEL), lambda i: (i, 0))
    vlast_spec = pl.BlockSpec(
        (CHUNK, SGU_WIDTH),
        lambda i: (jnp.where(i == N_TILES - 1, BATCH, i // TILES_PER_SEQ), 0))
    if source == "inputs":
        row_specs = [pl.BlockSpec((TM, D_MODEL), lambda i: (jnp.minimum(i, PROMPT_TILES - 1), 0)),
                     pl.BlockSpec((DEC_BATCH, D_MODEL), lambda i: (0, 0))]
        row_args = rows
    else:
        row_specs = _row_tile_specs()
        x, gates, y_slots = rows
        row_args = (x, gates, y_slots, y_slots)
    return pl.pallas_call(
        functools.partial(_sgu_kernel, source=source),
        grid=(N_TILES,),
        in_specs=row_specs + [
                  _const_spec((1, D_MODEL)),
                  _const_spec((D_MODEL, 2 * SGU_WIDTH)),
                  _const_spec((D_MODEL, 2 * SGU_WIDTH)),
                  _const_spec((1, 2 * SGU_WIDTH)),
                  _const_spec((1, SGU_WIDTH)),
                  _const_spec((1, SGU_WIDTH)),
                  _const_spec((SGU_GROUPS, CHUNK, CHUNK)),
                  _const_spec((SGU_GROUPS, CHUNK, 1)),
                  _const_spec((SGU_WIDTH, D_MODEL)),
                  _const_spec((SGU_WIDTH, D_MODEL))],
        out_specs=[tile, vlast_spec],
        out_shape=[jax.ShapeDtypeStruct((T_PAD, D_MODEL), F32),
                   jax.ShapeDtypeStruct(((BATCH + 1) * CHUNK, SGU_WIDTH), F32)],
        scratch_shapes=[pltpu.VMEM((TM, SGU_WIDTH), BF16)],
        compiler_params=_params(),
        name="sgu_mixer",
    )(*row_args, g.reshape(1, -1), *_hi_lo(w_in), b_in.reshape(1, -1), ln_g.reshape(1, -1),
      ln_b.reshape(1, -1), w_s, b_s.reshape(SGU_GROUPS, CHUNK, 1), *_hi_lo(w_out))


def _ffn_kernel(x_ref, g_ref, wg_hi, wg_lo, wu_hi, wu_lo, wd_hi, wd_lo, xo_ref):
    is_sample = pl.program_id(0) == N_TILES - 1

    def ffn(x, sample):
        h = _operand(_rms(x, g_ref[...]), sample)
        a = _silu(_mm(h, wg_hi[...], wg_lo[...])) * _mm(h, wu_hi[...], wu_lo[...])
        return x + _mm(_operand(a, sample), wd_hi[...], wd_lo[...])

    @pl.when(jnp.logical_not(is_sample))
    def _():
        xo_ref[...] = ffn(x_ref[...], sample=False)

    @pl.when(is_sample)
    def _():
        xo_ref[:DEC_BATCH, :] = ffn(x_ref[:DEC_BATCH, :], sample=True)
        xo_ref[DEC_BATCH:, :] = jnp.zeros((TM - DEC_BATCH, D_MODEL), F32)


def _ffn_layer(x, g, w_gate, w_up, w_down):
    tile = pl.BlockSpec((TM, D_MODEL), lambda i: (i, 0))
    return pl.pallas_call(
        _ffn_kernel,
        grid=(N_TILES,),
        in_specs=[tile, _const_spec((1, D_MODEL))]
        + [_const_spec((D_MODEL, D_FF))] * 4 + [_const_spec((D_FF, D_MODEL))] * 2,
        out_specs=tile,
        out_shape=jax.ShapeDtypeStruct((T_PAD, D_MODEL), F32),
        compiler_params=_params(),
        name="dense_swiglu",
    )(x, g.reshape(1, -1), *_hi_lo(w_gate), *_hi_lo(w_up), *_hi_lo(w_down))


def _qkv_kernel(x_ref, g_ref, w_hi, w_lo, q_ref, qs_ref, kv_ref):
    is_sample = pl.program_id(0) == N_TILES - 1

    def qkv(x, sample):
        out = _mm(_operand(_rms(x, g_ref[...]), sample), w_hi[...], w_lo[...])
        return out[:, :Q_DIM] * (HEAD_DIM ** -0.5), out[:, Q_DIM:]

    @pl.when(jnp.logical_not(is_sample))
    def _():
        q, kv = qkv(x_ref[...], sample=False)
        q_ref[...] = q.astype(BF16)
        kv_ref[...] = kv

    @pl.when(is_sample)
    def _():
        q, kv = qkv(x_ref[:DEC_BATCH, :], sample=True)
        qs_ref[...] = q
        kv_ref[:DEC_BATCH, :] = kv
        kv_ref[DEC_BATCH:, :] = jnp.zeros((TM - DEC_BATCH, 2 * KV_DIM), F32)


def _qkv_layer(x, g, w_qkv):
    tile = pl.BlockSpec((TM, D_MODEL), lambda i: (i, 0))
    return pl.pallas_call(
        _qkv_kernel,
        grid=(N_TILES,),
        in_specs=[tile, _const_spec((1, D_MODEL)),
                  _const_spec((D_MODEL, Q_DIM + 2 * KV_DIM)), _const_spec((D_MODEL, Q_DIM + 2 * KV_DIM))],
        out_specs=[pl.BlockSpec((TM, Q_DIM), lambda i: (jnp.minimum(i, PROMPT_TILES - 1), 0)),
                   pl.BlockSpec((DEC_BATCH, Q_DIM), lambda i: (0, 0)),
                   pl.BlockSpec((TM, 2 * KV_DIM), lambda i: (i, 0))],
        out_shape=[jax.ShapeDtypeStruct((N_PROMPT, Q_DIM), BF16),
                   jax.ShapeDtypeStruct((DEC_BATCH, Q_DIM), F32),
                   jax.ShapeDtypeStruct((T_PAD, 2 * KV_DIM), F32)],
        compiler_params=_params(),
        name="swa_qkv",
    )(x, g.reshape(1, -1), *_hi_lo(w_qkv))


def _half_masks_f32(x):
    lane = lax.broadcasted_iota(jnp.int32, x.shape, 1)
    low = lane < HEAD_DIM
    xr = pltpu.roll(x, HEAD_DIM, 1)
    return ((jnp.where(low, x, 0.0), jnp.where(low, 0.0, xr)),
            (jnp.where(low, xr, 0.0), jnp.where(low, 0.0, x)))


def _half_masks(x):
    return tuple(tuple(m.astype(BF16) for m in pair) for pair in _half_masks_f32(x))


def _swa_prompt_kernel(sink_ref, q_ref, kv_ref, kvp_ref, wg_ref, wu_ref, wd_ref,
                       o_ref, wg_out, wu_out, wd_out, bias_ref):
    wg_out[...] = wg_ref[...].astype(BF16)
    wu_out[...] = wu_ref[...].astype(BF16)
    wd_out[...] = wd_ref[...].astype(BF16)
    i = pl.program_id(0)
    t = lax.broadcasted_iota(jnp.int32, (BLOCK, BLOCK), 0)
    c = lax.broadcasted_iota(jnp.int32, (BLOCK, BLOCK), 1)
    own = c <= t
    diag = c == t

    @pl.when(i == 0)
    def _():
        dist = jnp.where(own, t - c, BLOCK + t - c).astype(F32)
        for hd in range(N_HEADS):
            bias_ref[hd] = _SLOPES[hd] * dist

    first_tile = (i % TILES_PER_SEQ) == 0
    kv_all = jnp.concatenate([kvp_ref[...], kv_ref[...]], axis=0)
    k_blk, v_blk, vf_blk = [], [], []
    for b in range(CHUNKS_PER_TILE + 1):
        blk = kv_all[b * BLOCK:(b + 1) * BLOCK]
        k_blk.append(_half_masks(blk[:, :KV_DIM]))
        v_blk.append(_half_masks(blk[:, KV_DIM:]))
        vf_blk.append(_half_masks_f32(blk[:, KV_DIM:]))
    has_prev = (jnp.zeros((BLOCK, BLOCK), jnp.int32) + jnp.where(first_tile, 0, 1)) == 1

    for b in range(CHUNKS_PER_TILE):
        rows = slice(b * BLOCK, (b + 1) * BLOCK)
        for p in range(N_PAIRS):
            kvh = (2 * p) // GQA_GROUP
            qp = q_ref[rows, p * LANES:(p + 1) * LANES]
            acc = None
            for par in range(2):
                hd = 2 * p + par
                sink = sink_ref[hd]
                keys = jnp.concatenate([k_blk[b][kvh][par], k_blk[b + 1][kvh][par]], axis=0)
                both = _dot_t(qp, keys)
                l_prev, l_own = both[:, :BLOCK], both[:, BLOCK:]
                if b == 0:
                    l_prev = jnp.where(has_prev, l_prev, -jnp.inf)
                logits = jnp.where(own, l_own, l_prev) - bias_ref[hd]
                extra = (jnp.sum(jnp.where(diag, l_prev, 0.0), axis=-1, keepdims=True)
                         - _SLOPES[hd] * BLOCK)
                m = jnp.maximum(jnp.maximum(jnp.max(logits, axis=-1, keepdims=True), extra), sink)
                e = jnp.exp(logits - m)
                e_extra = jnp.exp(extra - m)
                denom = jnp.sum(e, axis=-1, keepdims=True) + e_extra + jnp.exp(sink - m)
                probs = jnp.concatenate([jnp.where(own, 0.0, e).astype(BF16),
                                         jnp.where(own, e, 0.0).astype(BF16)], axis=1)
                vals = jnp.concatenate([v_blk[b][kvh][par], v_blk[b + 1][kvh][par]], axis=0)
                part = (_dot(probs, vals) + e_extra * vf_blk[b][kvh][par]) * (1.0 / denom)
                acc = part if acc is None else acc + part
            o_ref[rows, p * LANES:(p + 1) * LANES] = acc.astype(BF16)


def _swa_prompt(q, kv, sinks, layer, w_gate, w_up, w_down):
    n_layers = w_gate.shape[0]
    up_rows = N_EXPERTS * D_MODEL // PROMPT_TILES
    down_rows = N_EXPERTS * D_FF_EXPERT // PROMPT_TILES
    up_spec = pl.BlockSpec((None, up_rows, D_FF_EXPERT), lambda i: (layer, i, 0))
    down_spec = pl.BlockSpec((None, down_rows, D_MODEL), lambda i: (layer, i, 0))
    o, wg, wu, wd = pl.pallas_call(
        _swa_prompt_kernel,
        grid=(PROMPT_TILES,),
        in_specs=[pl.BlockSpec(memory_space=pltpu.SMEM),
                  pl.BlockSpec((TM, Q_DIM), lambda i: (i, 0)),
                  pl.BlockSpec((TM, 2 * KV_DIM), lambda i: (i, 0)),
                  pl.BlockSpec((BLOCK, 2 * KV_DIM),
                               lambda i: (jnp.maximum(i * CHUNKS_PER_TILE - 1, 0), 0)),
                  up_spec, up_spec, down_spec],
        out_specs=[pl.BlockSpec((TM, Q_DIM), lambda i: (i, 0)),
                   pl.BlockSpec((up_rows, D_FF_EXPERT), lambda i: (i, 0)),
                   pl.BlockSpec((up_rows, D_FF_EXPERT), lambda i: (i, 0)),
                   pl.BlockSpec((down_rows, D_MODEL), lambda i: (i, 0))],
        out_shape=[jax.ShapeDtypeStruct((N_PROMPT, Q_DIM), BF16),
                   jax.ShapeDtypeStruct((N_EXPERTS * D_MODEL, D_FF_EXPERT), BF16),
                   jax.ShapeDtypeStruct((N_EXPERTS * D_MODEL, D_FF_EXPERT), BF16),
                   jax.ShapeDtypeStruct((N_EXPERTS * D_FF_EXPERT, D_MODEL), BF16)],
        scratch_shapes=[pltpu.VMEM((N_HEADS, BLOCK, BLOCK), F32)],
        compiler_params=_params(),
        name="swa_prompt",
    )(sinks, q, kv, kv,
      w_gate.reshape(n_layers, N_EXPERTS * D_MODEL, D_FF_EXPERT),
      w_up.reshape(n_layers, N_EXPERTS * D_MODEL, D_FF_EXPERT),
      w_down.reshape(n_layers, N_EXPERTS * D_FF_EXPERT, D_MODEL))
    return (o, wg.reshape(N_EXPERTS, D_MODEL, D_FF_EXPERT), wu.reshape(N_EXPERTS, D_MODEL, D_FF_EXPERT),
            wd.reshape(N_EXPERTS, D_FF_EXPERT, D_MODEL))


SAMPLE_TILE = 32


def _swa_sample_kernel(q_ref, kvn_ref, ck_ref, cv_ref, slope_ref, sink_ref, o_ref):
    shape = (SAMPLE_TILE, N_PAIRS, LANES)
    lane = lax.broadcasted_iota(jnp.int32, shape, 2)
    pair = lax.broadcasted_iota(jnp.int32, shape, 1)
    low = lane < HEAD_DIM
    kv0 = pair < (N_PAIRS // 2)

    def swap(x):
        return pltpu.roll(x, HEAD_DIM, 2)

    def three_pass(dims, a, b):
        dot = lambda u, v: lax.dot_general(u, v, dims, preferred_element_type=F32)
        return dot(a[0], b[0]) + (dot(a[1], b[0]) + dot(a[0], b[1]))

    qk_dims = (((2,), (2,)), ((0,), (0,)))
    pv_dims = (((2,), (1,)), ((0,), (0,)))

    q = q_ref[...]
    q_even = jnp.where(low, q, 0.0)
    q_odd = jnp.where(low, 0.0, q)
    q_al = (jnp.where(kv0, q_even, swap(q_even)), jnp.where(kv0, swap(q_odd), q_odd))

    ck = _split(ck_ref[...])
    cv = _split(cv_ref[...])
    kvn = kvn_ref[...]
    k_new = kvn[:, :, :KV_DIM]
    v_new = kvn[:, :, KV_DIM:]
    r = lax.broadcasted_iota(jnp.int32, (SAMPLE_TILE, N_PAIRS, WINDOW), 2)
    dist = (WINDOW - r).astype(F32)

    outs = []
    for par in range(2):
        qa = q_al[par]
        slope = slope_ref[par]
        sink = sink_ref[par]
        logits = three_pass(qk_dims, _split(qa), ck) - slope * dist
        l_self = jnp.sum(qa * k_new, axis=-1, keepdims=True)
        m = jnp.maximum(jnp.maximum(jnp.max(logits, axis=-1, keepdims=True), l_self), sink)
        e = jnp.exp(logits - m)
        e_self = jnp.exp(l_self - m)
        inv = 1.0 / (jnp.sum(e, axis=-1, keepdims=True) + e_self + jnp.exp(sink - m))
        o = three_pass(pv_dims, _split(e * inv), cv) + (e_self * inv) * v_new
        outs.append(o)
    o_even = jnp.where(kv0, outs[0], swap(outs[0]))
    o_odd = jnp.where(kv0, swap(outs[1]), outs[1])
    o_ref[...] = jnp.where(low, o_even, o_odd)


def _swa_sample(q_s, kv_s, cache_k, cache_v, sinks):
    rows = cache_k.shape[1]
    slopes = np.asarray(_SLOPES, np.float32).reshape(N_PAIRS, 2).T.reshape(2, N_PAIRS, 1)
    sink_arr = sinks.astype(F32).reshape(N_PAIRS, 2).T.reshape(2, N_PAIRS, 1)
    blk = lambda *shape: pl.BlockSpec((SAMPLE_TILE,) + shape, lambda i: (i,) + (0,) * len(shape))
    o3 = pl.pallas_call(
        _swa_sample_kernel,
        grid=(DEC_BATCH // SAMPLE_TILE,),
        in_specs=[blk(N_PAIRS, LANES), blk(1, 2 * KV_DIM), blk(rows, KV_DIM), blk(rows, KV_DIM),
                  _const_spec((2, N_PAIRS, 1)), _const_spec((2, N_PAIRS, 1))],
        out_specs=blk(N_PAIRS, LANES),
        out_shape=jax.ShapeDtypeStruct((DEC_BATCH, N_PAIRS, LANES), F32),
        compiler_params=_params(),
        name="swa_sample",
    )(q_s.reshape(DEC_BATCH, N_PAIRS, LANES), kv_s.reshape(DEC_BATCH, 1, 2 * KV_DIM),
      cache_k.reshape(DEC_BATCH, rows, KV_DIM), cache_v.reshape(DEC_BATCH, rows, KV_DIM),
      jnp.asarray(slopes), sink_arr)
    return o3.reshape(DEC_BATCH, Q_DIM)


def _proj_route_kernel(x_ref, op_ref, os_ref, wo_hi, wo_lo, g_ref, wr_hi, wr_lo, xo_ref, idx_ref,
                       gate_ref, rank_ref, cnt_ref, carry_ref, xnew_ref):
    i = pl.program_id(0)
    is_sample = i == N_TILES - 1

    @pl.when(i == 0)
    def _():
        carry_ref[...] = jnp.zeros_like(carry_ref)

    @pl.when(jnp.logical_not(is_sample))
    def _():
        xnew_ref[...] = x_ref[...] + _dot(op_ref[...], wo_hi[...])

    @pl.when(is_sample)
    def _():
        xnew_ref[:DEC_BATCH, :] = x_ref[:DEC_BATCH, :] + _mm(_split(os_ref[...]), wo_hi[...], wo_lo[...])
        xnew_ref[DEC_BATCH:, :] = jnp.zeros((TM - DEC_BATCH, D_MODEL), F32)

    x = xnew_ref[...]
    _store_row_tiles(xo_ref, x)
    h = _rms(x, g_ref[...])
    logits = _mm(_split(h), wr_hi[...], wr_lo[...])
    lane = lax.broadcasted_iota(jnp.int32, (TM, LANES), 1)
    logits = jnp.where(lane < N_EXPERTS, logits, -jnp.inf)
    m0 = jnp.max(logits, axis=-1, keepdims=True)
    i0 = jnp.min(jnp.where(logits == m0, lane, LANES), axis=-1, keepdims=True)
    rest = jnp.where(lane == i0, -jnp.inf, logits)
    m1 = jnp.max(rest, axis=-1, keepdims=True)
    i1 = jnp.min(jnp.where(rest == m1, lane, LANES), axis=-1, keepdims=True)
    e1 = jnp.exp(m1 - m0)
    g0 = 1.0 / (1.0 + e1)
    g1 = e1 * g0
    idx_ref[...] = jnp.concatenate([i0, i1], axis=1)
    gate_ref[...] = jnp.concatenate([g0, g1], axis=1)

    row_id = i * TM + lax.broadcasted_iota(jnp.int32, (TM, 1), 0)
    onehot = jnp.where(((lane == i0) | (lane == i1)) & (row_id < T_REAL), 1.0, 0.0)
    r = lax.broadcasted_iota(jnp.int32, (TM, TM), 0)
    c = lax.broadcasted_iota(jnp.int32, (TM, TM), 1)
    before = jnp.where(c < r, 1.0, 0.0).astype(BF16)
    ranks = _dot(before, onehot.astype(BF16)) + carry_ref[...]
    r0 = jnp.sum(jnp.where(lane == i0, ranks, 0.0), axis=-1, keepdims=True)
    r1 = jnp.sum(jnp.where(lane == i1, ranks, 0.0), axis=-1, keepdims=True)
    rank_ref[...] = jnp.concatenate([r0, r1], axis=1).astype(jnp.int32)
    carry_ref[...] = carry_ref[...] + jnp.sum(onehot, axis=0, keepdims=True)
    cnt_ref[...] = carry_ref[...].astype(jnp.int32)


def _proj_route(x, o_prompt, o_sample, w_o, g, w_router):
    tile = pl.BlockSpec((TM, D_MODEL), lambda i: (i, 0))
    pair = pl.BlockSpec((TM, 2), lambda i: (i, 0))
    wr = jnp.zeros((D_MODEL, LANES), F32).at[:, :N_EXPERTS].set(w_router)
    return pl.pallas_call(
        _proj_route_kernel,
        grid=(N_TILES,),
        in_specs=[tile,
                  pl.BlockSpec((TM, Q_DIM), lambda i: (jnp.minimum(i, PROMPT_TILES - 1), 0)),
                  pl.BlockSpec((DEC_BATCH, Q_DIM), lambda i: (0, 0)),
                  _const_spec((Q_DIM, D_MODEL)), _const_spec((Q_DIM, D_MODEL)),
                  _const_spec((1, D_MODEL)), _const_spec((D_MODEL, LANES)),
                  _const_spec((D_MODEL, LANES))],
        out_specs=[pl.BlockSpec((TM * SUBLANES, LANES), lambda i: (i, 0)), pair, pair, pair,
                   pl.BlockSpec((1, LANES), lambda i: (0, 0))],
        out_shape=[jax.ShapeDtypeStruct((T_PAD * SUBLANES, LANES), F32),
                   jax.ShapeDtypeStruct((T_PAD, 2), jnp.int32),
                   jax.ShapeDtypeStruct((T_PAD, 2), F32),
                   jax.ShapeDtypeStruct((T_PAD, 2), jnp.int32),
                   jax.ShapeDtypeStruct((1, LANES), jnp.int32)],
        scratch_shapes=[pltpu.VMEM((1, LANES), F32), pltpu.VMEM((TM, D_MODEL), F32)],
        compiler_params=_params(),
        name="proj_route",
    )(x, o_prompt, o_sample, *_hi_lo(w_o), g.reshape(1, -1), *_hi_lo(wr))


def _tile_at(start):
    return pl.ds(start if isinstance(start, int) else pl.multiple_of(start, SUBLANES), SUBLANES)


def _gather_row(x_hbm, xbuf, sem, slot, src, r):
    return pltpu.make_async_copy(x_hbm.at[_tile_at(src), :], xbuf.at[slot, _tile_at(r * SUBLANES), :],
                                 sem.at[slot])


def _scatter_row(obuf, y_hbm, sem, slot, r, dst):
    return pltpu.make_async_copy(obuf.at[slot, _tile_at(r * SUBLANES), :], y_hbm.at[_tile_at(dst), :],
                                 sem.at[slot])


def _gather_wait(x_hbm, xbuf, sem, slot):
    pltpu.make_async_copy(x_hbm.at[pl.ds(0, TMM * SUBLANES), :], xbuf.at[slot], sem.at[slot]).wait()


def _scatter_wait(obuf, y_hbm, sem, slot):
    pltpu.make_async_copy(obuf.at[slot], y_hbm.at[pl.ds(0, TMM * SUBLANES), :], sem.at[slot]).wait()


def _slot_src(slot_ref, q):
    return slot_ref[q]


def _slot_dst(slot_ref, q):
    return slot_ref[SLOT_ENTRIES + q]


def _moe_kernel(te_ref, nu_ref, slot_ref, x_hbm, g_ref, wg_ref, wu_ref, wd_ref, y_hbm,
                xbuf, hbuf, obuf, gsem, ssem):
    i = pl.program_id(0)
    n_used = nu_ref[0]
    slot = i % 2
    other = 1 - slot

    @pl.when(i < n_used)
    def _():
        @pl.when(i == 0)
        def _():
            def start(r, carry):
                _gather_row(x_hbm, xbuf, gsem, 0, _slot_src(slot_ref, TMM + r), r).start()
                return carry
            lax.fori_loop(0, TMM, start, 0)
            obuf[...] = jnp.zeros_like(obuf)
            n_pad = (T_PAD - T_REAL) * SUBLANES
            for s in range(2):
                pad = pltpu.make_async_copy(
                    obuf.at[s, pl.ds(0, n_pad), :],
                    y_hbm.at[pl.ds((s * T_PAD + T_REAL) * SUBLANES, n_pad), :],
                    ssem.at[s])
                pad.start()
                pad.wait()
                spare = pltpu.make_async_copy(
                    obuf.at[s],
                    y_hbm.at[pl.ds((2 * T_PAD + s * TMM) * SUBLANES, TMM * SUBLANES), :],
                    ssem.at[s])
                spare.start()
                if s == 1:
                    spare.wait()

        _gather_wait(x_hbm, xbuf, gsem, slot)
        xin = xbuf.at[slot]
        chunks = [xin[_chunk(s, TMM), :] for s in range(ROW_CHUNKS)]
        ssq = chunks[0] * chunks[0]
        for c in chunks[1:]:
            ssq = ssq + c * c
        scale = lax.rsqrt(jnp.sum(ssq, axis=-1, keepdims=True) * (1.0 / D_MODEL) + EPS)
        for s, c in enumerate(chunks):
            cols = slice(s * LANES, (s + 1) * LANES)
            hbuf[:, cols] = (c * scale * g_ref[:, cols]).astype(BF16)

        nxt = (i + 2) * TMM
        prv = i * TMM
        for r in range(TMM):
            _gather_row(x_hbm, xbuf, gsem, other, _slot_src(slot_ref, nxt + r), r).start()
            _scatter_row(obuf, y_hbm, ssem, other, r, _slot_dst(slot_ref, prv + r)).start()

        h = hbuf[...]
        a = (_silu(_dot(h, wg_ref[...])) * _dot(h, wu_ref[...])).astype(BF16)
        out = _dot(a, wd_ref[...])
        _scatter_wait(obuf, y_hbm, ssem, slot)
        _store_row_tiles(obuf.at[slot], out)

        @pl.when(i == n_used - 1)
        def _():
            _gather_wait(x_hbm, xbuf, gsem, other)
            _scatter_wait(obuf, y_hbm, ssem, other)
            base = (i + 1) * TMM

            def start(r, carry):
                _scatter_row(obuf, y_hbm, ssem, slot, r, _slot_dst(slot_ref, base + r)).start()
                return carry
            lax.fori_loop(0, TMM, start, 0)
            _scatter_wait(obuf, y_hbm, ssem, slot)


def _moe_layer(x, g, tile_expert, n_used, slots, w_gate, w_up, w_down):
    def expert_weights(shape):
        return pl.BlockSpec((None,) + shape, lambda i, te, nu, tok: (te[i], 0, 0),
                            pipeline_mode=pl.Buffered(1))

    grid_spec = pltpu.PrefetchScalarGridSpec(
        num_scalar_prefetch=3,
        grid=(NT_MOE,),
        in_specs=[pl.BlockSpec(memory_space=pl.ANY),
                  pl.BlockSpec((1, D_MODEL), lambda i, te, nu, tok: (0, 0)),
                  expert_weights((D_MODEL, D_FF_EXPERT)),
                  expert_weights((D_MODEL, D_FF_EXPERT)),
                  expert_weights((D_FF_EXPERT, D_MODEL))],
        out_specs=pl.BlockSpec(memory_space=pl.ANY),
        scratch_shapes=[pltpu.VMEM((2, TMM * SUBLANES, LANES), F32),
                        pltpu.VMEM((TMM, D_MODEL), BF16),
                        pltpu.VMEM((2, TMM * SUBLANES, LANES), F32),
                        pltpu.SemaphoreType.DMA((2,)),
                        pltpu.SemaphoreType.DMA((2,))],
    )
    return pl.pallas_call(
        _moe_kernel,
        grid_spec=grid_spec,
        out_shape=jax.ShapeDtypeStruct((Y_ROWS * SUBLANES, LANES), F32),
        compiler_params=_params(1),
        name="moe_experts",
    )(tile_expert, n_used, slots, x, g.reshape(1, -1), w_gate, w_up, w_down)


def _final_kernel(x_ref, gate_ref, y0_ref, y1_ref, gf_ref, yp_ref, ys_ref):
    i = pl.program_id(0)
    y = _rms(_combined_rows(x_ref, gate_ref, y0_ref, y1_ref), gf_ref[...])

    @pl.when(i < PROMPT_TILES)
    def _():
        yp_ref[...] = y

    @pl.when(i == N_TILES - 1)
    def _():
        ys_ref[...] = y[:DEC_BATCH]


def _final_combine(x, gates, y_slots, g_final):
    return pl.pallas_call(
        _final_kernel,
        grid=(N_TILES,),
        in_specs=_row_tile_specs() + [_const_spec((1, D_MODEL))],
        out_specs=[pl.BlockSpec((TM, D_MODEL), lambda i: (jnp.minimum(i, PROMPT_TILES - 1), 0)),
                   pl.BlockSpec((DEC_BATCH, D_MODEL), lambda i: (0, 0))],
        out_shape=[jax.ShapeDtypeStruct((N_PROMPT, D_MODEL), F32),
                   jax.ShapeDtypeStruct((DEC_BATCH, D_MODEL), F32)],
        compiler_params=_params(),
        name="final_combine",
    )(x, gates, y_slots, y_slots, g_final.reshape(1, -1))


def _padding_slot_entries():
    p = np.arange(-TMM, P_TOTAL + TMM)
    spare_row = ((p // TMM) % 2) * TMM + p % TMM
    return np.concatenate([spare_row, 2 * T_PAD + spare_row]).astype(np.int32) * SUBLANES


FILL_UNROLL = 8


def _slot_fill_kernel(pos_ref, init_hbm, slots_hbm, buf, sem):
    load = pltpu.make_async_copy(init_hbm, buf, sem)
    load.start()
    load.wait()

    def body(j, carry):
        for u in range(FILL_UNROLL):
            t = j * FILL_UNROLL + u
            for k in range(2):
                q = TMM + pos_ref[2 * t + k]
                buf[q] = t * SUBLANES
                buf[SLOT_ENTRIES + q] = (k * T_PAD + t) * SUBLANES
        return carry

    lax.fori_loop(0, T_REAL // FILL_UNROLL, body, 0)
    store = pltpu.make_async_copy(buf, slots_hbm, sem)
    store.start()
    store.wait()


def _slot_fill(pos):
    return pl.pallas_call(
        _slot_fill_kernel,
        in_specs=[pl.BlockSpec(memory_space=pltpu.SMEM), pl.BlockSpec(memory_space=pl.ANY)],
        out_specs=pl.BlockSpec(memory_space=pl.ANY),
        out_shape=jax.ShapeDtypeStruct((2 * SLOT_ENTRIES,), jnp.int32),
        scratch_shapes=[pltpu.SMEM((2 * SLOT_ENTRIES,), jnp.int32), pltpu.SemaphoreType.DMA(())],
        name="slot_fill",
    )(pos.reshape(-1), jnp.asarray(_padding_slot_entries()))


def _slot_plan(idx, rank, counts):
    cnt = counts[0, :N_EXPERTS]
    tiles = (cnt + TMM - 1) // TMM
    tile_end = jnp.cumsum(tiles)
    start = (tile_end - tiles) * TMM
    n_used = tile_end[-1:].astype(jnp.int32)
    pos = (start[idx] + rank).astype(jnp.int32)
    slots = _slot_fill(pos)
    tile_ids = jnp.arange(NT_MOE, dtype=jnp.int32)
    tile_expert = jnp.minimum(jnp.sum(tile_end[None, :] <= tile_ids[:, None], axis=1),
                              N_EXPERTS - 1).astype(jnp.int32)
    last_expert = tile_expert[jnp.maximum(n_used[0] - 1, 0)]
    tile_expert = jnp.where(tile_ids < n_used[0], tile_expert, last_expert)
    return slots, tile_expert, n_used


def kernel(x_prompt, x_sample, cache_swa_k, cache_swa_v, norm_mix, norm_ffn, norm_final,
           sgu_w_in, sgu_b_in, sgu_ln_g, sgu_ln_b, sgu_w_s, sgu_b_s, sgu_w_out,
           attn_w_qkv, attn_sinks, attn_w_o,
           ffn_w_gate, ffn_w_up, ffn_w_down,
           moe_w_router, moe_w_gate, moe_w_up, moe_w_down):
    sgu_v_p, sgu_v_s, k_p, v_p, k_s, v_s = [], [], [], [], [], []
    rows_p = min(WINDOW, SEQ)
    rows, source = (x_prompt.reshape(N_PROMPT, D_MODEL), x_sample.reshape(DEC_BATCH, D_MODEL)), "inputs"
    for i in range(DEPTH):
        j = i // 2
        if i % 2 == 0:
            x, vlast = _sgu_layer(rows, source, norm_mix[i], sgu_w_in[j], sgu_b_in[j], sgu_ln_g[j],
                                  sgu_ln_b[j], sgu_w_s[j], sgu_b_s[j], sgu_w_out[j])
            sgu_v_p.append(vlast[:BATCH * CHUNK].reshape(BATCH, CHUNK, SGU_WIDTH))
            sgu_v_s.append(vlast[BATCH * CHUNK:].reshape(DEC_BATCH, 1, SGU_WIDTH))
            x = _ffn_layer(x, norm_ffn[i], ffn_w_gate[j], ffn_w_up[j], ffn_w_down[j])
        else:
            q, q_s, kv = _qkv_layer(x, norm_mix[i], attn_w_qkv[j])
            kv_p = kv[:N_PROMPT].reshape(BATCH, SEQ, 2 * KV_DIM)[:, SEQ - rows_p:]
            k_p.append(kv_p[..., :KV_DIM].reshape(BATCH, rows_p, N_KV_HEADS, HEAD_DIM))
            v_p.append(kv_p[..., KV_DIM:].reshape(BATCH, rows_p, N_KV_HEADS, HEAD_DIM))
            kv_s = kv[N_PROMPT:T_REAL]
            k_new = kv_s[:, None, :KV_DIM].reshape(DEC_BATCH, 1, N_KV_HEADS, HEAD_DIM)
            v_new = kv_s[:, None, KV_DIM:].reshape(DEC_BATCH, 1, N_KV_HEADS, HEAD_DIM)
            k_s.append(jnp.concatenate([cache_swa_k[j][:, 1:], k_new], axis=1))
            v_s.append(jnp.concatenate([cache_swa_v[j][:, 1:], v_new], axis=1))

            o, moe_wg, moe_wu, moe_wd = _swa_prompt(q, kv, attn_sinks[j].astype(F32), j,
                                                    moe_w_gate, moe_w_up, moe_w_down)
            o_s = _swa_sample(q_s, kv_s, cache_swa_k[j], cache_swa_v[j], attn_sinks[j])
            x, idx, gates, rank, counts = _proj_route(x, o, o_s, attn_w_o[j], norm_ffn[i],
                                                      moe_w_router[j])
            slots, tile_expert, n_used = _slot_plan(idx, rank, counts)
            y_slots = _moe_layer(x, norm_ffn[i], tile_expert, n_used, slots, moe_wg, moe_wu, moe_wd)
            rows, source = (x, gates, y_slots), "experts"
    y_prompt, y_sample = _final_combine(*rows, norm_final)
    y_prompt = y_prompt.reshape(BATCH, SEQ, D_MODEL)
    y_sample = y_sample.reshape(DEC_BATCH, 1, D_MODEL)
    return (y_prompt, y_sample, jnp.stack(sgu_v_p), jnp.stack(sgu_v_s),
            jnp.stack(k_p), jnp.stack(v_p), jnp.stack(k_s), jnp.stack(v_s))
```

```python
import functools

import numpy as np
import jax
import jax.numpy as jnp
from jax import lax
from jax.experimental import pallas as pl
from jax.experimental.pallas import tpu as pltpu

D_MODEL = 1024
BATCH = 4
SEQ = 4096
DEPTH = 4
DEC_BATCH = 128
PAST_LEN = 8192
CHUNK = 128
SGU_WIDTH = 2 * D_MODEL
SGU_GROUPS = 8
SGU_GROUP_DIM = SGU_WIDTH // SGU_GROUPS
WINDOW = 128
BLOCK = 128
HEAD_DIM = 64
N_HEADS = D_MODEL // HEAD_DIM
N_KV_HEADS = 2
GQA_GROUP = N_HEADS // N_KV_HEADS
Q_DIM = N_HEADS * HEAD_DIM
KV_DIM = N_KV_HEADS * HEAD_DIM
D_FF = 2816
N_EXPERTS = 8
D_FF_EXPERT = 3584
EPS = 1e-6
LN_EPS = 1e-5

F32 = jnp.float32
BF16 = jnp.bfloat16

LANES = 128
TM = 512
N_PROMPT = BATCH * SEQ
T_REAL = N_PROMPT + DEC_BATCH
N_TILES = -(-T_REAL // TM)
T_PAD = N_TILES * TM
PROMPT_TILES = N_PROMPT // TM
TILES_PER_SEQ = SEQ // TM
CHUNKS_PER_TILE = TM // CHUNK
N_PAIRS = N_HEADS // 2

TMM = 512
N_SLOTS = 2 * T_REAL
NT_MOE = (N_SLOTS + N_EXPERTS * (TMM - 1)) // TMM + 1
P_TOTAL = NT_MOE * TMM
Y_ROWS = 2 * T_PAD + 2 * TMM
SLOT_ENTRIES = P_TOTAL + 2 * TMM

VMEM_LIMIT = 56 * 1024 * 1024

_SLOPES = [2.0 ** (-8.0 * (h + 1) / N_HEADS) for h in range(N_HEADS)]


def _rms(x, g):
    return x * lax.rsqrt(jnp.mean(x * x, axis=-1, keepdims=True) + EPS) * g


def _gelu(x):
    k = -2.0 * np.sqrt(2.0 / np.pi) * np.log2(np.e)
    t = (x * x) * np.float32(0.044715 * k) + np.float32(k)
    return x * (1.0 / (1.0 + jnp.exp2(x * t)))


def _silu(x):
    return x * (1.0 / (1.0 + jnp.exp(-x)))


def _dot(a, b):
    return jnp.dot(a, b, preferred_element_type=F32)


def _split(x):
    hi = x.astype(BF16)
    return hi, (x - hi.astype(F32)).astype(BF16)


def _operand(x, precise):
    return _split(x) if precise else x.astype(BF16)


def _mm(x, w_hi, w_lo):
    if not isinstance(x, tuple):
        return _dot(x, w_hi)
    x_hi, x_lo = x
    return _dot(x_hi, w_hi) + (_dot(x_lo, w_hi) + _dot(x_hi, w_lo))


def _hi_lo(w, precise=True):
    if not precise:
        hi = w.astype(BF16)
        return hi, hi
    hi = lax.optimization_barrier(w.astype(BF16))
    return hi, (w - hi.astype(F32)).astype(BF16)


def _dot_t(a, b):
    return lax.dot_general(a, b, (((1,), (1,)), ((), ())), preferred_element_type=F32)


SUBLANES = 8
ROW_CHUNKS = D_MODEL // LANES


def _chunk(s, rows):
    return pl.ds(s, rows, stride=SUBLANES)


def _store_row_tiles(ref, x):
    rows = x.shape[0]
    for s in range(ROW_CHUNKS):
        ref[_chunk(s, rows), :] = x[:, s * LANES:(s + 1) * LANES]


def _const_spec(shape):
    nd = len(shape)
    return pl.BlockSpec(shape, lambda *_: (0,) * nd, pipeline_mode=pl.Buffered(1))


def _params(n_axes=1):
    return pltpu.CompilerParams(dimension_semantics=("arbitrary",) * n_axes,
                                vmem_limit_bytes=VMEM_LIMIT)


def _combined_rows(x_ref, gate_ref, y0_ref, y1_ref, rows=TM):
    gate = gate_ref[:rows, :]
    g0, g1 = gate[:, 0:1], gate[:, 1:2]
    return jnp.concatenate(
        [x_ref[_chunk(s, rows), :] + (g0 * y0_ref[_chunk(s, rows), :] + g1 * y1_ref[_chunk(s, rows), :])
         for s in range(ROW_CHUNKS)], axis=1)


def _sgu_kernel(*refs, source, precise):
    i = pl.program_id(0)
    is_sample = i == N_TILES - 1
    n_src = 2 if source == "inputs" else 4
    src, refs = refs[:n_src], refs[n_src:]
    (g_ref, win_hi, win_lo, bin_ref, lng_ref, lnb_ref, ws_ref, bs_ref, wout_hi, wout_lo,
     xo_ref, vlast_ref, ug_ref) = refs

    def mix(x, sample):
        h = _operand(_rms(x, g_ref[...]), sample and precise)
        v = _gelu(_mm(h, win_hi[:, SGU_WIDTH:], win_lo[:, SGU_WIDTH:]) + bin_ref[:, SGU_WIDTH:])
        mu = jnp.mean(v, axis=-1, keepdims=True)
        vc = v - mu
        var = jnp.mean(vc * vc, axis=-1, keepdims=True)
        vn = vc * lax.rsqrt(var + LN_EPS) * lng_ref[...] + lnb_ref[...]
        vb = vn.astype(BF16)
        row = lax.broadcasted_iota(jnp.int32, (CHUNK, CHUNK), 0)
        col = lax.broadcasted_iota(jnp.int32, (CHUNK, CHUNK), 1)
        y = jnp.zeros_like(x)
        for g in range(SGU_GROUPS):
            lo, hi = g * SGU_GROUP_DIM, (g + 1) * SGU_GROUP_DIM
            u = _gelu(_mm(h, win_hi[:, lo:hi], win_lo[:, lo:hi]) + bin_ref[:, lo:hi])
            if sample:
                gate = vn[:, lo:hi] * ws_ref[g][0:1, 0:1] + bs_ref[g][0:1, 0:1]
                y = y + _mm(_operand(u * gate, precise), wout_hi[lo:hi, :], wout_lo[lo:hi, :])
            else:
                w_tril = jnp.where(row >= col, ws_ref[g], 0.0).astype(BF16)
                gate = jnp.concatenate(
                    [_dot(w_tril, vb[c * CHUNK:(c + 1) * CHUNK, lo:hi]) + bs_ref[g]
                     for c in range(x.shape[0] // CHUNK)], axis=0)
                ug_ref[:, lo:hi] = (u * gate).astype(BF16)
        if not sample:
            y = _dot(ug_ref[...], wout_hi[...])
        return x + y, vn

    @pl.when(jnp.logical_not(is_sample))
    def _():
        x = src[0][...] if source == "inputs" else _combined_rows(*src)
        x_new, vn = mix(x, sample=False)
        xo_ref[...] = x_new
        vlast_ref[...] = vn[TM - CHUNK:]

    @pl.when(is_sample)
    def _():
        x = src[1][...] if source == "inputs" else _combined_rows(*src, rows=DEC_BATCH)
        x_new, vn = mix(x, sample=True)
        xo_ref[:DEC_BATCH, :] = x_new
        xo_ref[DEC_BATCH:, :] = jnp.zeros((TM - DEC_BATCH, D_MODEL), F32)
        vlast_ref[...] = vn


def _row_tile_specs():
    return [pl.BlockSpec((TM * SUBLANES, LANES), lambda i: (i, 0)),
            pl.BlockSpec((TM, 2), lambda i: (i, 0)),
            pl.BlockSpec((TM * SUBLANES, LANES), lambda i: (i, 0)),
            pl.BlockSpec((TM * SUBLANES, LANES), lambda i: (N_TILES + i, 0))]


def _sgu_layer(rows, source, precise, g, w_in, b_in, ln_g, ln_b, w_s, b_s, w_out):
    tile = pl.BlockSpec((TM, D_MODEL), lambda i: (i, 0))
    vlast_spec = pl.BlockSpec(
        (CHUNK, SGU_WIDTH),
        lambda i: (jnp.where(i == N_TILES - 1, BATCH, i // TILES_PER_SEQ), 0))
    if source == "inputs":
        row_specs = [pl.BlockSpec((TM, D_MODEL), lambda i: (jnp.minimum(i, PROMPT_TILES - 1), 0)),
                     pl.BlockSpec((DEC_BATCH, D_MODEL), lambda i: (0, 0))]
        row_args = rows
    else:
        row_specs = _row_tile_specs()
        x, gates, y_slots = rows
        row_args = (x, gates, y_slots, y_slots)
    return pl.pallas_call(
        functools.partial(_sgu_kernel, source=source, precise=precise),
        grid=(N_TILES,),
        in_specs=row_specs + [
                  _const_spec((1, D_MODEL)),
                  _const_spec((D_MODEL, 2 * SGU_WIDTH)),
                  _const_spec((D_MODEL, 2 * SGU_WIDTH)),
                  _const_spec((1, 2 * SGU_WIDTH)),
                  _const_spec((1, SGU_WIDTH)),
                  _const_spec((1, SGU_WIDTH)),
                  _const_spec((SGU_GROUPS, CHUNK, CHUNK)),
                  _const_spec((SGU_GROUPS, CHUNK, 1)),
                  _const_spec((SGU_WIDTH, D_MODEL)),
                  _const_spec((SGU_WIDTH, D_MODEL))],
        out_specs=[tile, vlast_spec],
        out_shape=[jax.ShapeDtypeStruct((T_PAD, D_MODEL), F32),
                   jax.ShapeDtypeStruct(((BATCH + 1) * CHUNK, SGU_WIDTH), F32)],
        scratch_shapes=[pltpu.VMEM((TM, SGU_WIDTH), BF16)],
        compiler_params=_params(),
        name="sgu_mixer",
    )(*row_args, g.reshape(1, -1), *_hi_lo(w_in, precise), b_in.reshape(1, -1), ln_g.reshape(1, -1),
      ln_b.reshape(1, -1), w_s, b_s.reshape(SGU_GROUPS, CHUNK, 1), *_hi_lo(w_out, precise))


def _ffn_kernel(x_ref, g_ref, wg_hi, wg_lo, wu_hi, wu_lo, wd_hi, wd_lo, xo_ref, *, precise):
    is_sample = pl.program_id(0) == N_TILES - 1

    def ffn(x, sample):
        h = _operand(_rms(x, g_ref[...]), sample and precise)
        a = _silu(_mm(h, wg_hi[...], wg_lo[...])) * _mm(h, wu_hi[...], wu_lo[...])
        return x + _mm(_operand(a, sample and precise), wd_hi[...], wd_lo[...])

    @pl.when(jnp.logical_not(is_sample))
    def _():
        xo_ref[...] = ffn(x_ref[...], sample=False)

    @pl.when(is_sample)
    def _():
        xo_ref[:DEC_BATCH, :] = ffn(x_ref[:DEC_BATCH, :], sample=True)
        xo_ref[DEC_BATCH:, :] = jnp.zeros((TM - DEC_BATCH, D_MODEL), F32)


def _ffn_layer(x, precise, g, w_gate, w_up, w_down):
    tile = pl.BlockSpec((TM, D_MODEL), lambda i: (i, 0))
    return pl.pallas_call(
        functools.partial(_ffn_kernel, precise=precise),
        grid=(N_TILES,),
        in_specs=[tile, _const_spec((1, D_MODEL))]
        + [_const_spec((D_MODEL, D_FF))] * 4 + [_const_spec((D_FF, D_MODEL))] * 2,
        out_specs=tile,
        out_shape=jax.ShapeDtypeStruct((T_PAD, D_MODEL), F32),
        compiler_params=_params(),
        name="dense_swiglu",
    )(x, g.reshape(1, -1), *_hi_lo(w_gate, precise), *_hi_lo(w_up, precise), *_hi_lo(w_down, precise))


def _qkv_kernel(x_ref, g_ref, w_hi, w_lo, q_ref, qs_ref, kv_ref, *, precise):
    is_sample = pl.program_id(0) == N_TILES - 1

    def qkv(x, sample):
        out = _mm(_operand(_rms(x, g_ref[...]), sample and precise), w_hi[...], w_lo[...])
        return out[:, :Q_DIM] * (HEAD_DIM ** -0.5), out[:, Q_DIM:]

    @pl.when(jnp.logical_not(is_sample))
    def _():
        q, kv = qkv(x_ref[...], sample=False)
        q_ref[...] = q.astype(BF16)
        kv_ref[...] = kv

    @pl.when(is_sample)
    def _():
        q, kv = qkv(x_ref[:DEC_BATCH, :], sample=True)
        qs_ref[...] = q
        kv_ref[:DEC_BATCH, :] = kv
        kv_ref[DEC_BATCH:, :] = jnp.zeros((TM - DEC_BATCH, 2 * KV_DIM), F32)


def _qkv_layer(x, precise, g, w_qkv):
    tile = pl.BlockSpec((TM, D_MODEL), lambda i: (i, 0))
    return pl.pallas_call(
        functools.partial(_qkv_kernel, precise=precise),
        grid=(N_TILES,),
        in_specs=[tile, _const_spec((1, D_MODEL)),
                  _const_spec((D_MODEL, Q_DIM + 2 * KV_DIM)), _const_spec((D_MODEL, Q_DIM + 2 * KV_DIM))],
        out_specs=[pl.BlockSpec((TM, Q_DIM), lambda i: (jnp.minimum(i, PROMPT_TILES - 1), 0)),
                   pl.BlockSpec((DEC_BATCH, Q_DIM), lambda i: (0, 0)),
                   pl.BlockSpec((TM, 2 * KV_DIM), lambda i: (i, 0))],
        out_shape=[jax.ShapeDtypeStruct((N_PROMPT, Q_DIM), BF16),
                   jax.ShapeDtypeStruct((DEC_BATCH, Q_DIM), F32),
                   jax.ShapeDtypeStruct((T_PAD, 2 * KV_DIM), F32)],
        compiler_params=_params(),
        name="swa_qkv",
    )(x, g.reshape(1, -1), *_hi_lo(w_qkv, precise))


def _half_masks_f32(x):
    lane = lax.broadcasted_iota(jnp.int32, x.shape, 1)
    low = lane < HEAD_DIM
    xr = pltpu.roll(x, HEAD_DIM, 1)
    return ((jnp.where(low, x, 0.0), jnp.where(low, 0.0, xr)),
            (jnp.where(low, xr, 0.0), jnp.where(low, 0.0, x)))


def _half_masks(x):
    return tuple(tuple(m.astype(BF16) for m in pair) for pair in _half_masks_f32(x))


def _swa_prompt_kernel(sink_ref, q_ref, kv_ref, kvp_ref, wg_ref, wu_ref, wd_ref,
                       o_ref, wg_out, wu_out, wd_out, bias_ref):
    wg_out[...] = wg_ref[...].astype(BF16)
    wu_out[...] = wu_ref[...].astype(BF16)
    wd_out[...] = wd_ref[...].astype(BF16)
    i = pl.program_id(0)
    t = lax.broadcasted_iota(jnp.int32, (BLOCK, BLOCK), 0)
    c = lax.broadcasted_iota(jnp.int32, (BLOCK, BLOCK), 1)
    own = c <= t
    diag = c == t

    @pl.when(i == 0)
    def _():
        dist = jnp.where(own, t - c, BLOCK + t - c).astype(F32)
        for hd in range(N_HEADS):
            bias_ref[hd] = _SLOPES[hd] * dist

    first_tile = (i % TILES_PER_SEQ) == 0
    kv_all = jnp.concatenate([kvp_ref[...], kv_ref[...]], axis=0)
    k_blk, v_blk, vf_blk = [], [], []
    for b in range(CHUNKS_PER_TILE + 1):
        blk = kv_all[b * BLOCK:(b + 1) * BLOCK]
        k_blk.append(_half_masks(blk[:, :KV_DIM]))
        v_blk.append(_half_masks(blk[:, KV_DIM:]))
        vf_blk.append(_half_masks_f32(blk[:, KV_DIM:]))
    has_prev = (jnp.zeros((BLOCK, BLOCK), jnp.int32) + jnp.where(first_tile, 0, 1)) == 1

    for b in range(CHUNKS_PER_TILE):
        rows = slice(b * BLOCK, (b + 1) * BLOCK)
        for p in range(N_PAIRS):
            kvh = (2 * p) // GQA_GROUP
            qp = q_ref[rows, p * LANES:(p + 1) * LANES]
            acc = None
            for par in range(2):
                hd = 2 * p + par
                sink = sink_ref[hd]
                keys = jnp.concatenate([k_blk[b][kvh][par], k_blk[b + 1][kvh][par]], axis=0)
                both = _dot_t(qp, keys)
                l_prev, l_own = both[:, :BLOCK], both[:, BLOCK:]
                if b == 0:
                    l_prev = jnp.where(has_prev, l_prev, -jnp.inf)
                logits = jnp.where(own, l_own, l_prev) - bias_ref[hd]
                extra = (jnp.sum(jnp.where(diag, l_prev, 0.0), axis=-1, keepdims=True)
                         - _SLOPES[hd] * BLOCK)
                m = jnp.maximum(jnp.maximum(jnp.max(logits, axis=-1, keepdims=True), extra), sink)
                e = jnp.exp(logits - m)
                e_extra = jnp.exp(extra - m)
                denom = jnp.sum(e, axis=-1, keepdims=True) + e_extra + jnp.exp(sink - m)
                probs = jnp.concatenate([jnp.where(own, 0.0, e).astype(BF16),
                                         jnp.where(own, e, 0.0).astype(BF16)], axis=1)
                vals = jnp.concatenate([v_blk[b][kvh][par], v_blk[b + 1][kvh][par]], axis=0)
                part = (_dot(probs, vals) + e_extra * vf_blk[b][kvh][par]) * (1.0 / denom)
                acc = part if acc is None else acc + part
            o_ref[rows, p * LANES:(p + 1) * LANES] = acc.astype(BF16)


def _swa_prompt(q, kv, sinks, layer, w_gate, w_up, w_down):
    n_layers = w_gate.shape[0]
    up_rows = N_EXPERTS * D_MODEL // PROMPT_TILES
    down_rows = N_EXPERTS * D_FF_EXPERT // PROMPT_TILES
    up_spec = pl.BlockSpec((None, up_rows, D_FF_EXPERT), lambda i: (layer, i, 0))
    down_spec = pl.BlockSpec((None, down_rows, D_MODEL), lambda i: (layer, i, 0))
    o, wg, wu, wd = pl.pallas_call(
        _swa_prompt_kernel,
        grid=(PROMPT_TILES,),
        in_specs=[pl.BlockSpec(memory_space=pltpu.SMEM),
                  pl.BlockSpec((TM, Q_DIM), lambda i: (i, 0)),
                  pl.BlockSpec((TM, 2 * KV_DIM), lambda i: (i, 0)),
                  pl.BlockSpec((BLOCK, 2 * KV_DIM),
                               lambda i: (jnp.maximum(i * CHUNKS_PER_TILE - 1, 0), 0)),
                  up_spec, up_spec, down_spec],
        out_specs=[pl.BlockSpec((TM, Q_DIM), lambda i: (i, 0)),
                   pl.BlockSpec((up_rows, D_FF_EXPERT), lambda i: (i, 0)),
                   pl.BlockSpec((up_rows, D_FF_EXPERT), lambda i: (i, 0)),
                   pl.BlockSpec((down_rows, D_MODEL), lambda i: (i, 0))],
        out_shape=[jax.ShapeDtypeStruct((N_PROMPT, Q_DIM), BF16),
                   jax.ShapeDtypeStruct((N_EXPERTS * D_MODEL, D_FF_EXPERT), BF16),
                   jax.ShapeDtypeStruct((N_EXPERTS * D_MODEL, D_FF_EXPERT), BF16),
                   jax.ShapeDtypeStruct((N_EXPERTS * D_FF_EXPERT, D_MODEL), BF16)],
        scratch_shapes=[pltpu.VMEM((N_HEADS, BLOCK, BLOCK), F32)],
        compiler_params=_params(),
        name="swa_prompt",
    )(sinks, q, kv, kv,
      w_gate.reshape(n_layers, N_EXPERTS * D_MODEL, D_FF_EXPERT),
      w_up.reshape(n_layers, N_EXPERTS * D_MODEL, D_FF_EXPERT),
      w_down.reshape(n_layers, N_EXPERTS * D_FF_EXPERT, D_MODEL))
    return (o, wg.reshape(N_EXPERTS, D_MODEL, D_FF_EXPERT), wu.reshape(N_EXPERTS, D_MODEL, D_FF_EXPERT),
            wd.reshape(N_EXPERTS, D_FF_EXPERT, D_MODEL))


SAMPLE_TILE = 32


def _swa_sample_kernel(q_ref, kvn_ref, ck_ref, cv_ref, slope_ref, sink_ref, o_ref, *, precise):
    shape = (SAMPLE_TILE, N_PAIRS, LANES)
    lane = lax.broadcasted_iota(jnp.int32, shape, 2)
    pair = lax.broadcasted_iota(jnp.int32, shape, 1)
    low = lane < HEAD_DIM
    kv0 = pair < (N_PAIRS // 2)

    def swap(x):
        return pltpu.roll(x, HEAD_DIM, 2)

    def three_pass(dims, a, b):
        dot = lambda u, v: lax.dot_general(u, v, dims, preferred_element_type=F32)
        if not precise:
            return dot(a[0], b[0])
        return dot(a[0], b[0]) + (dot(a[1], b[0]) + dot(a[0], b[1]))

    qk_dims = (((2,), (2,)), ((0,), (0,)))
    pv_dims = (((2,), (1,)), ((0,), (0,)))

    q = q_ref[...]
    q_even = jnp.where(low, q, 0.0)
    q_odd = jnp.where(low, 0.0, q)
    q_al = (jnp.where(kv0, q_even, swap(q_even)), jnp.where(kv0, swap(q_odd), q_odd))

    ck = _split(ck_ref[...])
    cv = _split(cv_ref[...])
    kvn = kvn_ref[...]
    k_new = kvn[:, :, :KV_DIM]
    v_new = kvn[:, :, KV_DIM:]
    r = lax.broadcasted_iota(jnp.int32, (SAMPLE_TILE, N_PAIRS, WINDOW), 2)
    dist = (WINDOW - r).astype(F32)

    outs = []
    for par in range(2):
        qa = q_al[par]
        slope = slope_ref[par]
        sink = sink_ref[par]
        logits = three_pass(qk_dims, _split(qa), ck) - slope * dist
        l_self = jnp.sum(qa * k_new, axis=-1, keepdims=True)
        m = jnp.maximum(jnp.maximum(jnp.max(logits, axis=-1, keepdims=True), l_self), sink)
        e = jnp.exp(logits - m)
        e_self = jnp.exp(l_self - m)
        inv = 1.0 / (jnp.sum(e, axis=-1, keepdims=True) + e_self + jnp.exp(sink - m))
        o = three_pass(pv_dims, _split(e * inv), cv) + (e_self * inv) * v_new
        outs.append(o)
    o_even = jnp.where(kv0, outs[0], swap(outs[0]))
    o_odd = jnp.where(kv0, swap(outs[1]), outs[1])
    o_ref[...] = jnp.where(low, o_even, o_odd)


def _swa_sample(q_s, kv_s, cache_k, cache_v, sinks, precise):
    rows = cache_k.shape[1]
    slopes = np.asarray(_SLOPES, np.float32).reshape(N_PAIRS, 2).T.reshape(2, N_PAIRS, 1)
    sink_arr = sinks.astype(F32).reshape(N_PAIRS, 2).T.reshape(2, N_PAIRS, 1)
    blk = lambda *shape: pl.BlockSpec((SAMPLE_TILE,) + shape, lambda i: (i,) + (0,) * len(shape))
    o3 = pl.pallas_call(
        functools.partial(_swa_sample_kernel, precise=precise),
        grid=(DEC_BATCH // SAMPLE_TILE,),
        in_specs=[blk(N_PAIRS, LANES), blk(1, 2 * KV_DIM), blk(rows, KV_DIM), blk(rows, KV_DIM),
                  _const_spec((2, N_PAIRS, 1)), _const_spec((2, N_PAIRS, 1))],
        out_specs=blk(N_PAIRS, LANES),
        out_shape=jax.ShapeDtypeStruct((DEC_BATCH, N_PAIRS, LANES), F32),
        compiler_params=_params(),
        name="swa_sample",
    )(q_s.reshape(DEC_BATCH, N_PAIRS, LANES), kv_s.reshape(DEC_BATCH, 1, 2 * KV_DIM),
      cache_k.reshape(DEC_BATCH, rows, KV_DIM), cache_v.reshape(DEC_BATCH, rows, KV_DIM),
      jnp.asarray(slopes), sink_arr)
    return o3.reshape(DEC_BATCH, Q_DIM)


def _proj_route_kernel(x_ref, op_ref, os_ref, wo_hi, wo_lo, g_ref, wr_hi, wr_lo, xo_ref, idx_ref,
                       gate_ref, rank_ref, cnt_ref, carry_ref, xnew_ref, *, precise):
    i = pl.program_id(0)
    is_sample = i == N_TILES - 1

    @pl.when(i == 0)
    def _():
        carry_ref[...] = jnp.zeros_like(carry_ref)

    @pl.when(jnp.logical_not(is_sample))
    def _():
        xnew_ref[...] = x_ref[...] + _dot(op_ref[...], wo_hi[...])

    @pl.when(is_sample)
    def _():
        xnew_ref[:DEC_BATCH, :] = x_ref[:DEC_BATCH, :] + _mm(_operand(os_ref[...], precise), wo_hi[...], wo_lo[...])
        xnew_ref[DEC_BATCH:, :] = jnp.zeros((TM - DEC_BATCH, D_MODEL), F32)

    x = xnew_ref[...]
    _store_row_tiles(xo_ref, x)
    h = _rms(x, g_ref[...])
    logits = _mm(_split(h), wr_hi[...], wr_lo[...])
    lane = lax.broadcasted_iota(jnp.int32, (TM, LANES), 1)
    logits = jnp.where(lane < N_EXPERTS, logits, -jnp.inf)
    m0 = jnp.max(logits, axis=-1, keepdims=True)
    i0 = jnp.min(jnp.where(logits == m0, lane, LANES), axis=-1, keepdims=True)
    rest = jnp.where(lane == i0, -jnp.inf, logits)
    m1 = jnp.max(rest, axis=-1, keepdims=True)
    i1 = jnp.min(jnp.where(rest == m1, lane, LANES), axis=-1, keepdims=True)
    e1 = jnp.exp(m1 - m0)
    g0 = 1.0 / (1.0 + e1)
    g1 = e1 * g0
    idx_ref[...] = jnp.concatenate([i0, i1], axis=1)
    gate_ref[...] = jnp.concatenate([g0, g1], axis=1)

    row_id = i * TM + lax.broadcasted_iota(jnp.int32, (TM, 1), 0)
    onehot = jnp.where(((lane == i0) | (lane == i1)) & (row_id < T_REAL), 1.0, 0.0)
    r = lax.broadcasted_iota(jnp.int32, (TM, TM), 0)
    c = lax.broadcasted_iota(jnp.int32, (TM, TM), 1)
    before = jnp.where(c < r, 1.0, 0.0).astype(BF16)
    ranks = _dot(before, onehot.astype(BF16)) + carry_ref[...]
    r0 = jnp.sum(jnp.where(lane == i0, ranks, 0.0), axis=-1, keepdims=True)
    r1 = jnp.sum(jnp.where(lane == i1, ranks, 0.0), axis=-1, keepdims=True)
    rank_ref[...] = jnp.concatenate([r0, r1], axis=1).astype(jnp.int32)
    carry_ref[...] = carry_ref[...] + jnp.sum(onehot, axis=0, keepdims=True)
    cnt_ref[...] = carry_ref[...].astype(jnp.int32)


def _proj_route(x, o_prompt, o_sample, precise, w_o, g, w_router):
    tile = pl.BlockSpec((TM, D_MODEL), lambda i: (i, 0))
    pair = pl.BlockSpec((TM, 2), lambda i: (i, 0))
    wr = jnp.zeros((D_MODEL, LANES), F32).at[:, :N_EXPERTS].set(w_router)
    return pl.pallas_call(
        functools.partial(_proj_route_kernel, precise=precise),
        grid=(N_TILES,),
        in_specs=[tile,
                  pl.BlockSpec((TM, Q_DIM), lambda i: (jnp.minimum(i, PROMPT_TILES - 1), 0)),
                  pl.BlockSpec((DEC_BATCH, Q_DIM), lambda i: (0, 0)),
                  _const_spec((Q_DIM, D_MODEL)), _const_spec((Q_DIM, D_MODEL)),
                  _const_spec((1, D_MODEL)), _const_spec((D_MODEL, LANES)),
                  _const_spec((D_MODEL, LANES))],
        out_specs=[pl.BlockSpec((TM * SUBLANES, LANES), lambda i: (i, 0)), pair, pair, pair,
                   pl.BlockSpec((1, LANES), lambda i: (0, 0))],
        out_shape=[jax.ShapeDtypeStruct((T_PAD * SUBLANES, LANES), F32),
                   jax.ShapeDtypeStruct((T_PAD, 2), jnp.int32),
                   jax.ShapeDtypeStruct((T_PAD, 2), F32),
                   jax.ShapeDtypeStruct((T_PAD, 2), jnp.int32),
                   jax.ShapeDtypeStruct((1, LANES), jnp.int32)],
        scratch_shapes=[pltpu.VMEM((1, LANES), F32), pltpu.VMEM((TM, D_MODEL), F32)],
        compiler_params=_params(),
        name="proj_route",
    )(x, o_prompt, o_sample, *_hi_lo(w_o, precise), g.reshape(1, -1), *_hi_lo(wr))


def _tile_at(start):
    return pl.ds(start if isinstance(start, int) else pl.multiple_of(start, SUBLANES), SUBLANES)


def _gather_row(x_hbm, xbuf, sem, slot, src, r):
    return pltpu.make_async_copy(x_hbm.at[_tile_at(src), :], xbuf.at[slot, _tile_at(r * SUBLANES), :],
                                 sem.at[slot])


def _scatter_row(obuf, y_hbm, sem, slot, r, dst):
    return pltpu.make_async_copy(obuf.at[slot, _tile_at(r * SUBLANES), :], y_hbm.at[_tile_at(dst), :],
                                 sem.at[slot])


def _gather_wait(x_hbm, xbuf, sem, slot):
    pltpu.make_async_copy(x_hbm.at[pl.ds(0, TMM * SUBLANES), :], xbuf.at[slot], sem.at[slot]).wait()


def _scatter_wait(obuf, y_hbm, sem, slot):
    pltpu.make_async_copy(obuf.at[slot], y_hbm.at[pl.ds(0, TMM * SUBLANES), :], sem.at[slot]).wait()


def _slot_src(slot_ref, q):
    return slot_ref[q]


def _slot_dst(slot_ref, q):
    return slot_ref[SLOT_ENTRIES + q]


def _moe_kernel(te_ref, nu_ref, slot_ref, x_hbm, g_ref, wg_ref, wu_ref, wd_ref, y_hbm,
                xbuf, hbuf, obuf, gsem, ssem):
    i = pl.program_id(0)
    n_used = nu_ref[0]
    slot = i % 2
    other = 1 - slot

    @pl.when(i < n_used)
    def _():
        @pl.when(i == 0)
        def _():
            def start(r, carry):
                _gather_row(x_hbm, xbuf, gsem, 0, _slot_src(slot_ref, TMM + r), r).start()
                return carry
            lax.fori_loop(0, TMM, start, 0)
            obuf[...] = jnp.zeros_like(obuf)
            n_pad = (T_PAD - T_REAL) * SUBLANES
            for s in range(2):
                pad = pltpu.make_async_copy(
                    obuf.at[s, pl.ds(0, n_pad), :],
                    y_hbm.at[pl.ds((s * T_PAD + T_REAL) * SUBLANES, n_pad), :],
                    ssem.at[s])
                pad.start()
                pad.wait()
                spare = pltpu.make_async_copy(
                    obuf.at[s],
                    y_hbm.at[pl.ds((2 * T_PAD + s * TMM) * SUBLANES, TMM * SUBLANES), :],
                    ssem.at[s])
                spare.start()
                if s == 1:
                    spare.wait()

        _gather_wait(x_hbm, xbuf, gsem, slot)
        xin = xbuf.at[slot]
        chunks = [xin[_chunk(s, TMM), :] for s in range(ROW_CHUNKS)]
        ssq = chunks[0] * chunks[0]
        for c in chunks[1:]:
            ssq = ssq + c * c
        scale = lax.rsqrt(jnp.sum(ssq, axis=-1, keepdims=True) * (1.0 / D_MODEL) + EPS)
        for s, c in enumerate(chunks):
            cols = slice(s * LANES, (s + 1) * LANES)
            hbuf[:, cols] = (c * scale * g_ref[:, cols]).astype(BF16)

        nxt = (i + 2) * TMM
        prv = i * TMM
        for r in range(TMM):
            _gather_row(x_hbm, xbuf, gsem, other, _slot_src(slot_ref, nxt + r), r).start()
            _scatter_row(obuf, y_hbm, ssem, other, r, _slot_dst(slot_ref, prv + r)).start()

        h = hbuf[...]
        a = (_silu(_dot(h, wg_ref[...])) * _dot(h, wu_ref[...])).astype(BF16)
        out = _dot(a, wd_ref[...])
        _scatter_wait(obuf, y_hbm, ssem, slot)
        _store_row_tiles(obuf.at[slot], out)

        @pl.when(i == n_used - 1)
        def _():
            _gather_wait(x_hbm, xbuf, gsem, other)
            _scatter_wait(obuf, y_hbm, ssem, other)
            base = (i + 1) * TMM

            def start(r, carry):
                _scatter_row(obuf, y_hbm, ssem, slot, r, _slot_dst(slot_ref, base + r)).start()
                return carry
            lax.fori_loop(0, TMM, start, 0)
            _scatter_wait(obuf, y_hbm, ssem, slot)


def _moe_layer(x, g, tile_expert, n_used, slots, w_gate, w_up, w_down):
    def expert_weights(shape):
        return pl.BlockSpec((None,) + shape, lambda i, te, nu, tok: (te[i], 0, 0),
                            pipeline_mode=pl.Buffered(1))

    grid_spec = pltpu.PrefetchScalarGridSpec(
        num_scalar_prefetch=3,
        grid=(NT_MOE,),
        in_specs=[pl.BlockSpec(memory_space=pl.ANY),
                  pl.BlockSpec((1, D_MODEL), lambda i, te, nu, tok: (0, 0)),
                  expert_weights((D_MODEL, D_FF_EXPERT)),
                  expert_weights((D_MODEL, D_FF_EXPERT)),
                  expert_weights((D_FF_EXPERT, D_MODEL))],
        out_specs=pl.BlockSpec(memory_space=pl.ANY),
        scratch_shapes=[pltpu.VMEM((2, TMM * SUBLANES, LANES), F32),
                        pltpu.VMEM((TMM, D_MODEL), BF16),
                        pltpu.VMEM((2, TMM * SUBLANES, LANES), F32),
                        pltpu.SemaphoreType.DMA((2,)),
                        pltpu.SemaphoreType.DMA((2,))],
    )
    return pl.pallas_call(
        _moe_kernel,
        grid_spec=grid_spec,
        out_shape=jax.ShapeDtypeStruct((Y_ROWS * SUBLANES, LANES), F32),
        compiler_params=_params(1),
        name="moe_experts",
    )(tile_expert, n_used, slots, x, g.reshape(1, -1), w_gate, w_up, w_down)


def _final_kernel(x_ref, gate_ref, y0_ref, y1_ref, gf_ref, yp_ref, ys_ref):
    i = pl.program_id(0)
    y = _rms(_combined_rows(x_ref, gate_ref, y0_ref, y1_ref), gf_ref[...])

    @pl.when(i < PROMPT_TILES)
    def _():
        yp_ref[...] = y

    @pl.when(i == N_TILES - 1)
    def _():
        ys_ref[...] = y[:DEC_BATCH]


def _final_combine(x, gates, y_slots, g_final):
    return pl.pallas_call(
        _final_kernel,
        grid=(N_TILES,),
        in_specs=_row_tile_specs() + [_const_spec((1, D_MODEL))],
        out_specs=[pl.BlockSpec((TM, D_MODEL), lambda i: (jnp.minimum(i, PROMPT_TILES - 1), 0)),
                   pl.BlockSpec((DEC_BATCH, D_MODEL), lambda i: (0, 0))],
        out_shape=[jax.ShapeDtypeStruct((N_PROMPT, D_MODEL), F32),
                   jax.ShapeDtypeStruct((DEC_BATCH, D_MODEL), F32)],
        compiler_params=_params(),
        name="final_combine",
    )(x, gates, y_slots, y_slots, g_final.reshape(1, -1))


def _padding_slot_entries():
    p = np.arange(-TMM, P_TOTAL + TMM)
    spare_row = ((p // TMM) % 2) * TMM + p % TMM
    return np.concatenate([spare_row, 2 * T_PAD + spare_row]).astype(np.int32) * SUBLANES


FILL_UNROLL = 8


def _slot_fill_kernel(pos_ref, init_hbm, slots_hbm, buf, sem):
    load = pltpu.make_async_copy(init_hbm, buf, sem)
    load.start()
    load.wait()

    def body(j, carry):
        for u in range(FILL_UNROLL):
            t = j * FILL_UNROLL + u
            for k in range(2):
                q = TMM + pos_ref[2 * t + k]
                buf[q] = t * SUBLANES
                buf[SLOT_ENTRIES + q] = (k * T_PAD + t) * SUBLANES
        return carry

    lax.fori_loop(0, T_REAL // FILL_UNROLL, body, 0)
    store = pltpu.make_async_copy(buf, slots_hbm, sem)
    store.start()
    store.wait()


def _slot_fill(pos):
    return pl.pallas_call(
        _slot_fill_kernel,
        in_specs=[pl.BlockSpec(memory_space=pltpu.SMEM), pl.BlockSpec(memory_space=pl.ANY)],
        out_specs=pl.BlockSpec(memory_space=pl.ANY),
        out_shape=jax.ShapeDtypeStruct((2 * SLOT_ENTRIES,), jnp.int32),
        scratch_shapes=[pltpu.SMEM((2 * SLOT_ENTRIES,), jnp.int32), pltpu.SemaphoreType.DMA(())],
        name="slot_fill",
    )(pos.reshape(-1), jnp.asarray(_padding_slot_entries()))


def _slot_plan(idx, rank, counts):
    cnt = counts[0, :N_EXPERTS]
    tiles = (cnt + TMM - 1) // TMM
    tile_end = jnp.cumsum(tiles)
    start = (tile_end - tiles) * TMM
    n_used = tile_end[-1:].astype(jnp.int32)
    pos = (start[idx] + rank).astype(jnp.int32)
    slots = _slot_fill(pos)
    tile_ids = jnp.arange(NT_MOE, dtype=jnp.int32)
    tile_expert = jnp.minimum(jnp.sum(tile_end[None, :] <= tile_ids[:, None], axis=1),
                              N_EXPERTS - 1).astype(jnp.int32)
    last_expert = tile_expert[jnp.maximum(n_used[0] - 1, 0)]
    tile_expert = jnp.where(tile_ids < n_used[0], tile_expert, last_expert)
    return slots, tile_expert, n_used


def kernel(x_prompt, x_sample, cache_swa_k, cache_swa_v, norm_mix, norm_ffn, norm_final,
           sgu_w_in, sgu_b_in, sgu_ln_g, sgu_ln_b, sgu_w_s, sgu_b_s, sgu_w_out,
           attn_w_qkv, attn_sinks, attn_w_o,
           ffn_w_gate, ffn_w_up, ffn_w_down,
           moe_w_router, moe_w_gate, moe_w_up, moe_w_down):
    sgu_v_p, sgu_v_s, k_p, v_p, k_s, v_s = [], [], [], [], [], []
    rows_p = min(WINDOW, SEQ)
    rows, source = (x_prompt.reshape(N_PROMPT, D_MODEL), x_sample.reshape(DEC_BATCH, D_MODEL)), "inputs"
    for i in range(DEPTH):
        j = i // 2
        precise = i <= 1
        if i % 2 == 0:
            x, vlast = _sgu_layer(rows, source, precise, norm_mix[i], sgu_w_in[j], sgu_b_in[j], sgu_ln_g[j],
                                  sgu_ln_b[j], sgu_w_s[j], sgu_b_s[j], sgu_w_out[j])
            sgu_v_p.append(vlast[:BATCH * CHUNK].reshape(BATCH, CHUNK, SGU_WIDTH))
            sgu_v_s.append(vlast[BATCH * CHUNK:].reshape(DEC_BATCH, 1, SGU_WIDTH))
            x = _ffn_layer(x, precise, norm_ffn[i], ffn_w_gate[j], ffn_w_up[j], ffn_w_down[j])
        else:
            q, q_s, kv = _qkv_layer(x, precise, norm_mix[i], attn_w_qkv[j])
            kv_p = kv[:N_PROMPT].reshape(BATCH, SEQ, 2 * KV_DIM)[:, SEQ - rows_p:]
            k_p.append(kv_p[..., :KV_DIM].reshape(BATCH, rows_p, N_KV_HEADS, HEAD_DIM))
            v_p.append(kv_p[..., KV_DIM:].reshape(BATCH, rows_p, N_KV_HEADS, HEAD_DIM))
            kv_s = kv[N_PROMPT:T_REAL]
            k_new = kv_s[:, None, :KV_DIM].reshape(DEC_BATCH, 1, N_KV_HEADS, HEAD_DIM)
            v_new = kv_s[:, None, KV_DIM:].reshape(DEC_BATCH, 1, N_KV_HEADS, HEAD_DIM)
            k_s.append(jnp.concatenate([cache_swa_k[j][:, 1:], k_new], axis=1))
            v_s.append(jnp.concatenate([cache_swa_v[j][:, 1:], v_new], axis=1))

            o, moe_wg, moe_wu, moe_wd = _swa_prompt(q, kv, attn_sinks[j].astype(F32), j,
                                                    moe_w_gate, moe_w_up, moe_w_down)
            o_s = _swa_sample(q_s, kv_s, cache_swa_k[j], cache_swa_v[j], attn_sinks[j], precise)
            x, idx, gates, rank, counts = _proj_route(x, o, o_s, precise, attn_w_o[j], norm_ffn[i],
                                                      moe_w_router[j])
            slots, tile_expert, n_used = _slot_plan(idx, rank, counts)
            y_slots = _moe_layer(x, norm_ffn[i], tile_expert, n_used, slots, moe_wg, moe_wu, moe_wd)
            rows, source = (x, gates, y_slots), "experts"
    y_prompt, y_sample = _final_combine(*rows, norm_final)
    y_prompt = y_prompt.reshape(BATCH, SEQ, D_MODEL)
    y_sample = y_sample.reshape(DEC_BATCH, 1, D_MODEL)
    return (y_prompt, y_sample, jnp.stack(sgu_v_p), jnp.stack(sgu_v_s),
            jnp.stack(k_p), jnp.stack(v_p), jnp.stack(k_s), jnp.stack(v_s))
```

```python
import functools

import numpy as np
import jax
import jax.numpy as jnp
from jax import lax
from jax.experimental import pallas as pl
from jax.experimental.pallas import tpu as pltpu

D_MODEL = 1024
BATCH = 4
SEQ = 4096
DEPTH = 4
DEC_BATCH = 128
PAST_LEN = 8192
CHUNK = 128
SGU_WIDTH = 2 * D_MODEL
SGU_GROUPS = 8
SGU_GROUP_DIM = SGU_WIDTH // SGU_GROUPS
WINDOW = 128
BLOCK = 128
HEAD_DIM = 64
N_HEADS = D_MODEL // HEAD_DIM
N_KV_HEADS = 2
GQA_GROUP = N_HEADS // N_KV_HEADS
Q_DIM = N_HEADS * HEAD_DIM
KV_DIM = N_KV_HEADS * HEAD_DIM
D_FF = 2816
N_EXPERTS = 8
D_FF_EXPERT = 3584
EPS = 1e-6
LN_EPS = 1e-5

F32 = jnp.float32
BF16 = jnp.bfloat16

LANES = 128
TM = 512
N_PROMPT = BATCH * SEQ
T_REAL = N_PROMPT + DEC_BATCH
N_TILES = -(-T_REAL // TM)
T_PAD = N_TILES * TM
PROMPT_TILES = N_PROMPT // TM
TILES_PER_SEQ = SEQ // TM
CHUNKS_PER_TILE = TM // CHUNK
N_PAIRS = N_HEADS // 2

TMM = 512
N_SLOTS = 2 * T_REAL
NT_MOE = (N_SLOTS + N_EXPERTS * (TMM - 1)) // TMM + 1
P_TOTAL = NT_MOE * TMM
Y_ROWS = 2 * T_PAD + 2 * TMM
SLOT_ENTRIES = P_TOTAL + 2 * TMM

VMEM_LIMIT = 56 * 1024 * 1024

_SLOPES = [2.0 ** (-8.0 * (h + 1) / N_HEADS) for h in range(N_HEADS)]


def _rms(x, g):
    return x * lax.rsqrt(jnp.mean(x * x, axis=-1, keepdims=True) + EPS) * g


def _gelu(x):
    k = -2.0 * np.sqrt(2.0 / np.pi) * np.log2(np.e)
    t = (x * x) * np.float32(0.044715 * k) + np.float32(k)
    return x * (1.0 / (1.0 + jnp.exp2(x * t)))


def _silu(x):
    return x * (1.0 / (1.0 + jnp.exp(-x)))


def _dot(a, b):
    return jnp.dot(a, b, preferred_element_type=F32)


def _split(x):
    hi = x.astype(BF16)
    return hi, (x - hi.astype(F32)).astype(BF16)


def _operand(x, precise):
    return _split(x) if precise else x.astype(BF16)


def _mm(x, w_hi, w_lo):
    if not isinstance(x, tuple):
        return _dot(x, w_hi)
    x_hi, x_lo = x
    return _dot(x_hi, w_hi) + (_dot(x_lo, w_hi) + _dot(x_hi, w_lo))


def _hi_lo(w, precise=True):
    if not precise:
        hi = w.astype(BF16)
        return hi, hi
    hi = lax.optimization_barrier(w.astype(BF16))
    return hi, (w - hi.astype(F32)).astype(BF16)


def _dot_t(a, b):
    return lax.dot_general(a, b, (((1,), (1,)), ((), ())), preferred_element_type=F32)


SUBLANES = 8
ROW_CHUNKS = D_MODEL // LANES


def _chunk(s, rows):
    return pl.ds(s, rows, stride=SUBLANES)


def _store_row_tiles(ref, x):
    rows = x.shape[0]
    for s in range(ROW_CHUNKS):
        ref[_chunk(s, rows), :] = x[:, s * LANES:(s + 1) * LANES]


def _const_spec(shape):
    nd = len(shape)
    return pl.BlockSpec(shape, lambda *_: (0,) * nd, pipeline_mode=pl.Buffered(1))


def _params(n_axes=1):
    return pltpu.CompilerParams(dimension_semantics=("arbitrary",) * n_axes,
                                vmem_limit_bytes=VMEM_LIMIT)


def _combined_rows(x_ref, gate_ref, y0_ref, y1_ref, rows=TM):
    gate = gate_ref[:rows, :]
    g0, g1 = gate[:, 0:1], gate[:, 1:2]
    return jnp.concatenate(
        [x_ref[_chunk(s, rows), :] + (g0 * y0_ref[_chunk(s, rows), :] + g1 * y1_ref[_chunk(s, rows), :])
         for s in range(ROW_CHUNKS)], axis=1)


def _sgu_kernel(*refs, source, precise):
    i = pl.program_id(0)
    is_sample = i == N_TILES - 1
    n_src = 2 if source == "inputs" else 4
    src, refs = refs[:n_src], refs[n_src:]
    (g_ref, win_hi, win_lo, bin_ref, lng_ref, lnb_ref, ws_ref, bs_ref, wout_hi, wout_lo,
     xo_ref, vlast_ref, ug_ref) = refs

    def mix(x, sample):
        h = _operand(_rms(x, g_ref[...]), sample and precise)
        v = _gelu(_mm(h, win_hi[:, SGU_WIDTH:], win_lo[:, SGU_WIDTH:]) + bin_ref[:, SGU_WIDTH:])
        mu = jnp.mean(v, axis=-1, keepdims=True)
        vc = v - mu
        var = jnp.mean(vc * vc, axis=-1, keepdims=True)
        vn = vc * lax.rsqrt(var + LN_EPS) * lng_ref[...] + lnb_ref[...]
        vb = vn.astype(BF16)
        row = lax.broadcasted_iota(jnp.int32, (CHUNK, CHUNK), 0)
        col = lax.broadcasted_iota(jnp.int32, (CHUNK, CHUNK), 1)
        y = jnp.zeros_like(x)
        for g in range(SGU_GROUPS):
            lo, hi = g * SGU_GROUP_DIM, (g + 1) * SGU_GROUP_DIM
            u = _gelu(_mm(h, win_hi[:, lo:hi], win_lo[:, lo:hi]) + bin_ref[:, lo:hi])
            if sample:
                gate = vn[:, lo:hi] * ws_ref[g][0:1, 0:1] + bs_ref[g][0:1, 0:1]
                y = y + _mm(_operand(u * gate, precise), wout_hi[lo:hi, :], wout_lo[lo:hi, :])
            else:
                w_tril = jnp.where(row >= col, ws_ref[g], 0.0).astype(BF16)
                gate = jnp.concatenate(
                    [_dot(w_tril, vb[c * CHUNK:(c + 1) * CHUNK, lo:hi]) + bs_ref[g]
                     for c in range(x.shape[0] // CHUNK)], axis=0)
                ug_ref[:, lo:hi] = (u * gate).astype(BF16)
        if not sample:
            y = _dot(ug_ref[...], wout_hi[...])
        return x + y, vn

    @pl.when(jnp.logical_not(is_sample))
    def _():
        x = src[0][...] if source == "inputs" else _combined_rows(*src)
        x_new, vn = mix(x, sample=False)
        xo_ref[...] = x_new
        vlast_ref[...] = vn[TM - CHUNK:]

    @pl.when(is_sample)
    def _():
        x = src[1][...] if source == "inputs" else _combined_rows(*src, rows=DEC_BATCH)
        x_new, vn = mix(x, sample=True)
        xo_ref[:DEC_BATCH, :] = x_new
        xo_ref[DEC_BATCH:, :] = jnp.zeros((TM - DEC_BATCH, D_MODEL), F32)
        vlast_ref[...] = vn


def _row_tile_specs():
    return [pl.BlockSpec((TM * SUBLANES, LANES), lambda i: (i, 0)),
            pl.BlockSpec((TM, 2), lambda i: (i, 0)),
            pl.BlockSpec((TM * SUBLANES, LANES), lambda i: (i, 0)),
            pl.BlockSpec((TM * SUBLANES, LANES), lambda i: (N_TILES + i, 0))]


def _sgu_layer(rows, source, precise, g, w_in, b_in, ln_g, ln_b, w_s, b_s, w_out):
    tile = pl.BlockSpec((TM, D_MODEL), lambda i: (i, 0))
    vlast_spec = pl.BlockSpec(
        (CHUNK, SGU_WIDTH),
        lambda i: (jnp.where(i == N_TILES - 1, BATCH, i // TILES_PER_SEQ), 0))
    if source == "inputs":
        row_specs = [pl.BlockSpec((TM, D_MODEL), lambda i: (jnp.minimum(i, PROMPT_TILES - 1), 0)),
                     pl.BlockSpec((DEC_BATCH, D_MODEL), lambda i: (0, 0))]
        row_args = rows
    else:
        row_specs = _row_tile_specs()
        x, gates, y_slots = rows
        row_args = (x, gates, y_slots, y_slots)
    return pl.pallas_call(
        functools.partial(_sgu_kernel, source=source, precise=precise),
        grid=(N_TILES,),
        in_specs=row_specs + [
                  _const_spec((1, D_MODEL)),
                  _const_spec((D_MODEL, 2 * SGU_WIDTH)),
                  _const_spec((D_MODEL, 2 * SGU_WIDTH)),
                  _const_spec((1, 2 * SGU_WIDTH)),
                  _const_spec((1, SGU_WIDTH)),
                  _const_spec((1, SGU_WIDTH)),
                  _const_spec((SGU_GROUPS, CHUNK, CHUNK)),
                  _const_spec((SGU_GROUPS, CHUNK, 1)),
                  _const_spec((SGU_WIDTH, D_MODEL)),
                  _const_spec((SGU_WIDTH, D_MODEL))],
        out_specs=[tile, vlast_spec],
        out_shape=[jax.ShapeDtypeStruct((T_PAD, D_MODEL), F32),
                   jax.ShapeDtypeStruct(((BATCH + 1) * CHUNK, SGU_WIDTH), F32)],
        scratch_shapes=[pltpu.VMEM((TM, SGU_WIDTH), BF16)],
        compiler_params=_params(),
        name="sgu_mixer",
    )(*row_args, g.reshape(1, -1), *_hi_lo(w_in, precise), b_in.reshape(1, -1), ln_g.reshape(1, -1),
      ln_b.reshape(1, -1), w_s, b_s.reshape(SGU_GROUPS, CHUNK, 1), *_hi_lo(w_out, precise))


def _ffn_kernel(x_ref, g_ref, wg_hi, wg_lo, wu_hi, wu_lo, wd_hi, wd_lo, xo_ref, *, precise):
    is_sample = pl.program_id(0) == N_TILES - 1

    def ffn(x, sample):
        h = _operand(_rms(x, g_ref[...]), sample and precise)
        a = _silu(_mm(h, wg_hi[...], wg_lo[...])) * _mm(h, wu_hi[...], wu_lo[...])
        return x + _mm(_operand(a, sample and precise), wd_hi[...], wd_lo[...])

    @pl.when(jnp.logical_not(is_sample))
    def _():
        xo_ref[...] = ffn(x_ref[...], sample=False)

    @pl.when(is_sample)
    def _():
        xo_ref[:DEC_BATCH, :] = ffn(x_ref[:DEC_BATCH, :], sample=True)
        xo_ref[DEC_BATCH:, :] = jnp.zeros((TM - DEC_BATCH, D_MODEL), F32)


def _ffn_layer(x, precise, g, w_gate, w_up, w_down):
    tile = pl.BlockSpec((TM, D_MODEL), lambda i: (i, 0))
    return pl.pallas_call(
        functools.partial(_ffn_kernel, precise=precise),
        grid=(N_TILES,),
        in_specs=[tile, _const_spec((1, D_MODEL))]
        + [_const_spec((D_MODEL, D_FF))] * 4 + [_const_spec((D_FF, D_MODEL))] * 2,
        out_specs=tile,
        out_shape=jax.ShapeDtypeStruct((T_PAD, D_MODEL), F32),
        compiler_params=_params(),
        name="dense_swiglu",
    )(x, g.reshape(1, -1), *_hi_lo(w_gate, precise), *_hi_lo(w_up, precise), *_hi_lo(w_down, precise))


def _qkv_kernel(x_ref, g_ref, w_hi, w_lo, q_ref, qs_ref, kv_ref, kvlast_ref, *, precise):
    is_sample = pl.program_id(0) == N_TILES - 1

    def qkv(x, sample):
        out = _mm(_operand(_rms(x, g_ref[...]), sample and precise), w_hi[...], w_lo[...])
        return out[:, :Q_DIM] * (HEAD_DIM ** -0.5), out[:, Q_DIM:]

    @pl.when(jnp.logical_not(is_sample))
    def _():
        q, kv = qkv(x_ref[...], sample=False)
        q_ref[...] = q.astype(BF16)
        kv_ref[...] = kv
        kvlast_ref[...] = kv[TM - WINDOW:]

    @pl.when(is_sample)
    def _():
        q, kv = qkv(x_ref[:DEC_BATCH, :], sample=True)
        qs_ref[...] = q
        kv_ref[:DEC_BATCH, :] = kv
        kv_ref[DEC_BATCH:, :] = jnp.zeros((TM - DEC_BATCH, 2 * KV_DIM), F32)
        kvlast_ref[...] = kv


def _qkv_layer(x, precise, g, w_qkv):
    tile = pl.BlockSpec((TM, D_MODEL), lambda i: (i, 0))
    return pl.pallas_call(
        functools.partial(_qkv_kernel, precise=precise),
        grid=(N_TILES,),
        in_specs=[tile, _const_spec((1, D_MODEL)),
                  _const_spec((D_MODEL, Q_DIM + 2 * KV_DIM)), _const_spec((D_MODEL, Q_DIM + 2 * KV_DIM))],
        out_specs=[pl.BlockSpec((TM, Q_DIM), lambda i: (jnp.minimum(i, PROMPT_TILES - 1), 0)),
                   pl.BlockSpec((DEC_BATCH, Q_DIM), lambda i: (0, 0)),
                   pl.BlockSpec((TM, 2 * KV_DIM), lambda i: (i, 0)),
                   pl.BlockSpec((WINDOW, 2 * KV_DIM),
                                lambda i: (jnp.where(i == N_TILES - 1, BATCH, i // TILES_PER_SEQ), 0))],
        out_shape=[jax.ShapeDtypeStruct((N_PROMPT, Q_DIM), BF16),
                   jax.ShapeDtypeStruct((DEC_BATCH, Q_DIM), F32),
                   jax.ShapeDtypeStruct((T_PAD, 2 * KV_DIM), F32),
                   jax.ShapeDtypeStruct(((BATCH + 1) * WINDOW, 2 * KV_DIM), F32)],
        compiler_params=_params(),
        name="swa_qkv",
    )(x, g.reshape(1, -1), *_hi_lo(w_qkv, precise))


def _half_masks_f32(x):
    lane = lax.broadcasted_iota(jnp.int32, x.shape, 1)
    low = lane < HEAD_DIM
    xr = pltpu.roll(x, HEAD_DIM, 1)
    return ((jnp.where(low, x, 0.0), jnp.where(low, 0.0, xr)),
            (jnp.where(low, xr, 0.0), jnp.where(low, 0.0, x)))


def _half_masks(x):
    return tuple(tuple(m.astype(BF16) for m in pair) for pair in _half_masks_f32(x))


def _swa_prompt_kernel(sink_ref, q_ref, kv_ref, kvp_ref, wg_ref, wu_ref, wd_ref,
                       o_ref, wg_out, wu_out, wd_out, bias_ref):
    wg_out[...] = wg_ref[...].astype(BF16)
    wu_out[...] = wu_ref[...].astype(BF16)
    wd_out[...] = wd_ref[...].astype(BF16)
    i = pl.program_id(0)
    t = lax.broadcasted_iota(jnp.int32, (BLOCK, BLOCK), 0)
    c = lax.broadcasted_iota(jnp.int32, (BLOCK, BLOCK), 1)
    own = c <= t
    diag = c == t

    @pl.when(i == 0)
    def _():
        dist = jnp.where(own, t - c, BLOCK + t - c).astype(F32)
        for hd in range(N_HEADS):
            bias_ref[hd] = _SLOPES[hd] * dist

    first_tile = (i % TILES_PER_SEQ) == 0
    kv_all = jnp.concatenate([kvp_ref[...], kv_ref[...]], axis=0)
    k_blk, v_blk, vf_blk = [], [], []
    for b in range(CHUNKS_PER_TILE + 1):
        blk = kv_all[b * BLOCK:(b + 1) * BLOCK]
        k_blk.append(_half_masks(blk[:, :KV_DIM]))
        v_blk.append(_half_masks(blk[:, KV_DIM:]))
        vf_blk.append(_half_masks_f32(blk[:, KV_DIM:]))
    has_prev = (jnp.zeros((BLOCK, BLOCK), jnp.int32) + jnp.where(first_tile, 0, 1)) == 1

    for b in range(CHUNKS_PER_TILE):
        rows = slice(b * BLOCK, (b + 1) * BLOCK)
        for p in range(N_PAIRS):
            kvh = (2 * p) // GQA_GROUP
            qp = q_ref[rows, p * LANES:(p + 1) * LANES]
            acc = None
            for par in range(2):
                hd = 2 * p + par
                sink = sink_ref[hd]
                keys = jnp.concatenate([k_blk[b][kvh][par], k_blk[b + 1][kvh][par]], axis=0)
                both = _dot_t(qp, keys)
                l_prev, l_own = both[:, :BLOCK], both[:, BLOCK:]
                if b == 0:
                    l_prev = jnp.where(has_prev, l_prev, -jnp.inf)
                logits = jnp.where(own, l_own, l_prev) - bias_ref[hd]
                extra = (jnp.sum(jnp.where(diag, l_prev, 0.0), axis=-1, keepdims=True)
                         - _SLOPES[hd] * BLOCK)
                m = jnp.maximum(jnp.maximum(jnp.max(logits, axis=-1, keepdims=True), extra), sink)
                e = jnp.exp(logits - m)
                e_extra = jnp.exp(extra - m)
                denom = jnp.sum(e, axis=-1, keepdims=True) + e_extra + jnp.exp(sink - m)
                probs = jnp.concatenate([jnp.where(own, 0.0, e).astype(BF16),
                                         jnp.where(own, e, 0.0).astype(BF16)], axis=1)
                vals = jnp.concatenate([v_blk[b][kvh][par], v_blk[b + 1][kvh][par]], axis=0)
                part = (_dot(probs, vals) + e_extra * vf_blk[b][kvh][par]) * (1.0 / denom)
                acc = part if acc is None else acc + part
            o_ref[rows, p * LANES:(p + 1) * LANES] = acc.astype(BF16)


def _swa_prompt(q, kv, sinks, layer, w_gate, w_up, w_down):
    n_layers = w_gate.shape[0]
    up_rows = N_EXPERTS * D_MODEL // PROMPT_TILES
    down_rows = N_EXPERTS * D_FF_EXPERT // PROMPT_TILES
    up_spec = pl.BlockSpec((None, up_rows, D_FF_EXPERT), lambda i: (layer, i, 0))
    down_spec = pl.BlockSpec((None, down_rows, D_MODEL), lambda i: (layer, i, 0))
    o, wg, wu, wd = pl.pallas_call(
        _swa_prompt_kernel,
        grid=(PROMPT_TILES,),
        in_specs=[pl.BlockSpec(memory_space=pltpu.SMEM),
                  pl.BlockSpec((TM, Q_DIM), lambda i: (i, 0)),
                  pl.BlockSpec((TM, 2 * KV_DIM), lambda i: (i, 0)),
                  pl.BlockSpec((BLOCK, 2 * KV_DIM),
                               lambda i: (jnp.maximum(i * CHUNKS_PER_TILE - 1, 0), 0)),
                  up_spec, up_spec, down_spec],
        out_specs=[pl.BlockSpec((TM, Q_DIM), lambda i: (i, 0)),
                   pl.BlockSpec((up_rows, D_FF_EXPERT), lambda i: (i, 0)),
                   pl.BlockSpec((up_rows, D_FF_EXPERT), lambda i: (i, 0)),
                   pl.BlockSpec((down_rows, D_MODEL), lambda i: (i, 0))],
        out_shape=[jax.ShapeDtypeStruct((N_PROMPT, Q_DIM), BF16),
                   jax.ShapeDtypeStruct((N_EXPERTS * D_MODEL, D_FF_EXPERT), BF16),
                   jax.ShapeDtypeStruct((N_EXPERTS * D_MODEL, D_FF_EXPERT), BF16),
                   jax.ShapeDtypeStruct((N_EXPERTS * D_FF_EXPERT, D_MODEL), BF16)],
        scratch_shapes=[pltpu.VMEM((N_HEADS, BLOCK, BLOCK), F32)],
        compiler_params=_params(),
        name="swa_prompt",
    )(sinks, q, kv, kv,
      w_gate.reshape(n_layers, N_EXPERTS * D_MODEL, D_FF_EXPERT),
      w_up.reshape(n_layers, N_EXPERTS * D_MODEL, D_FF_EXPERT),
      w_down.reshape(n_layers, N_EXPERTS * D_FF_EXPERT, D_MODEL))
    return (o, wg.reshape(N_EXPERTS, D_MODEL, D_FF_EXPERT), wu.reshape(N_EXPERTS, D_MODEL, D_FF_EXPERT),
            wd.reshape(N_EXPERTS, D_FF_EXPERT, D_MODEL))


SAMPLE_TILE = 32


def _swa_sample_kernel(q_ref, kvn_ref, ck_ref, cv_ref, slope_ref, sink_ref, o_ref, *, precise):
    shape = (SAMPLE_TILE, N_PAIRS, LANES)
    lane = lax.broadcasted_iota(jnp.int32, shape, 2)
    pair = lax.broadcasted_iota(jnp.int32, shape, 1)
    low = lane < HEAD_DIM
    kv0 = pair < (N_PAIRS // 2)

    def swap(x):
        return pltpu.roll(x, HEAD_DIM, 2)

    def three_pass(dims, a, b):
        dot = lambda u, v: lax.dot_general(u, v, dims, preferred_element_type=F32)
        if not precise:
            return dot(a[0], b[0])
        return dot(a[0], b[0]) + (dot(a[1], b[0]) + dot(a[0], b[1]))

    qk_dims = (((2,), (2,)), ((0,), (0,)))
    pv_dims = (((2,), (1,)), ((0,), (0,)))

    q = q_ref[...]
    q_even = jnp.where(low, q, 0.0)
    q_odd = jnp.where(low, 0.0, q)
    q_al = (jnp.where(kv0, q_even, swap(q_even)), jnp.where(kv0, swap(q_odd), q_odd))

    ck = _split(ck_ref[...])
    cv = _split(cv_ref[...])
    kvn = kvn_ref[...]
    k_new = kvn[:, :, :KV_DIM]
    v_new = kvn[:, :, KV_DIM:]
    r = lax.broadcasted_iota(jnp.int32, (SAMPLE_TILE, N_PAIRS, WINDOW), 2)
    dist = (WINDOW - r).astype(F32)

    outs = []
    for par in range(2):
        qa = q_al[par]
        slope = slope_ref[par]
        sink = sink_ref[par]
        logits = three_pass(qk_dims, _split(qa), ck) - slope * dist
        l_self = jnp.sum(qa * k_new, axis=-1, keepdims=True)
        m = jnp.maximum(jnp.maximum(jnp.max(logits, axis=-1, keepdims=True), l_self), sink)
        e = jnp.exp(logits - m)
        e_self = jnp.exp(l_self - m)
        inv = 1.0 / (jnp.sum(e, axis=-1, keepdims=True) + e_self + jnp.exp(sink - m))
        o = three_pass(pv_dims, _split(e * inv), cv) + (e_self * inv) * v_new
        outs.append(o)
    o_even = jnp.where(kv0, outs[0], swap(outs[0]))
    o_odd = jnp.where(kv0, swap(outs[1]), outs[1])
    o_ref[...] = jnp.where(low, o_even, o_odd)


def _swa_sample(q_s, kv_s, cache_k, cache_v, sinks, precise):
    rows = cache_k.shape[1]
    slopes = np.asarray(_SLOPES, np.float32).reshape(N_PAIRS, 2).T.reshape(2, N_PAIRS, 1)
    sink_arr = sinks.astype(F32).reshape(N_PAIRS, 2).T.reshape(2, N_PAIRS, 1)
    blk = lambda *shape: pl.BlockSpec((SAMPLE_TILE,) + shape, lambda i: (i,) + (0,) * len(shape))
    o3 = pl.pallas_call(
        functools.partial(_swa_sample_kernel, precise=precise),
        grid=(DEC_BATCH // SAMPLE_TILE,),
        in_specs=[blk(N_PAIRS, LANES), blk(1, 2 * KV_DIM), blk(rows, KV_DIM), blk(rows, KV_DIM),
                  _const_spec((2, N_PAIRS, 1)), _const_spec((2, N_PAIRS, 1))],
        out_specs=blk(N_PAIRS, LANES),
        out_shape=jax.ShapeDtypeStruct((DEC_BATCH, N_PAIRS, LANES), F32),
        compiler_params=_params(),
        name="swa_sample",
    )(q_s.reshape(DEC_BATCH, N_PAIRS, LANES), kv_s.reshape(DEC_BATCH, 1, 2 * KV_DIM),
      cache_k.reshape(DEC_BATCH, rows, KV_DIM), cache_v.reshape(DEC_BATCH, rows, KV_DIM),
      jnp.asarray(slopes), sink_arr)
    return o3.reshape(DEC_BATCH, Q_DIM)


def _proj_route_kernel(x_ref, op_ref, os_ref, wo_hi, wo_lo, g_ref, wr_hi, wr_lo, xo_ref, idx_ref,
                       gate_ref, rank_ref, cnt_ref, carry_ref, xnew_ref, *, precise):
    i = pl.program_id(0)
    is_sample = i == N_TILES - 1

    @pl.when(i == 0)
    def _():
        carry_ref[...] = jnp.zeros_like(carry_ref)

    @pl.when(jnp.logical_not(is_sample))
    def _():
        xnew_ref[...] = x_ref[...] + _dot(op_ref[...], wo_hi[...])

    @pl.when(is_sample)
    def _():
        xnew_ref[:DEC_BATCH, :] = x_ref[:DEC_BATCH, :] + _mm(_operand(os_ref[...], precise), wo_hi[...], wo_lo[...])
        xnew_ref[DEC_BATCH:, :] = jnp.zeros((TM - DEC_BATCH, D_MODEL), F32)

    x = xnew_ref[...]
    _store_row_tiles(xo_ref, x)
    h = _rms(x, g_ref[...])
    logits = _mm(_split(h), wr_hi[...], wr_lo[...])
    lane = lax.broadcasted_iota(jnp.int32, (TM, LANES), 1)
    logits = jnp.where(lane < N_EXPERTS, logits, -jnp.inf)
    m0 = jnp.max(logits, axis=-1, keepdims=True)
    i0 = jnp.min(jnp.where(logits == m0, lane, LANES), axis=-1, keepdims=True)
    rest = jnp.where(lane == i0, -jnp.inf, logits)
    m1 = jnp.max(rest, axis=-1, keepdims=True)
    i1 = jnp.min(jnp.where(rest == m1, lane, LANES), axis=-1, keepdims=True)
    e1 = jnp.exp(m1 - m0)
    g0 = 1.0 / (1.0 + e1)
    g1 = e1 * g0
    idx_ref[...] = jnp.concatenate([i0, i1], axis=1)
    gate_ref[...] = jnp.concatenate([g0, g1], axis=1)

    row_id = i * TM + lax.broadcasted_iota(jnp.int32, (TM, 1), 0)
    onehot = jnp.where(((lane == i0) | (lane == i1)) & (row_id < T_REAL), 1.0, 0.0)
    r = lax.broadcasted_iota(jnp.int32, (TM, TM), 0)
    c = lax.broadcasted_iota(jnp.int32, (TM, TM), 1)
    before = jnp.where(c < r, 1.0, 0.0).astype(BF16)
    ranks = _dot(before, onehot.astype(BF16)) + carry_ref[...]
    r0 = jnp.sum(jnp.where(lane == i0, ranks, 0.0), axis=-1, keepdims=True)
    r1 = jnp.sum(jnp.where(lane == i1, ranks, 0.0), axis=-1, keepdims=True)
    rank_ref[...] = jnp.concatenate([r0, r1], axis=1).astype(jnp.int32)
    carry_ref[...] = carry_ref[...] + jnp.sum(onehot, axis=0, keepdims=True)
    cnt_ref[...] = carry_ref[...].astype(jnp.int32)


def _proj_route(x, o_prompt, o_sample, precise, w_o, g, w_router):
    tile = pl.BlockSpec((TM, D_MODEL), lambda i: (i, 0))
    pair = pl.BlockSpec((TM, 2), lambda i: (i, 0))
    wr = jnp.zeros((D_MODEL, LANES), F32).at[:, :N_EXPERTS].set(w_router)
    return pl.pallas_call(
        functools.partial(_proj_route_kernel, precise=precise),
        grid=(N_TILES,),
        in_specs=[tile,
                  pl.BlockSpec((TM, Q_DIM), lambda i: (jnp.minimum(i, PROMPT_TILES - 1), 0)),
                  pl.BlockSpec((DEC_BATCH, Q_DIM), lambda i: (0, 0)),
                  _const_spec((Q_DIM, D_MODEL)), _const_spec((Q_DIM, D_MODEL)),
                  _const_spec((1, D_MODEL)), _const_spec((D_MODEL, LANES)),
                  _const_spec((D_MODEL, LANES))],
        out_specs=[pl.BlockSpec((TM * SUBLANES, LANES), lambda i: (i, 0)), pair, pair, pair,
                   pl.BlockSpec((1, LANES), lambda i: (0, 0))],
        out_shape=[jax.ShapeDtypeStruct((T_PAD * SUBLANES, LANES), F32),
                   jax.ShapeDtypeStruct((T_PAD, 2), jnp.int32),
                   jax.ShapeDtypeStruct((T_PAD, 2), F32),
                   jax.ShapeDtypeStruct((T_PAD, 2), jnp.int32),
                   jax.ShapeDtypeStruct((1, LANES), jnp.int32)],
        scratch_shapes=[pltpu.VMEM((1, LANES), F32), pltpu.VMEM((TM, D_MODEL), F32)],
        compiler_params=_params(),
        name="proj_route",
    )(x, o_prompt, o_sample, *_hi_lo(w_o, precise), g.reshape(1, -1), *_hi_lo(wr))


def _tile_at(start):
    return pl.ds(start if isinstance(start, int) else pl.multiple_of(start, SUBLANES), SUBLANES)


def _gather_row(x_hbm, xbuf, sem, slot, src, r):
    return pltpu.make_async_copy(x_hbm.at[_tile_at(src), :], xbuf.at[slot, _tile_at(r * SUBLANES), :],
                                 sem.at[slot])


def _scatter_row(obuf, y_hbm, sem, slot, r, dst):
    return pltpu.make_async_copy(obuf.at[slot, _tile_at(r * SUBLANES), :], y_hbm.at[_tile_at(dst), :],
                                 sem.at[slot])


def _gather_wait(x_hbm, xbuf, sem, slot):
    pltpu.make_async_copy(x_hbm.at[pl.ds(0, TMM * SUBLANES), :], xbuf.at[slot], sem.at[slot]).wait()


def _scatter_wait(obuf, y_hbm, sem, slot):
    pltpu.make_async_copy(obuf.at[slot], y_hbm.at[pl.ds(0, TMM * SUBLANES), :], sem.at[slot]).wait()


def _slot_src(slot_ref, q):
    return slot_ref[q]


def _slot_dst(slot_ref, q):
    return slot_ref[SLOT_ENTRIES + q]


def _moe_kernel(te_ref, nu_ref, slot_ref, x_hbm, g_ref, wg_ref, wu_ref, wd_ref, y_hbm,
                xbuf, hbuf, obuf, gsem, ssem):
    i = pl.program_id(0)
    n_used = nu_ref[0]
    slot = i % 2
    other = 1 - slot

    @pl.when(i < n_used)
    def _():
        @pl.when(i == 0)
        def _():
            def start(r, carry):
                _gather_row(x_hbm, xbuf, gsem, 0, _slot_src(slot_ref, TMM + r), r).start()
                return carry
            lax.fori_loop(0, TMM, start, 0)
            obuf[...] = jnp.zeros_like(obuf)
            n_pad = (T_PAD - T_REAL) * SUBLANES
            for s in range(2):
                pad = pltpu.make_async_copy(
                    obuf.at[s, pl.ds(0, n_pad), :],
                    y_hbm.at[pl.ds((s * T_PAD + T_REAL) * SUBLANES, n_pad), :],
                    ssem.at[s])
                pad.start()
                pad.wait()
                spare = pltpu.make_async_copy(
                    obuf.at[s],
                    y_hbm.at[pl.ds((2 * T_PAD + s * TMM) * SUBLANES, TMM * SUBLANES), :],
                    ssem.at[s])
                spare.start()
                if s == 1:
                    spare.wait()

        _gather_wait(x_hbm, xbuf, gsem, slot)
        xin = xbuf.at[slot]
        chunks = [xin[_chunk(s, TMM), :] for s in range(ROW_CHUNKS)]
        ssq = chunks[0] * chunks[0]
        for c in chunks[1:]:
            ssq = ssq + c * c
        scale = lax.rsqrt(jnp.sum(ssq, axis=-1, keepdims=True) * (1.0 / D_MODEL) + EPS)
        for s, c in enumerate(chunks):
            cols = slice(s * LANES, (s + 1) * LANES)
            hbuf[:, cols] = (c * scale * g_ref[:, cols]).astype(BF16)

        nxt = (i + 2) * TMM
        prv = i * TMM
        for r in range(TMM):
            _gather_row(x_hbm, xbuf, gsem, other, _slot_src(slot_ref, nxt + r), r).start()
            _scatter_row(obuf, y_hbm, ssem, other, r, _slot_dst(slot_ref, prv + r)).start()

        h = hbuf[...]
        a = (_silu(_dot(h, wg_ref[...])) * _dot(h, wu_ref[...])).astype(BF16)
        out = _dot(a, wd_ref[...])
        _scatter_wait(obuf, y_hbm, ssem, slot)
        _store_row_tiles(obuf.at[slot], out)

        @pl.when(i == n_used - 1)
        def _():
            _gather_wait(x_hbm, xbuf, gsem, other)
            _scatter_wait(obuf, y_hbm, ssem, other)
            base = (i + 1) * TMM

            def start(r, carry):
                _scatter_row(obuf, y_hbm, ssem, slot, r, _slot_dst(slot_ref, base + r)).start()
                return carry
            lax.fori_loop(0, TMM, start, 0)
            _scatter_wait(obuf, y_hbm, ssem, slot)


def _moe_layer(x, g, tile_expert, n_used, slots, w_gate, w_up, w_down):
    def expert_weights(shape):
        return pl.BlockSpec((None,) + shape, lambda i, te, nu, tok: (te[i], 0, 0),
                            pipeline_mode=pl.Buffered(1))

    grid_spec = pltpu.PrefetchScalarGridSpec(
        num_scalar_prefetch=3,
        grid=(NT_MOE,),
        in_specs=[pl.BlockSpec(memory_space=pl.ANY),
                  pl.BlockSpec((1, D_MODEL), lambda i, te, nu, tok: (0, 0)),
                  expert_weights((D_MODEL, D_FF_EXPERT)),
                  expert_weights((D_MODEL, D_FF_EXPERT)),
                  expert_weights((D_FF_EXPERT, D_MODEL))],
        out_specs=pl.BlockSpec(memory_space=pl.ANY),
        scratch_shapes=[pltpu.VMEM((2, TMM * SUBLANES, LANES), F32),
                        pltpu.VMEM((TMM, D_MODEL), BF16),
                        pltpu.VMEM((2, TMM * SUBLANES, LANES), F32),
                        pltpu.SemaphoreType.DMA((2,)),
                        pltpu.SemaphoreType.DMA((2,))],
    )
    return pl.pallas_call(
        _moe_kernel,
        grid_spec=grid_spec,
        out_shape=jax.ShapeDtypeStruct((Y_ROWS * SUBLANES, LANES), F32),
        compiler_params=_params(1),
        name="moe_experts",
    )(tile_expert, n_used, slots, x, g.reshape(1, -1), w_gate, w_up, w_down)


def _final_kernel(x_ref, gate_ref, y0_ref, y1_ref, gf_ref, yp_ref, ys_ref):
    i = pl.program_id(0)
    y = _rms(_combined_rows(x_ref, gate_ref, y0_ref, y1_ref), gf_ref[...])

    @pl.when(i < PROMPT_TILES)
    def _():
        yp_ref[...] = y

    @pl.when(i == N_TILES - 1)
    def _():
        ys_ref[...] = y[:DEC_BATCH]


def _final_combine(x, gates, y_slots, g_final):
    return pl.pallas_call(
        _final_kernel,
        grid=(N_TILES,),
        in_specs=_row_tile_specs() + [_const_spec((1, D_MODEL))],
        out_specs=[pl.BlockSpec((TM, D_MODEL), lambda i: (jnp.minimum(i, PROMPT_TILES - 1), 0)),
                   pl.BlockSpec((DEC_BATCH, D_MODEL), lambda i: (0, 0))],
        out_shape=[jax.ShapeDtypeStruct((N_PROMPT, D_MODEL), F32),
                   jax.ShapeDtypeStruct((DEC_BATCH, D_MODEL), F32)],
        compiler_params=_params(),
        name="final_combine",
    )(x, gates, y_slots, y_slots, g_final.reshape(1, -1))


def _padding_slot_entries():
    p = np.arange(-TMM, P_TOTAL + TMM)
    spare_row = ((p // TMM) % 2) * TMM + p % TMM
    return np.concatenate([spare_row, 2 * T_PAD + spare_row]).astype(np.int32) * SUBLANES


FILL_UNROLL = 8


def _slot_fill_kernel(pos_ref, init_hbm, slots_hbm, buf, sem):
    load = pltpu.make_async_copy(init_hbm, buf, sem)
    load.start()
    load.wait()

    def body(j, carry):
        for u in range(FILL_UNROLL):
            t = j * FILL_UNROLL + u
            for k in range(2):
                q = TMM + pos_ref[2 * t + k]
                buf[q] = t * SUBLANES
                buf[SLOT_ENTRIES + q] = (k * T_PAD + t) * SUBLANES
        return carry

    lax.fori_loop(0, T_REAL // FILL_UNROLL, body, 0)
    store = pltpu.make_async_copy(buf, slots_hbm, sem)
    store.start()
    store.wait()


def _slot_fill(pos):
    return pl.pallas_call(
        _slot_fill_kernel,
        in_specs=[pl.BlockSpec(memory_space=pltpu.SMEM), pl.BlockSpec(memory_space=pl.ANY)],
        out_specs=pl.BlockSpec(memory_space=pl.ANY),
        out_shape=jax.ShapeDtypeStruct((2 * SLOT_ENTRIES,), jnp.int32),
        scratch_shapes=[pltpu.SMEM((2 * SLOT_ENTRIES,), jnp.int32), pltpu.SemaphoreType.DMA(())],
        name="slot_fill",
    )(pos.reshape(-1), jnp.asarray(_padding_slot_entries()))


def _slot_plan(idx, rank, counts):
    cnt = counts[0, :N_EXPERTS]
    tiles = (cnt + TMM - 1) // TMM
    tile_end = jnp.cumsum(tiles)
    start = (tile_end - tiles) * TMM
    n_used = tile_end[-1:].astype(jnp.int32)
    pos = (start[idx] + rank).astype(jnp.int32)
    slots = _slot_fill(pos)
    tile_ids = jnp.arange(NT_MOE, dtype=jnp.int32)
    tile_expert = jnp.minimum(jnp.sum(tile_end[None, :] <= tile_ids[:, None], axis=1),
                              N_EXPERTS - 1).astype(jnp.int32)
    last_expert = tile_expert[jnp.maximum(n_used[0] - 1, 0)]
    tile_expert = jnp.where(tile_ids < n_used[0], tile_expert, last_expert)
    return slots, tile_expert, n_used


def kernel(x_prompt, x_sample, cache_swa_k, cache_swa_v, norm_mix, norm_ffn, norm_final,
           sgu_w_in, sgu_b_in, sgu_ln_g, sgu_ln_b, sgu_w_s, sgu_b_s, sgu_w_out,
           attn_w_qkv, attn_sinks, attn_w_o,
           ffn_w_gate, ffn_w_up, ffn_w_down,
           moe_w_router, moe_w_gate, moe_w_up, moe_w_down):
    sgu_v_p, sgu_v_s, k_p, v_p, k_s, v_s = [], [], [], [], [], []
    rows_p = min(WINDOW, SEQ)
    rows, source = (x_prompt.reshape(N_PROMPT, D_MODEL), x_sample.reshape(DEC_BATCH, D_MODEL)), "inputs"
    for i in range(DEPTH):
        j = i // 2
        precise = i <= 1
        if i % 2 == 0:
            x, vlast = _sgu_layer(rows, source, precise, norm_mix[i], sgu_w_in[j], sgu_b_in[j], sgu_ln_g[j],
                                  sgu_ln_b[j], sgu_w_s[j], sgu_b_s[j], sgu_w_out[j])
            sgu_v_p.append(vlast[:BATCH * CHUNK].reshape(BATCH, CHUNK, SGU_WIDTH))
            sgu_v_s.append(vlast[BATCH * CHUNK:].reshape(DEC_BATCH, 1, SGU_WIDTH))
            x = _ffn_layer(x, precise, norm_ffn[i], ffn_w_gate[j], ffn_w_up[j], ffn_w_down[j])
        else:
            q, q_s, kv, kv_last = _qkv_layer(x, precise, norm_mix[i], attn_w_qkv[j])
            kv_p = kv_last[:BATCH * rows_p].reshape(BATCH, rows_p, 2 * KV_DIM)
            k_p.append(kv_p[..., :KV_DIM].reshape(BATCH, rows_p, N_KV_HEADS, HEAD_DIM))
            v_p.append(kv_p[..., KV_DIM:].reshape(BATCH, rows_p, N_KV_HEADS, HEAD_DIM))
            kv_s = kv_last[BATCH * rows_p:]
            k_new = kv_s[:, None, :KV_DIM].reshape(DEC_BATCH, 1, N_KV_HEADS, HEAD_DIM)
            v_new = kv_s[:, None, KV_DIM:].reshape(DEC_BATCH, 1, N_KV_HEADS, HEAD_DIM)
            k_s.append(jnp.concatenate([cache_swa_k[j][:, 1:], k_new], axis=1))
            v_s.append(jnp.concatenate([cache_swa_v[j][:, 1:], v_new], axis=1))

            o, moe_wg, moe_wu, moe_wd = _swa_prompt(q, kv, attn_sinks[j].astype(F32), j,
                                                    moe_w_gate, moe_w_up, moe_w_down)
            o_s = _swa_sample(q_s, kv_s, cache_swa_k[j], cache_swa_v[j], attn_sinks[j], precise)
            x, idx, gates, rank, counts = _proj_route(x, o, o_s, precise, attn_w_o[j], norm_ffn[i],
                                                      moe_w_router[j])
            slots, tile_expert, n_used = _slot_plan(idx, rank, counts)
            y_slots = _moe_layer(x, norm_ffn[i], tile_expert, n_used, slots, moe_wg, moe_wu, moe_wd)
            rows, source = (x, gates, y_slots), "experts"
    y_prompt, y_sample = _final_combine(*rows, norm_final)
    y_prompt = y_prompt.reshape(BATCH, SEQ, D_MODEL)
    y_sample = y_sample.reshape(DEC_BATCH, 1, D_MODEL)
    return (y_prompt, y_sample, jnp.stack(sgu_v_p), jnp.stack(sgu_v_s),
            jnp.stack(k_p), jnp.stack(v_p), jnp.stack(k_s), jnp.stack(v_s))
```

```python
import functools

import numpy as np
import jax
import jax.numpy as jnp
from jax import lax
from jax.experimental import pallas as pl
from jax.experimental.pallas import tpu as pltpu

D_MODEL = 1024
BATCH = 4
SEQ = 4096
DEPTH = 4
DEC_BATCH = 128
PAST_LEN = 8192
CHUNK = 128
SGU_WIDTH = 2 * D_MODEL
SGU_GROUPS = 8
SGU_GROUP_DIM = SGU_WIDTH // SGU_GROUPS
WINDOW = 128
BLOCK = 128
HEAD_DIM = 64
N_HEADS = D_MODEL // HEAD_DIM
N_KV_HEADS = 2
GQA_GROUP = N_HEADS // N_KV_HEADS
Q_DIM = N_HEADS * HEAD_DIM
KV_DIM = N_KV_HEADS * HEAD_DIM
D_FF = 2816
N_EXPERTS = 8
D_FF_EXPERT = 3584
EPS = 1e-6
LN_EPS = 1e-5

F32 = jnp.float32
BF16 = jnp.bfloat16

LANES = 128
TM = 512
N_PROMPT = BATCH * SEQ
T_REAL = N_PROMPT + DEC_BATCH
N_TILES = -(-T_REAL // TM)
T_PAD = N_TILES * TM
PROMPT_TILES = N_PROMPT // TM
TILES_PER_SEQ = SEQ // TM
CHUNKS_PER_TILE = TM // CHUNK
N_PAIRS = N_HEADS // 2

TMM = 512
N_SLOTS = 2 * T_REAL
NT_MOE = (N_SLOTS + N_EXPERTS * (TMM - 1)) // TMM + 1
P_TOTAL = NT_MOE * TMM
Y_ROWS = 2 * T_PAD + 2 * TMM
SLOT_ENTRIES = P_TOTAL + 2 * TMM

VMEM_LIMIT = 56 * 1024 * 1024

_SLOPES = [2.0 ** (-8.0 * (h + 1) / N_HEADS) for h in range(N_HEADS)]


def _rms(x, g):
    return x * lax.rsqrt(jnp.mean(x * x, axis=-1, keepdims=True) + EPS) * g


def _gelu(x):
    k = -2.0 * np.sqrt(2.0 / np.pi) * np.log2(np.e)
    t = (x * x) * np.float32(0.044715 * k) + np.float32(k)
    return x * (1.0 / (1.0 + jnp.exp2(x * t)))


def _silu(x):
    return x * (1.0 / (1.0 + jnp.exp(-x)))


def _dot(a, b):
    return jnp.dot(a, b, preferred_element_type=F32)


def _split(x):
    hi = x.astype(BF16)
    return hi, (x - hi.astype(F32)).astype(BF16)


def _operand(x, precise):
    return _split(x) if precise else x.astype(BF16)


def _mm(x, w_hi, w_lo):
    if not isinstance(x, tuple):
        return _dot(x, w_hi)
    x_hi, x_lo = x
    return _dot(x_hi, w_hi) + (_dot(x_lo, w_hi) + _dot(x_hi, w_lo))


def _hi_lo(w, precise=True):
    if not precise:
        hi = w.astype(BF16)
        return hi, hi
    hi = lax.optimization_barrier(w.astype(BF16))
    return hi, (w - hi.astype(F32)).astype(BF16)


def _dot_t(a, b):
    return lax.dot_general(a, b, (((1,), (1,)), ((), ())), preferred_element_type=F32)


SUBLANES = 8
ROW_CHUNKS = D_MODEL // LANES


def _chunk(s, rows):
    return pl.ds(s, rows, stride=SUBLANES)


def _store_row_tiles(ref, x):
    rows = x.shape[0]
    for s in range(ROW_CHUNKS):
        ref[_chunk(s, rows), :] = x[:, s * LANES:(s + 1) * LANES]


def _const_spec(shape):
    nd = len(shape)
    return pl.BlockSpec(shape, lambda *_: (0,) * nd, pipeline_mode=pl.Buffered(1))


def _params(n_axes=1):
    return pltpu.CompilerParams(dimension_semantics=("arbitrary",) * n_axes,
                                vmem_limit_bytes=VMEM_LIMIT)


def _combined_rows(x_ref, gate_ref, y0_ref, y1_ref, rows=TM):
    gate = gate_ref[:rows, :]
    g0, g1 = gate[:, 0:1], gate[:, 1:2]
    return jnp.concatenate(
        [x_ref[_chunk(s, rows), :] + (g0 * y0_ref[_chunk(s, rows), :] + g1 * y1_ref[_chunk(s, rows), :])
         for s in range(ROW_CHUNKS)], axis=1)


def _sgu_kernel(*refs, source, precise):
    i = pl.program_id(0)
    is_sample = i == N_TILES - 1
    n_src = 2 if source == "inputs" else 4
    src, refs = refs[:n_src], refs[n_src:]
    (g_ref, win_hi, win_lo, bin_ref, lng_ref, lnb_ref, ws_ref, bs_ref, wout_hi, wout_lo,
     xo_ref, vlast_ref, ug_ref) = refs

    def mix(x, sample):
        h = _operand(_rms(x, g_ref[...]), sample and precise)
        v = _gelu(_mm(h, win_hi[:, SGU_WIDTH:], win_lo[:, SGU_WIDTH:]) + bin_ref[:, SGU_WIDTH:])
        mu = jnp.mean(v, axis=-1, keepdims=True)
        vc = v - mu
        var = jnp.mean(vc * vc, axis=-1, keepdims=True)
        vn = vc * lax.rsqrt(var + LN_EPS) * lng_ref[...] + lnb_ref[...]
        vb = vn.astype(BF16)
        row = lax.broadcasted_iota(jnp.int32, (CHUNK, CHUNK), 0)
        col = lax.broadcasted_iota(jnp.int32, (CHUNK, CHUNK), 1)
        y = jnp.zeros_like(x)
        for g in range(SGU_GROUPS):
            lo, hi = g * SGU_GROUP_DIM, (g + 1) * SGU_GROUP_DIM
            u = _gelu(_mm(h, win_hi[:, lo:hi], win_lo[:, lo:hi]) + bin_ref[:, lo:hi])
            if sample:
                gate = vn[:, lo:hi] * ws_ref[g][0:1, 0:1] + bs_ref[g][0:1, 0:1]
                y = y + _mm(_operand(u * gate, precise), wout_hi[lo:hi, :], wout_lo[lo:hi, :])
            else:
                w_tril = jnp.where(row >= col, ws_ref[g], 0.0).astype(BF16)
                gate = jnp.concatenate(
                    [_dot(w_tril, vb[c * CHUNK:(c + 1) * CHUNK, lo:hi]) + bs_ref[g]
                     for c in range(x.shape[0] // CHUNK)], axis=0)
                ug_ref[:, lo:hi] = (u * gate).astype(BF16)
        if not sample:
            y = _dot(ug_ref[...], wout_hi[...])
        return x + y, vn

    @pl.when(jnp.logical_not(is_sample))
    def _():
        x = src[0][...] if source == "inputs" else _combined_rows(*src)
        x_new, vn = mix(x, sample=False)
        xo_ref[...] = x_new
        vlast_ref[...] = vn[TM - CHUNK:]

    @pl.when(is_sample)
    def _():
        x = src[1][...] if source == "inputs" else _combined_rows(*src, rows=DEC_BATCH)
        x_new, vn = mix(x, sample=True)
        xo_ref[:DEC_BATCH, :] = x_new
        xo_ref[DEC_BATCH:, :] = jnp.zeros((TM - DEC_BATCH, D_MODEL), F32)
        vlast_ref[...] = vn


def _row_tile_specs():
    return [pl.BlockSpec((TM * SUBLANES, LANES), lambda i: (i, 0)),
            pl.BlockSpec((TM, 2), lambda i: (i, 0)),
            pl.BlockSpec((TM * SUBLANES, LANES), lambda i: (i, 0)),
            pl.BlockSpec((TM * SUBLANES, LANES), lambda i: (N_TILES + i, 0))]


def _sgu_layer(rows, source, precise, g, w_in, b_in, ln_g, ln_b, w_s, b_s, w_out):
    tile = pl.BlockSpec((TM, D_MODEL), lambda i: (i, 0))
    vlast_spec = pl.BlockSpec(
        (CHUNK, SGU_WIDTH),
        lambda i: (jnp.where(i == N_TILES - 1, BATCH, i // TILES_PER_SEQ), 0))
    if source == "inputs":
        row_specs = [pl.BlockSpec((TM, D_MODEL), lambda i: (jnp.minimum(i, PROMPT_TILES - 1), 0)),
                     pl.BlockSpec((DEC_BATCH, D_MODEL), lambda i: (0, 0))]
        row_args = rows
    else:
        row_specs = _row_tile_specs()
        x, gates, y_slots = rows
        row_args = (x, gates, y_slots, y_slots)
    return pl.pallas_call(
        functools.partial(_sgu_kernel, source=source, precise=precise),
        grid=(N_TILES,),
        in_specs=row_specs + [
                  _const_spec((1, D_MODEL)),
                  _const_spec((D_MODEL, 2 * SGU_WIDTH)),
                  _const_spec((D_MODEL, 2 * SGU_WIDTH)),
                  _const_spec((1, 2 * SGU_WIDTH)),
                  _const_spec((1, SGU_WIDTH)),
                  _const_spec((1, SGU_WIDTH)),
                  _const_spec((SGU_GROUPS, CHUNK, CHUNK)),
                  _const_spec((SGU_GROUPS, CHUNK, 1)),
                  _const_spec((SGU_WIDTH, D_MODEL)),
                  _const_spec((SGU_WIDTH, D_MODEL))],
        out_specs=[tile, vlast_spec],
        out_shape=[jax.ShapeDtypeStruct((T_PAD, D_MODEL), F32),
                   jax.ShapeDtypeStruct(((BATCH + 1) * CHUNK, SGU_WIDTH), F32)],
        scratch_shapes=[pltpu.VMEM((TM, SGU_WIDTH), BF16)],
        compiler_params=_params(),
        name="sgu_mixer",
    )(*row_args, g.reshape(1, -1), *_hi_lo(w_in, precise), b_in.reshape(1, -1), ln_g.reshape(1, -1),
      ln_b.reshape(1, -1), w_s, b_s.reshape(SGU_GROUPS, CHUNK, 1), *_hi_lo(w_out, precise))


def _ffn_kernel(x_ref, g_ref, wg_hi, wg_lo, wu_hi, wu_lo, wd_hi, wd_lo, xo_ref, *, precise):
    is_sample = pl.program_id(0) == N_TILES - 1

    def ffn(x, sample):
        h = _operand(_rms(x, g_ref[...]), sample and precise)
        a = _silu(_mm(h, wg_hi[...], wg_lo[...])) * _mm(h, wu_hi[...], wu_lo[...])
        return x + _mm(_operand(a, sample and precise), wd_hi[...], wd_lo[...])

    @pl.when(jnp.logical_not(is_sample))
    def _():
        xo_ref[...] = ffn(x_ref[...], sample=False)

    @pl.when(is_sample)
    def _():
        xo_ref[:DEC_BATCH, :] = ffn(x_ref[:DEC_BATCH, :], sample=True)
        xo_ref[DEC_BATCH:, :] = jnp.zeros((TM - DEC_BATCH, D_MODEL), F32)


def _ffn_layer(x, precise, g, w_gate, w_up, w_down):
    tile = pl.BlockSpec((TM, D_MODEL), lambda i: (i, 0))
    return pl.pallas_call(
        functools.partial(_ffn_kernel, precise=precise),
        grid=(N_TILES,),
        in_specs=[tile, _const_spec((1, D_MODEL))]
        + [_const_spec((D_MODEL, D_FF))] * 4 + [_const_spec((D_FF, D_MODEL))] * 2,
        out_specs=tile,
        out_shape=jax.ShapeDtypeStruct((T_PAD, D_MODEL), F32),
        compiler_params=_params(),
        name="dense_swiglu",
    )(x, g.reshape(1, -1), *_hi_lo(w_gate, precise), *_hi_lo(w_up, precise), *_hi_lo(w_down, precise))


def _qkv_kernel(x_ref, g_ref, w_hi, w_lo, q_ref, qs_ref, kv_ref, kvlast_ref, *, precise):
    is_sample = pl.program_id(0) == N_TILES - 1

    def qkv(x, sample):
        out = _mm(_operand(_rms(x, g_ref[...]), sample and precise), w_hi[...], w_lo[...])
        return out[:, :Q_DIM] * (HEAD_DIM ** -0.5), out[:, Q_DIM:]

    @pl.when(jnp.logical_not(is_sample))
    def _():
        q, kv = qkv(x_ref[...], sample=False)
        q_ref[...] = q.astype(BF16)
        kv_ref[...] = kv
        kvlast_ref[...] = kv[TM - WINDOW:]

    @pl.when(is_sample)
    def _():
        q, kv = qkv(x_ref[:DEC_BATCH, :], sample=True)
        qs_ref[...] = q
        kv_ref[:DEC_BATCH, :] = kv
        kv_ref[DEC_BATCH:, :] = jnp.zeros((TM - DEC_BATCH, 2 * KV_DIM), F32)
        kvlast_ref[...] = kv


def _qkv_layer(x, precise, g, w_qkv):
    tile = pl.BlockSpec((TM, D_MODEL), lambda i: (i, 0))
    return pl.pallas_call(
        functools.partial(_qkv_kernel, precise=precise),
        grid=(N_TILES,),
        in_specs=[tile, _const_spec((1, D_MODEL)),
                  _const_spec((D_MODEL, Q_DIM + 2 * KV_DIM)), _const_spec((D_MODEL, Q_DIM + 2 * KV_DIM))],
        out_specs=[pl.BlockSpec((TM, Q_DIM), lambda i: (jnp.minimum(i, PROMPT_TILES - 1), 0)),
                   pl.BlockSpec((DEC_BATCH, Q_DIM), lambda i: (0, 0)),
                   pl.BlockSpec((TM, 2 * KV_DIM), lambda i: (i, 0)),
                   pl.BlockSpec((WINDOW, 2 * KV_DIM),
                                lambda i: (jnp.where(i == N_TILES - 1, BATCH, i // TILES_PER_SEQ), 0))],
        out_shape=[jax.ShapeDtypeStruct((N_PROMPT, Q_DIM), BF16),
                   jax.ShapeDtypeStruct((DEC_BATCH, Q_DIM), F32),
                   jax.ShapeDtypeStruct((T_PAD, 2 * KV_DIM), F32),
                   jax.ShapeDtypeStruct(((BATCH + 1) * WINDOW, 2 * KV_DIM), F32)],
        compiler_params=_params(),
        name="swa_qkv",
    )(x, g.reshape(1, -1), *_hi_lo(w_qkv, precise))


def _half_masks_f32(x):
    lane = lax.broadcasted_iota(jnp.int32, x.shape, 1)
    low = lane < HEAD_DIM
    xr = pltpu.roll(x, HEAD_DIM, 1)
    return ((jnp.where(low, x, 0.0), jnp.where(low, 0.0, xr)),
            (jnp.where(low, xr, 0.0), jnp.where(low, 0.0, x)))


def _half_masks(x):
    return tuple(tuple(m.astype(BF16) for m in pair) for pair in _half_masks_f32(x))


def _swa_prompt_kernel(sink_ref, q_ref, kv_ref, kvp_ref, wg_ref, wu_ref, wd_ref,
                       o_ref, wg_out, wu_out, wd_out, bias_ref):
    wg_out[...] = wg_ref[...].astype(BF16)
    wu_out[...] = wu_ref[...].astype(BF16)
    wd_out[...] = wd_ref[...].astype(BF16)
    i = pl.program_id(0)
    t = lax.broadcasted_iota(jnp.int32, (BLOCK, BLOCK), 0)
    c = lax.broadcasted_iota(jnp.int32, (BLOCK, BLOCK), 1)
    own = c <= t
    diag = c == t

    @pl.when(i == 0)
    def _():
        dist = jnp.where(own, t - c, BLOCK + t - c).astype(F32)
        for hd in range(N_HEADS):
            bias_ref[hd] = _SLOPES[hd] * dist

    first_tile = (i % TILES_PER_SEQ) == 0
    kv_all = jnp.concatenate([kvp_ref[...], kv_ref[...]], axis=0)
    k_blk, v_blk, vf_blk = [], [], []
    for b in range(CHUNKS_PER_TILE + 1):
        blk = kv_all[b * BLOCK:(b + 1) * BLOCK]
        k_blk.append(_half_masks(blk[:, :KV_DIM]))
        v_blk.append(_half_masks(blk[:, KV_DIM:]))
        vf_blk.append(_half_masks_f32(blk[:, KV_DIM:]))
    has_prev = (jnp.zeros((BLOCK, BLOCK), jnp.int32) + jnp.where(first_tile, 0, 1)) == 1

    for b in range(CHUNKS_PER_TILE):
        rows = slice(b * BLOCK, (b + 1) * BLOCK)
        for p in range(N_PAIRS):
            kvh = (2 * p) // GQA_GROUP
            qp = q_ref[rows, p * LANES:(p + 1) * LANES]
            acc = None
            for par in range(2):
                hd = 2 * p + par
                sink = sink_ref[hd]
                keys = jnp.concatenate([k_blk[b][kvh][par], k_blk[b + 1][kvh][par]], axis=0)
                both = _dot_t(qp, keys)
                l_prev, l_own = both[:, :BLOCK], both[:, BLOCK:]
                if b == 0:
                    l_prev = jnp.where(has_prev, l_prev, -jnp.inf)
                logits = jnp.where(own, l_own, l_prev) - bias_ref[hd]
                extra = (jnp.sum(jnp.where(diag, l_prev, 0.0), axis=-1, keepdims=True)
                         - _SLOPES[hd] * BLOCK)
                m = jnp.maximum(jnp.maximum(jnp.max(logits, axis=-1, keepdims=True), extra), sink)
                e = jnp.exp(logits - m)
                e_extra = jnp.exp(extra - m)
                denom = jnp.sum(e, axis=-1, keepdims=True) + e_extra + jnp.exp(sink - m)
                probs = jnp.concatenate([jnp.where(own, 0.0, e).astype(BF16),
                                         jnp.where(own, e, 0.0).astype(BF16)], axis=1)
                vals = jnp.concatenate([v_blk[b][kvh][par], v_blk[b + 1][kvh][par]], axis=0)
                part = (_dot(probs, vals) + e_extra * vf_blk[b][kvh][par]) * (1.0 / denom)
                acc = part if acc is None else acc + part
            o_ref[rows, p * LANES:(p + 1) * LANES] = acc.astype(BF16)


def _swa_prompt(q, kv, sinks, layer, w_gate, w_up, w_down):
    n_layers = w_gate.shape[0]
    up_rows = N_EXPERTS * D_MODEL // PROMPT_TILES
    down_rows = N_EXPERTS * D_FF_EXPERT // PROMPT_TILES
    up_spec = pl.BlockSpec((None, up_rows, D_FF_EXPERT), lambda i: (layer, i, 0))
    down_spec = pl.BlockSpec((None, down_rows, D_MODEL), lambda i: (layer, i, 0))
    o, wg, wu, wd = pl.pallas_call(
        _swa_prompt_kernel,
        grid=(PROMPT_TILES,),
        in_specs=[pl.BlockSpec(memory_space=pltpu.SMEM),
                  pl.BlockSpec((TM, Q_DIM), lambda i: (i, 0)),
                  pl.BlockSpec((TM, 2 * KV_DIM), lambda i: (i, 0)),
                  pl.BlockSpec((BLOCK, 2 * KV_DIM),
                               lambda i: (jnp.maximum(i * CHUNKS_PER_TILE - 1, 0), 0)),
                  up_spec, up_spec, down_spec],
        out_specs=[pl.BlockSpec((TM, Q_DIM), lambda i: (i, 0)),
                   pl.BlockSpec((up_rows, D_FF_EXPERT), lambda i: (i, 0)),
                   pl.BlockSpec((up_rows, D_FF_EXPERT), lambda i: (i, 0)),
                   pl.BlockSpec((down_rows, D_MODEL), lambda i: (i, 0))],
        out_shape=[jax.ShapeDtypeStruct((N_PROMPT, Q_DIM), BF16),
                   jax.ShapeDtypeStruct((N_EXPERTS * D_MODEL, D_FF_EXPERT), BF16),
                   jax.ShapeDtypeStruct((N_EXPERTS * D_MODEL, D_FF_EXPERT), BF16),
                   jax.ShapeDtypeStruct((N_EXPERTS * D_FF_EXPERT, D_MODEL), BF16)],
        scratch_shapes=[pltpu.VMEM((N_HEADS, BLOCK, BLOCK), F32)],
        compiler_params=_params(),
        name="swa_prompt",
    )(sinks, q, kv, kv,
      w_gate.reshape(n_layers, N_EXPERTS * D_MODEL, D_FF_EXPERT),
      w_up.reshape(n_layers, N_EXPERTS * D_MODEL, D_FF_EXPERT),
      w_down.reshape(n_layers, N_EXPERTS * D_FF_EXPERT, D_MODEL))
    return (o, wg.reshape(N_EXPERTS, D_MODEL, D_FF_EXPERT), wu.reshape(N_EXPERTS, D_MODEL, D_FF_EXPERT),
            wd.reshape(N_EXPERTS, D_FF_EXPERT, D_MODEL))


SAMPLE_TILE = 32


def _swa_sample_kernel(q_ref, kvn_ref, ck_ref, cv_ref, slope_ref, sink_ref, o_ref, *, precise):
    shape = (SAMPLE_TILE, N_PAIRS, LANES)
    lane = lax.broadcasted_iota(jnp.int32, shape, 2)
    pair = lax.broadcasted_iota(jnp.int32, shape, 1)
    low = lane < HEAD_DIM
    kv0 = pair < (N_PAIRS // 2)

    def swap(x):
        return pltpu.roll(x, HEAD_DIM, 2)

    def three_pass(dims, a, b):
        dot = lambda u, v: lax.dot_general(u, v, dims, preferred_element_type=F32)
        if not precise:
            return dot(a[0], b[0])
        return dot(a[0], b[0]) + (dot(a[1], b[0]) + dot(a[0], b[1]))

    qk_dims = (((2,), (2,)), ((0,), (0,)))
    pv_dims = (((2,), (1,)), ((0,), (0,)))

    q = q_ref[...]
    q_even = jnp.where(low, q, 0.0)
    q_odd = jnp.where(low, 0.0, q)
    q_al = (jnp.where(kv0, q_even, swap(q_even)), jnp.where(kv0, swap(q_odd), q_odd))

    ck = _split(ck_ref[...])
    cv = _split(cv_ref[...])
    kvn = kvn_ref[...]
    k_new = kvn[:, :, :KV_DIM]
    v_new = kvn[:, :, KV_DIM:]
    r = lax.broadcasted_iota(jnp.int32, (SAMPLE_TILE, N_PAIRS, WINDOW), 2)
    dist = (WINDOW - r).astype(F32)

    outs = []
    for par in range(2):
        qa = q_al[par]
        slope = slope_ref[par]
        sink = sink_ref[par]
        logits = three_pass(qk_dims, _split(qa), ck) - slope * dist
        l_self = jnp.sum(qa * k_new, axis=-1, keepdims=True)
        m = jnp.maximum(jnp.maximum(jnp.max(logits, axis=-1, keepdims=True), l_self), sink)
        e = jnp.exp(logits - m)
        e_self = jnp.exp(l_self - m)
        inv = 1.0 / (jnp.sum(e, axis=-1, keepdims=True) + e_self + jnp.exp(sink - m))
        o = three_pass(pv_dims, _split(e * inv), cv) + (e_self * inv) * v_new
        outs.append(o)
    o_even = jnp.where(kv0, outs[0], swap(outs[0]))
    o_odd = jnp.where(kv0, swap(outs[1]), outs[1])
    o_ref[...] = jnp.where(low, o_even, o_odd)


def _swa_sample(q_s, kv_s, cache_k, cache_v, sinks, precise):
    rows = cache_k.shape[1]
    slopes = np.asarray(_SLOPES, np.float32).reshape(N_PAIRS, 2).T.reshape(2, N_PAIRS, 1)
    sink_arr = sinks.astype(F32).reshape(N_PAIRS, 2).T.reshape(2, N_PAIRS, 1)
    blk = lambda *shape: pl.BlockSpec((SAMPLE_TILE,) + shape, lambda i: (i,) + (0,) * len(shape))
    o3 = pl.pallas_call(
        functools.partial(_swa_sample_kernel, precise=precise),
        grid=(DEC_BATCH // SAMPLE_TILE,),
        in_specs=[blk(N_PAIRS, LANES), blk(1, 2 * KV_DIM), blk(rows, KV_DIM), blk(rows, KV_DIM),
                  _const_spec((2, N_PAIRS, 1)), _const_spec((2, N_PAIRS, 1))],
        out_specs=blk(N_PAIRS, LANES),
        out_shape=jax.ShapeDtypeStruct((DEC_BATCH, N_PAIRS, LANES), F32),
        compiler_params=_params(),
        name="swa_sample",
    )(q_s.reshape(DEC_BATCH, N_PAIRS, LANES), kv_s.reshape(DEC_BATCH, 1, 2 * KV_DIM),
      cache_k.reshape(DEC_BATCH, rows, KV_DIM), cache_v.reshape(DEC_BATCH, rows, KV_DIM),
      jnp.asarray(slopes), sink_arr)
    return o3.reshape(DEC_BATCH, Q_DIM)


def _proj_route_kernel(x_ref, op_ref, os_ref, wo_hi, wo_lo, g_ref, wr_hi, wr_lo, xo_ref, idx_ref,
                       gate_ref, rank_ref, cnt_ref, carry_ref, xnew_ref, *, precise):
    i = pl.program_id(0)
    is_sample = i == N_TILES - 1

    @pl.when(i == 0)
    def _():
        carry_ref[...] = jnp.zeros_like(carry_ref)

    @pl.when(jnp.logical_not(is_sample))
    def _():
        xnew_ref[...] = x_ref[...] + _dot(op_ref[...], wo_hi[...])

    @pl.when(is_sample)
    def _():
        xnew_ref[:DEC_BATCH, :] = x_ref[:DEC_BATCH, :] + _mm(_operand(os_ref[...], precise), wo_hi[...], wo_lo[...])
        xnew_ref[DEC_BATCH:, :] = jnp.zeros((TM - DEC_BATCH, D_MODEL), F32)

    x = xnew_ref[...]
    _store_row_tiles(xo_ref, x)
    h = _rms(x, g_ref[...])
    logits = _mm(_split(h), wr_hi[...], wr_lo[...])
    lane = lax.broadcasted_iota(jnp.int32, (TM, LANES), 1)
    logits = jnp.where(lane < N_EXPERTS, logits, -jnp.inf)
    m0 = jnp.max(logits, axis=-1, keepdims=True)
    i0 = jnp.min(jnp.where(logits == m0, lane, LANES), axis=-1, keepdims=True)
    rest = jnp.where(lane == i0, -jnp.inf, logits)
    m1 = jnp.max(rest, axis=-1, keepdims=True)
    i1 = jnp.min(jnp.where(rest == m1, lane, LANES), axis=-1, keepdims=True)
    e1 = jnp.exp(m1 - m0)
    g0 = 1.0 / (1.0 + e1)
    g1 = e1 * g0
    idx_ref[...] = jnp.concatenate([i0, i1], axis=1)
    gate_ref[...] = jnp.concatenate([g0, g1], axis=1)

    row_id = i * TM + lax.broadcasted_iota(jnp.int32, (TM, 1), 0)
    onehot = jnp.where(((lane == i0) | (lane == i1)) & (row_id < T_REAL), 1.0, 0.0)
    r = lax.broadcasted_iota(jnp.int32, (TM, TM), 0)
    c = lax.broadcasted_iota(jnp.int32, (TM, TM), 1)
    before = jnp.where(c < r, 1.0, 0.0).astype(BF16)
    ranks = _dot(before, onehot.astype(BF16)) + carry_ref[...]
    r0 = jnp.sum(jnp.where(lane == i0, ranks, 0.0), axis=-1, keepdims=True)
    r1 = jnp.sum(jnp.where(lane == i1, ranks, 0.0), axis=-1, keepdims=True)
    rank_ref[...] = jnp.concatenate([r0, r1], axis=1).astype(jnp.int32)
    carry_ref[...] = carry_ref[...] + jnp.sum(onehot, axis=0, keepdims=True)
    cnt_ref[...] = carry_ref[...].astype(jnp.int32)


def _proj_route(x, o_prompt, o_sample, precise, w_o, g, w_router):
    tile = pl.BlockSpec((TM, D_MODEL), lambda i: (i, 0))
    pair = pl.BlockSpec((TM, 2), lambda i: (i, 0))
    wr = jnp.zeros((D_MODEL, LANES), F32).at[:, :N_EXPERTS].set(w_router)
    return pl.pallas_call(
        functools.partial(_proj_route_kernel, precise=precise),
        grid=(N_TILES,),
        in_specs=[tile,
                  pl.BlockSpec((TM, Q_DIM), lambda i: (jnp.minimum(i, PROMPT_TILES - 1), 0)),
                  pl.BlockSpec((DEC_BATCH, Q_DIM), lambda i: (0, 0)),
                  _const_spec((Q_DIM, D_MODEL)), _const_spec((Q_DIM, D_MODEL)),
                  _const_spec((1, D_MODEL)), _const_spec((D_MODEL, LANES)),
                  _const_spec((D_MODEL, LANES))],
        out_specs=[pl.BlockSpec((TM * SUBLANES, LANES), lambda i: (i, 0)), pair, pair, pair,
                   pl.BlockSpec((1, LANES), lambda i: (0, 0))],
        out_shape=[jax.ShapeDtypeStruct((T_PAD * SUBLANES, LANES), F32),
                   jax.ShapeDtypeStruct((T_PAD, 2), jnp.int32),
                   jax.ShapeDtypeStruct((T_PAD, 2), F32),
                   jax.ShapeDtypeStruct((T_PAD, 2), jnp.int32),
                   jax.ShapeDtypeStruct((1, LANES), jnp.int32)],
        scratch_shapes=[pltpu.VMEM((1, LANES), F32), pltpu.VMEM((TM, D_MODEL), F32)],
        compiler_params=_params(),
        name="proj_route",
    )(x, o_prompt, o_sample, *_hi_lo(w_o, precise), g.reshape(1, -1), *_hi_lo(wr))


def _tile_at(start):
    return pl.ds(start if isinstance(start, int) else pl.multiple_of(start, SUBLANES), SUBLANES)


def _gather_row(x_hbm, xbuf, sem, slot, src, r):
    return pltpu.make_async_copy(x_hbm.at[_tile_at(src), :], xbuf.at[slot, _tile_at(r * SUBLANES), :],
                                 sem.at[slot])


def _scatter_row(obuf, y_hbm, sem, slot, r, dst):
    return pltpu.make_async_copy(obuf.at[slot, _tile_at(r * SUBLANES), :], y_hbm.at[_tile_at(dst), :],
                                 sem.at[slot])


def _gather_wait(x_hbm, xbuf, sem, slot):
    pltpu.make_async_copy(x_hbm.at[pl.ds(0, TMM * SUBLANES), :], xbuf.at[slot], sem.at[slot]).wait()


def _scatter_wait(obuf, y_hbm, sem, slot):
    pltpu.make_async_copy(obuf.at[slot], y_hbm.at[pl.ds(0, TMM * SUBLANES), :], sem.at[slot]).wait()


def _slot_src(slot_ref, q):
    return slot_ref[q]


def _slot_dst(slot_ref, q):
    return slot_ref[SLOT_ENTRIES + q]


def _moe_kernel(te_ref, nu_ref, slot_ref, x_hbm, g_ref, wg_ref, wu_ref, wd_ref, y_hbm,
                xbuf, hbuf, obuf, gsem, ssem):
    i = pl.program_id(0)
    n_used = nu_ref[0]
    slot = i % 2
    other = 1 - slot

    @pl.when(i < n_used)
    def _():
        @pl.when(i == 0)
        def _():
            def start(r, carry):
                _gather_row(x_hbm, xbuf, gsem, 0, _slot_src(slot_ref, TMM + r), r).start()
                return carry
            lax.fori_loop(0, TMM, start, 0)
            obuf[...] = jnp.zeros_like(obuf)
            n_pad = (T_PAD - T_REAL) * SUBLANES
            for s in range(2):
                pad = pltpu.make_async_copy(
                    obuf.at[s, pl.ds(0, n_pad), :],
                    y_hbm.at[pl.ds((s * T_PAD + T_REAL) * SUBLANES, n_pad), :],
                    ssem.at[s])
                pad.start()
                pad.wait()
                spare = pltpu.make_async_copy(
                    obuf.at[s],
                    y_hbm.at[pl.ds((2 * T_PAD + s * TMM) * SUBLANES, TMM * SUBLANES), :],
                    ssem.at[s])
                spare.start()
                if s == 1:
                    spare.wait()

        _gather_wait(x_hbm, xbuf, gsem, slot)
        xin = xbuf.at[slot]
        chunks = [xin[_chunk(s, TMM), :] for s in range(ROW_CHUNKS)]
        ssq = chunks[0] * chunks[0]
        for c in chunks[1:]:
            ssq = ssq + c * c
        scale = lax.rsqrt(jnp.sum(ssq, axis=-1, keepdims=True) * (1.0 / D_MODEL) + EPS)
        for s, c in enumerate(chunks):
            cols = slice(s * LANES, (s + 1) * LANES)
            hbuf[:, cols] = (c * scale * g_ref[:, cols]).astype(BF16)

        nxt = (i + 2) * TMM
        prv = i * TMM
        for r in range(TMM):
            _gather_row(x_hbm, xbuf, gsem, other, _slot_src(slot_ref, nxt + r), r).start(priority=r % 2)
            _scatter_row(obuf, y_hbm, ssem, other, r, _slot_dst(slot_ref, prv + r)).start(priority=r % 2)

        h = hbuf[...]
        a = (_silu(_dot(h, wg_ref[...])) * _dot(h, wu_ref[...])).astype(BF16)
        out = _dot(a, wd_ref[...])
        _scatter_wait(obuf, y_hbm, ssem, slot)
        _store_row_tiles(obuf.at[slot], out)

        @pl.when(i == n_used - 1)
        def _():
            _gather_wait(x_hbm, xbuf, gsem, other)
            _scatter_wait(obuf, y_hbm, ssem, other)
            base = (i + 1) * TMM

            def start(r, carry):
                _scatter_row(obuf, y_hbm, ssem, slot, r, _slot_dst(slot_ref, base + r)).start()
                return carry
            lax.fori_loop(0, TMM, start, 0)
            _scatter_wait(obuf, y_hbm, ssem, slot)


def _moe_layer(x, g, tile_expert, n_used, slots, w_gate, w_up, w_down):
    def expert_weights(shape, buffers=1):
        return pl.BlockSpec((None,) + shape, lambda i, te, nu, tok: (te[i], 0, 0),
                            pipeline_mode=pl.Buffered(buffers))

    grid_spec = pltpu.PrefetchScalarGridSpec(
        num_scalar_prefetch=3,
        grid=(NT_MOE,),
        in_specs=[pl.BlockSpec(memory_space=pl.ANY),
                  pl.BlockSpec((1, D_MODEL), lambda i, te, nu, tok: (0, 0)),
                  expert_weights((D_MODEL, D_FF_EXPERT)),
                  expert_weights((D_MODEL, D_FF_EXPERT)),
                  expert_weights((D_FF_EXPERT, D_MODEL), buffers=2)],
        out_specs=pl.BlockSpec(memory_space=pl.ANY),
        scratch_shapes=[pltpu.VMEM((2, TMM * SUBLANES, LANES), F32),
                        pltpu.VMEM((TMM, D_MODEL), BF16),
                        pltpu.VMEM((2, TMM * SUBLANES, LANES), F32),
                        pltpu.SemaphoreType.DMA((2,)),
                        pltpu.SemaphoreType.DMA((2,))],
    )
    return pl.pallas_call(
        _moe_kernel,
        grid_spec=grid_spec,
        out_shape=jax.ShapeDtypeStruct((Y_ROWS * SUBLANES, LANES), F32),
        compiler_params=_params(1),
        name="moe_experts",
    )(tile_expert, n_used, slots, x, g.reshape(1, -1), w_gate, w_up, w_down)


def _final_kernel(x_ref, gate_ref, y0_ref, y1_ref, gf_ref, yp_ref, ys_ref):
    i = pl.program_id(0)
    y = _rms(_combined_rows(x_ref, gate_ref, y0_ref, y1_ref), gf_ref[...])

    @pl.when(i < PROMPT_TILES)
    def _():
        yp_ref[...] = y

    @pl.when(i == N_TILES - 1)
    def _():
        ys_ref[...] = y[:DEC_BATCH]


def _final_combine(x, gates, y_slots, g_final):
    return pl.pallas_call(
        _final_kernel,
        grid=(N_TILES,),
        in_specs=_row_tile_specs() + [_const_spec((1, D_MODEL))],
        out_specs=[pl.BlockSpec((TM, D_MODEL), lambda i: (jnp.minimum(i, PROMPT_TILES - 1), 0)),
                   pl.BlockSpec((DEC_BATCH, D_MODEL), lambda i: (0, 0))],
        out_shape=[jax.ShapeDtypeStruct((N_PROMPT, D_MODEL), F32),
                   jax.ShapeDtypeStruct((DEC_BATCH, D_MODEL), F32)],
        compiler_params=_params(),
        name="final_combine",
    )(x, gates, y_slots, y_slots, g_final.reshape(1, -1))


def _padding_slot_entries():
    p = np.arange(-TMM, P_TOTAL + TMM)
    spare_row = ((p // TMM) % 2) * TMM + p % TMM
    return np.concatenate([spare_row, 2 * T_PAD + spare_row]).astype(np.int32) * SUBLANES


FILL_UNROLL = 8


def _slot_fill_kernel(pos_ref, init_hbm, slots_hbm, buf, sem):
    load = pltpu.make_async_copy(init_hbm, buf, sem)
    load.start()
    load.wait()

    def body(j, carry):
        for u in range(FILL_UNROLL):
            t = j * FILL_UNROLL + u
            for k in range(2):
                q = TMM + pos_ref[2 * t + k]
                buf[q] = t * SUBLANES
                buf[SLOT_ENTRIES + q] = (k * T_PAD + t) * SUBLANES
        return carry

    lax.fori_loop(0, T_REAL // FILL_UNROLL, body, 0)
    store = pltpu.make_async_copy(buf, slots_hbm, sem)
    store.start()
    store.wait()


def _slot_fill(pos):
    return pl.pallas_call(
        _slot_fill_kernel,
        in_specs=[pl.BlockSpec(memory_space=pltpu.SMEM), pl.BlockSpec(memory_space=pl.ANY)],
        out_specs=pl.BlockSpec(memory_space=pl.ANY),
        out_shape=jax.ShapeDtypeStruct((2 * SLOT_ENTRIES,), jnp.int32),
        scratch_shapes=[pltpu.SMEM((2 * SLOT_ENTRIES,), jnp.int32), pltpu.SemaphoreType.DMA(())],
        name="slot_fill",
    )(pos.reshape(-1), jnp.asarray(_padding_slot_entries()))


def _slot_plan(idx, rank, counts):
    cnt = counts[0, :N_EXPERTS]
    tiles = (cnt + TMM - 1) // TMM
    tile_end = jnp.cumsum(tiles)
    start = (tile_end - tiles) * TMM
    n_used = tile_end[-1:].astype(jnp.int32)
    pos = (start[idx] + rank).astype(jnp.int32)
    slots = _slot_fill(pos)
    tile_ids = jnp.arange(NT_MOE, dtype=jnp.int32)
    tile_expert = jnp.minimum(jnp.sum(tile_end[None, :] <= tile_ids[:, None], axis=1),
                              N_EXPERTS - 1).astype(jnp.int32)
    last_expert = tile_expert[jnp.maximum(n_used[0] - 1, 0)]
    tile_expert = jnp.where(tile_ids < n_used[0], tile_expert, last_expert)
    return slots, tile_expert, n_used


def kernel(x_prompt, x_sample, cache_swa_k, cache_swa_v, norm_mix, norm_ffn, norm_final,
           sgu_w_in, sgu_b_in, sgu_ln_g, sgu_ln_b, sgu_w_s, sgu_b_s, sgu_w_out,
           attn_w_qkv, attn_sinks, attn_w_o,
           ffn_w_gate, ffn_w_up, ffn_w_down,
           moe_w_router, moe_w_gate, moe_w_up, moe_w_down):
    sgu_v_p, sgu_v_s, k_p, v_p, k_s, v_s = [], [], [], [], [], []
    rows_p = min(WINDOW, SEQ)
    rows, source = (x_prompt.reshape(N_PROMPT, D_MODEL), x_sample.reshape(DEC_BATCH, D_MODEL)), "inputs"
    for i in range(DEPTH):
        j = i // 2
        precise = i <= 1
        if i % 2 == 0:
            x, vlast = _sgu_layer(rows, source, precise, norm_mix[i], sgu_w_in[j], sgu_b_in[j], sgu_ln_g[j],
                                  sgu_ln_b[j], sgu_w_s[j], sgu_b_s[j], sgu_w_out[j])
            sgu_v_p.append(vlast[:BATCH * CHUNK].reshape(BATCH, CHUNK, SGU_WIDTH))
            sgu_v_s.append(vlast[BATCH * CHUNK:].reshape(DEC_BATCH, 1, SGU_WIDTH))
            x = _ffn_layer(x, precise, norm_ffn[i], ffn_w_gate[j], ffn_w_up[j], ffn_w_down[j])
        else:
            q, q_s, kv, kv_last = _qkv_layer(x, precise, norm_mix[i], attn_w_qkv[j])
            kv_p = kv_last[:BATCH * rows_p].reshape(BATCH, rows_p, 2 * KV_DIM)
            k_p.append(kv_p[..., :KV_DIM].reshape(BATCH, rows_p, N_KV_HEADS, HEAD_DIM))
            v_p.append(kv_p[..., KV_DIM:].reshape(BATCH, rows_p, N_KV_HEADS, HEAD_DIM))
            kv_s = kv_last[BATCH * rows_p:]
            k_new = kv_s[:, None, :KV_DIM].reshape(DEC_BATCH, 1, N_KV_HEADS, HEAD_DIM)
            v_new = kv_s[:, None, KV_DIM:].reshape(DEC_BATCH, 1, N_KV_HEADS, HEAD_DIM)
            k_s.append(jnp.concatenate([cache_swa_k[j][:, 1:], k_new], axis=1))
            v_s.append(jnp.concatenate([cache_swa_v[j][:, 1:], v_new], axis=1))

            o, moe_wg, moe_wu, moe_wd = _swa_prompt(q, kv, attn_sinks[j].astype(F32), j,
                                                    moe_w_gate, moe_w_up, moe_w_down)
            o_s = _swa_sample(q_s, kv_s, cache_swa_k[j], cache_swa_v[j], attn_sinks[j], precise)
            x, idx, gates, rank, counts = _proj_route(x, o, o_s, precise, attn_w_o[j], norm_ffn[i],
                                                      moe_w_router[j])
            slots, tile_expert, n_used = _slot_plan(idx, rank, counts)
            y_slots = _moe_layer(x, norm_ffn[i], tile_expert, n_used, slots, moe_wg, moe_wu, moe_wd)
            rows, source = (x, gates, y_slots), "experts"
    y_prompt, y_sample = _final_combine(*rows, norm_final)
    y_prompt = y_prompt.reshape(BATCH, SEQ, D_MODEL)
    y_sample = y_sample.reshape(DEC_BATCH, 1, D_MODEL)
    return (y_prompt, y_sample, jnp.stack(sgu_v_p), jnp.stack(sgu_v_s),
            jnp.stack(k_p), jnp.stack(v_p), jnp.stack(k_s), jnp.stack(v_s))
```
